```python
import math
import jax, jax.numpy as jnp
from jax import lax
import numpy as np

D_MODEL = 1024
BATCH = 2
SEQ = 8192
DEPTH = 2

CHUNK = 64
D_MIX = D_MODEL
A_HEADS = 6
A_HEAD_DIM = 64
A_WIDTH = A_HEADS * A_HEAD_DIM
A_W_LORA = 32
A_A_LORA = 32
A_G_LORA = 64
A_V_LORA = 16
RWKV_GN_EPS = 64e-5
B_HEADS = 6
B_NOPE = 64
B_ROPE = 32
B_QK = B_NOPE + B_ROPE
B_V = 64
B_WIDTH = B_HEADS * B_V
B_Q_LORA = 256
B_KV_LORA = 128
ROPE_THETA = 10000.0
Q_BLOCK = 128
C_GROUPS = 4
C_GROUP_DIM = 64
C_WIDTH = C_GROUPS * C_GROUP_DIM
C_BLOCK = 128
A_SPLITS = (A_WIDTH, 2 * A_WIDTH, 3 * A_WIDTH, 3 * A_WIDTH + A_W_LORA, 3 * A_WIDTH + A_W_LORA + A_A_LORA)
A_COLS = 3 * A_WIDTH + A_W_LORA + A_A_LORA + A_G_LORA
B_COLS = B_Q_LORA + B_KV_LORA + B_ROPE
C_COLS = 2 * C_WIDTH
IN_COLS = A_COLS + B_COLS + C_COLS
D_FF_DENSE = 2816
N_EXPERTS = 8
TOP_K = 2
D_FF_EXPERT = 3584
N_DENSE = (DEPTH + 1) // 2
N_MOE = DEPTH // 2
N_VRES = DEPTH - 1
EPS = 1e-6

kernel_name = 'hybrid_rwkv7_mla_gmlp_moe_trunk'


def rms_norm(x, g, eps=EPS):
    xf = x.astype(jnp.float32)
    y = xf * lax.rsqrt(jnp.mean(xf * xf, axis=-1, keepdims=True) + eps)
    return (y * g.astype(jnp.float32)).astype(x.dtype)


def layer_norm(x, g, b, eps=EPS):
    xf = x.astype(jnp.float32)
    mu = jnp.mean(xf, axis=-1, keepdims=True)
    var = jnp.mean(jnp.square(xf - mu), axis=-1, keepdims=True)
    y = (xf - mu) * lax.rsqrt(var + eps) * g.astype(jnp.float32) + b.astype(jnp.float32)
    return y.astype(x.dtype)


def token_shift(y, mu):
    prev = jnp.pad(y, ((0, 0), (1, 0), (0, 0)))[:, :-1]
    return y + (prev - y) * mu


def rwkv7_time_mix(za, shift_mu, w0, w_up, a0, a_up, g_up, k_k, k_a, r_k, ln_g, ln_b,
                   v_first, v0, v_down, v_up):
    B_, S_, _ = za.shape
    za = token_shift(za, shift_mu)
    r, k, v, wd, ad, gd = jnp.split(za, A_SPLITS, axis=-1)
    w_log = -jax.nn.softplus(-(w0 + jnp.tanh(wd) @ w_up)) - 0.5
    decay = jnp.exp(-jnp.exp(w_log.astype(jnp.float32)))
    a = jax.nn.sigmoid(a0 + ad @ a_up)
    g = jax.nn.sigmoid(gd) @ g_up
    if v_first is not None:
        v = v + (v_first - v) * jax.nn.sigmoid(v0 + (v @ v_down) @ v_up)
    heads = lambda t: t.reshape(B_, S_, A_HEADS, A_HEAD_DIM).astype(jnp.float32)
    kk = heads(k * k_k)
    kk = kk / jnp.maximum(jnp.sqrt(jnp.sum(kk * kk, axis=-1, keepdims=True)), 1e-12)
    k = k * (1 + (a - 1) * k_a)
    rh, kh, vh, ah, wh = heads(r), heads(k), heads(v), heads(a), heads(decay)

    def step(state, inp):
        r_t, w_t, k_t, v_t, kk_t, a_t = inp
        sa = jnp.einsum('bhvk,bhk->bhv', state, -kk_t)
        state = (state * w_t[:, :, None, :]
                 + sa[..., None] * (kk_t * a_t)[:, :, None, :]
                 + v_t[..., None] * k_t[:, :, None, :])
        return state, jnp.einsum('bhvk,bhk->bhv', state, r_t)

    xs = tuple(jnp.swapaxes(t, 0, 1) for t in (rh, wh, kh, vh, kk, ah))
    state0 = jnp.zeros((B_, A_HEADS, A_HEAD_DIM, A_HEAD_DIM), jnp.float32)
    _, y = lax.scan(step, state0, xs)
    y = jnp.swapaxes(y, 0, 1)
    mu = jnp.mean(y, axis=-1, keepdims=True)
    var = jnp.mean(jnp.square(y - mu), axis=-1, keepdims=True)
    y = (y - mu) * lax.rsqrt(var + RWKV_GN_EPS)
    y = (y * ln_g.reshape(A_HEADS, A_HEAD_DIM).astype(jnp.float32)
         + ln_b.reshape(A_HEADS, A_HEAD_DIM).astype(jnp.float32))
    y = y + jnp.sum(rh * kh * r_k.astype(jnp.float32), axis=-1, keepdims=True) * vh
    y = y.reshape(B_, S_, A_WIDTH).astype(za.dtype) * g
    return y, v


def apply_rope(t, cos, sin):
    half = B_ROPE // 2
    t1, t2 = t[..., :half], t[..., half:]
    return jnp.concatenate([t1 * cos - t2 * sin, t1 * sin + t2 * cos], axis=-1)


def mla_attention(zb, positions, q_norm_g, w_uq, kv_norm_g, w_ukv, q_head_g, k_head_g, out_g):
    B_, S_, _ = zb.shape
    cq, ckv, kr = jnp.split(zb, (B_Q_LORA, B_Q_LORA + B_KV_LORA), axis=-1)
    q = (rms_norm(cq, q_norm_g) @ w_uq).reshape(B_, S_, B_HEADS, B_QK)
    kv = (rms_norm(ckv, kv_norm_g) @ w_ukv).reshape(B_, S_, B_HEADS, B_NOPE + B_V)
    k_nope, v = kv[..., :B_NOPE], kv[..., B_NOPE:]
    k = jnp.concatenate([k_nope, jnp.broadcast_to(kr[:, :, None, :], (B_, S_, B_HEADS, B_ROPE))], axis=-1)
    q = rms_norm(q, q_head_g)
    k = rms_norm(k, k_head_g)
    inv_freq = ROPE_THETA ** (-jnp.arange(0, B_ROPE, 2, dtype=jnp.float32) / B_ROPE)
    ang = positions.astype(jnp.float32)[..., None] * inv_freq
    cos = jnp.cos(ang)[:, :, None, :].astype(q.dtype)
    sin = jnp.sin(ang)[:, :, None, :].astype(q.dtype)
    q = jnp.concatenate([q[..., :B_NOPE], apply_rope(q[..., B_NOPE:], cos, sin)], axis=-1)
    k = jnp.concatenate([k[..., :B_NOPE], apply_rope(k[..., B_NOPE:], cos, sin)], axis=-1)
    q = jnp.transpose(q, (0, 2, 1, 3))
    k = jnp.transpose(k, (0, 2, 1, 3))
    v = jnp.transpose(v, (0, 2, 1, 3))
    n_blocks = S_ // Q_BLOCK
    qb = jnp.transpose(q.reshape(B_, B_HEADS, n_blocks, Q_BLOCK, B_QK), (2, 0, 1, 3, 4))
    key_chunk = jnp.arange(S_) // CHUNK
    scale = 1.0 / math.sqrt(B_QK)

    def block(args):
        q_blk, i = args
        s = jnp.einsum('bhqd,bhkd->bhqk', q_blk, k).astype(jnp.float32) * scale
        q_chunk = (i * Q_BLOCK + jnp.arange(Q_BLOCK)) // CHUNK
        s = jnp.where(key_chunk[None, :] <= q_chunk[:, None], s, -jnp.inf)
        p = jax.nn.softmax(s, axis=-1).astype(v.dtype)
        return jnp.einsum('bhqk,bhkd->bhqd', p, v)

    o = lax.map(block, (qb, jnp.arange(n_blocks)))
    o = jnp.transpose(o, (1, 0, 3, 2, 4)).reshape(B_, S_, B_WIDTH)
    return rms_norm(o, out_g)


def gmlp_spatial_gate(zc, ln_g, ln_b, w_s, b_s, out_g):
    B_, S_, _ = zc.shape
    z = jax.nn.gelu(zc, approximate=False)
    u, v = jnp.split(z, 2, axis=-1)
    v = layer_norm(v, ln_g, ln_b)
    vb = v.reshape(B_, S_ // C_BLOCK, C_BLOCK, C_GROUPS, C_GROUP_DIM)
    tri = jnp.tril(jnp.ones((C_BLOCK, C_BLOCK), dtype=bool))
    w = jnp.where(tri[None], w_s, 0)
    sv = jnp.einsum('gts,bnsgc->bntgc', w, vb) + jnp.transpose(b_s)[None, None, :, :, None]
    y = u * sv.reshape(B_, S_, C_WIDTH)
    return rms_norm(y, out_g)


def swiglu(h, wg, wu, wd):
    return (jax.nn.silu(h @ wg) * (h @ wu)) @ wd


def moe_swiglu(h, router, wg, wu, wd):
    B_, S_, D_ = h.shape
    t = h.reshape(B_ * S_, D_)
    logits = (t @ router).astype(jnp.float32)
    top_v, top_i = lax.top_k(logits, TOP_K)
    gates = jax.nn.softmax(top_v, axis=-1)
    gate_full = jnp.sum(jax.nn.one_hot(top_i, N_EXPERTS, dtype=jnp.float32) * gates[..., None], axis=1)
    out = jnp.zeros_like(t)
    for e in range(N_EXPERTS):
        out = out + gate_full[:, e:e + 1].astype(t.dtype) * swiglu(t, wg[e], wu[e], wd[e])
    return out.reshape(B_, S_, D_)


def setup_inputs(seed: int = 0) -> dict:
    key = jax.random.key(seed)
    ks = iter(jax.random.split(key, 64))
    nrm = lambda shape, scale: jax.random.normal(next(ks), shape, jnp.float32) * scale
    gain = lambda shape: 1.0 + nrm(shape, 0.05)
    offs = jax.random.randint(next(ks), (BATCH, 1), 0, 64) * CHUNK
    positions = (offs + jnp.arange(SEQ, dtype=jnp.int32)[None, :]).astype(jnp.int32)
    return {
        'x': nrm((BATCH, SEQ, D_MODEL), 1.0),
        'positions': positions,
        'mix_norm_g': gain((DEPTH, D_MODEL)),
        'w_in': nrm((DEPTH, D_MODEL, IN_COLS), D_MODEL ** -0.5),
        'shift_mu': jax.random.uniform(next(ks), (DEPTH, A_COLS), jnp.float32, 0.1, 0.9),
        'a_w0': jax.random.uniform(next(ks), (DEPTH, A_WIDTH), jnp.float32, -6.0, 1.0),
        'a_w_up': nrm((DEPTH, A_W_LORA, A_WIDTH), 0.1 * A_W_LORA ** -0.5),
        'a_a0': nrm((DEPTH, A_WIDTH), 0.5),
        'a_a_up': nrm((DEPTH, A_A_LORA, A_WIDTH), 0.1 * A_A_LORA ** -0.5),
        'a_g_up': nrm((DEPTH, A_G_LORA, A_WIDTH), A_G_LORA ** -0.5),
        'a_k_k': 0.85 + nrm((DEPTH, A_WIDTH), 0.1),
        'a_k_a': 1.0 + nrm((DEPTH, A_WIDTH), 0.1),
        'a_r_k': nrm((DEPTH, A_HEADS, A_HEAD_DIM), 0.1),
        'a_ln_g': gain((DEPTH, A_WIDTH)),
        'a_ln_b': nrm((DEPTH, A_WIDTH), 0.02),
        'a_v0': nrm((N_VRES, A_WIDTH), 0.5),
        'a_v_down': nrm((N_VRES, A_WIDTH, A_V_LORA), A_WIDTH ** -0.5),
        'a_v_up': nrm((N_VRES, A_V_LORA, A_WIDTH), 0.1 * A_V_LORA ** -0.5),
        'b_q_norm_g': gain((DEPTH, B_Q_LORA)),
        'b_w_uq': nrm((DEPTH, B_Q_LORA, B_HEADS * B_QK), B_Q_LORA ** -0.5),
        'b_kv_norm_g': gain((DEPTH, B_KV_LORA)),
        'b_w_ukv': nrm((DEPTH, B_KV_LORA, B_HEADS * (B_NOPE + B_V)), B_KV_LORA ** -0.5),
        'b_q_head_g': gain((DEPTH, B_QK)),
        'b_k_head_g': gain((DEPTH, B_QK)),
        'b_out_g': gain((DEPTH, B_WIDTH)),
        'c_ln_g': gain((DEPTH, C_WIDTH)),
        'c_ln_b': nrm((DEPTH, C_WIDTH), 0.02),
        'c_w_s': nrm((DEPTH, C_GROUPS, C_BLOCK, C_BLOCK), 0.5 * C_BLOCK ** -0.5),
        'c_b_s': 1.0 + nrm((DEPTH, C_GROUPS, C_BLOCK), 0.1),
        'c_out_g': gain((DEPTH, C_WIDTH)),
        'w_out': nrm((DEPTH, D_MIX, D_MODEL), D_MIX ** -0.5),
        'ffn_norm_g': gain((DEPTH, D_MODEL)),
        'dense_w_gate': nrm((N_DENSE, D_MODEL, D_FF_DENSE), D_MODEL ** -0.5),
        'dense_w_up': nrm((N_DENSE, D_MODEL, D_FF_DENSE), D_MODEL ** -0.5),
        'dense_w_down': nrm((N_DENSE, D_FF_DENSE, D_MODEL), D_FF_DENSE ** -0.5),
        'moe_router': nrm((N_MOE, D_MODEL, N_EXPERTS), D_MODEL ** -0.5),
        'moe_w_gate': nrm((N_MOE, N_EXPERTS, D_MODEL, D_FF_EXPERT), D_MODEL ** -0.5),
        'moe_w_up': nrm((N_MOE, N_EXPERTS, D_MODEL, D_FF_EXPERT), D_MODEL ** -0.5),
        'moe_w_down': nrm((N_MOE, N_EXPERTS, D_FF_EXPERT, D_MODEL), D_FF_EXPERT ** -0.5),
    }


def reference(x, positions, mix_norm_g, w_in, shift_mu, a_w0, a_w_up, a_a0, a_a_up, a_g_up,
              a_k_k, a_k_a, a_r_k, a_ln_g, a_ln_b, a_v0, a_v_down, a_v_up,
              b_q_norm_g, b_w_uq, b_kv_norm_g, b_w_ukv, b_q_head_g, b_k_head_g, b_out_g,
              c_ln_g, c_ln_b, c_w_s, c_b_s, c_out_g, w_out, ffn_norm_g,
              dense_w_gate, dense_w_up, dense_w_down, moe_router, moe_w_gate, moe_w_up, moe_w_down):
    v_first = None
    for l in range(DEPTH):
        h = rms_norm(x, mix_norm_g[l])
        z = h @ w_in[l]
        za, zb, zc = jnp.split(z, (A_COLS, A_COLS + B_COLS), axis=-1)
        if l > 0:
            vres = (a_v0[l - 1], a_v_down[l - 1], a_v_up[l - 1])
        else:
            vres = (None, None, None)
        ya, v_l = rwkv7_time_mix(za, shift_mu[l], a_w0[l], a_w_up[l], a_a0[l], a_a_up[l], a_g_up[l],
                                 a_k_k[l], a_k_a[l], a_r_k[l], a_ln_g[l], a_ln_b[l], v_first, *vres)
        if l == 0:
            v_first = v_l
        yb = mla_attention(zb, positions, b_q_norm_g[l], b_w_uq[l], b_kv_norm_g[l], b_w_ukv[l],
                           b_q_head_g[l], b_k_head_g[l], b_out_g[l])
        yc = gmlp_spatial_gate(zc, c_ln_g[l], c_ln_b[l], c_w_s[l], c_b_s[l], c_out_g[l])
        x = x + jnp.concatenate([ya, yb, yc], axis=-1) @ w_out[l]
        h = rms_norm(x, ffn_norm_g[l])
        if l % 2 == 0:
            i = l // 2
            x = x + swiglu(h, dense_w_gate[i], dense_w_up[i], dense_w_down[i])
        else:
            i = l // 2
            x = x + moe_swiglu(h, moe_router[i], moe_w_gate[i], moe_w_up[i], moe_w_down[i])
    return x
```

```python
import functools
import math

import jax
import jax.numpy as jnp
from jax import lax
from jax.experimental import pallas as pl
from jax.experimental.pallas import tpu as pltpu

F32 = jnp.float32
BF16 = jnp.bfloat16
HIGHEST = lax.Precision.HIGHEST

D_MODEL = 1024
CHUNK = 64
A_HEADS = 6
A_HEAD_DIM = 64
A_WIDTH = A_HEADS * A_HEAD_DIM
A_W_LORA = 32
A_A_LORA = 32
A_G_LORA = 64
A_COLS = 3 * A_WIDTH + A_W_LORA + A_A_LORA + A_G_LORA
RWKV_GN_EPS = 64e-5
B_HEADS = 6
B_NOPE = 64
B_ROPE = 32
B_QK = B_NOPE + B_ROPE
B_V = 64
B_WIDTH = B_HEADS * B_V
B_Q_LORA = 256
B_KV_LORA = 128
B_COLS = B_Q_LORA + B_KV_LORA + B_ROPE
ROPE_THETA = 10000.0
C_GROUPS = 4
C_GROUP_DIM = 64
C_WIDTH = C_GROUPS * C_GROUP_DIM
C_BLOCK = 128
C_COLS = 2 * C_WIDTH
N_EXPERTS = 8
EPS = 1e-6

LANES = 128
SUBLANES = 8
VMEM_LIMIT_BYTES = 56 * 1024 * 1024

SCAN_CHUNK = 64
PAIR = 2 * A_HEAD_DIM
N_PAIRS = A_WIDTH // PAIR
B_HEAD_PAD = LANES
NEG_BIG = -1e30


def _cparams(sem):
    return pltpu.CompilerParams(dimension_semantics=sem, vmem_limit_bytes=VMEM_LIMIT_BYTES)


def _dot(a, b):
    return jnp.dot(a.astype(BF16), b.astype(BF16), preferred_element_type=F32)


def _dot_nt(a, b):
    return lax.dot_general(a.astype(BF16), b.astype(BF16), (((1,), (1,)), ((), ())),
                           preferred_element_type=F32)


def _dot_f32(a, b):
    return jnp.dot(a, b, precision=HIGHEST, preferred_element_type=F32)


def _sigmoid(x):
    return 1.0 / (1.0 + jnp.exp(-x))


def _const_spec(shape):
    nd = len(shape)
    return pl.BlockSpec(shape, lambda *_: (0,) * nd)


def _row_spec(tm, width):
    return pl.BlockSpec((tm, width), lambda i: (i, 0))


def _in_proj_kernel(x_ref, g_ref, wa_ref, wb_ref, wc_ref, mu_ref, za_ref, zb_ref, zc_ref, carry_ref,
                    *, tiles_per_seq):
    x = x_ref[...]
    ms = jnp.mean(x * x, axis=-1, keepdims=True)
    h = (x * lax.rsqrt(ms + EPS) * g_ref[...]).astype(BF16)
    za = jnp.dot(h, wa_ref[...], preferred_element_type=F32)
    zb_ref[...] = jnp.dot(h, wb_ref[...], preferred_element_type=F32)
    zc_ref[...] = jnp.dot(h, wc_ref[...], preferred_element_type=F32)
    tm = za.shape[0]
    first = jnp.where(pl.program_id(0) % tiles_per_seq == 0, 0.0, carry_ref[0:1, :])
    row = lax.broadcasted_iota(jnp.int32, za.shape, 0)
    prev = jnp.where(row == 0, first, pltpu.roll(za, 1, 0))
    carry_ref[0:1, :] = za[tm - 1:tm, :]
    za_ref[...] = za + (prev - za) * mu_ref[...]


def _in_proj(x, g, w_in, mu, seq_len, tm=256):
    n = x.shape[0]
    wa = w_in[:, :A_COLS].astype(BF16)
    wb = jnp.pad(w_in[:, A_COLS:A_COLS + B_COLS], ((0, 0), (0, 4 * LANES - B_COLS))).astype(BF16)
    wc = w_in[:, A_COLS + B_COLS:].astype(BF16)
    return pl.pallas_call(
        functools.partial(_in_proj_kernel, tiles_per_seq=seq_len // tm),
        out_shape=(jax.ShapeDtypeStruct((n, A_COLS), F32),
                   jax.ShapeDtypeStruct((n, 4 * LANES), F32),
                   jax.ShapeDtypeStruct((n, C_COLS), F32)),
        grid=(n // tm,),
        in_specs=[_row_spec(tm, D_MODEL), _const_spec((1, D_MODEL)), _const_spec(wa.shape),
                  _const_spec(wb.shape), _const_spec(wc.shape), _const_spec((1, A_COLS))],
        out_specs=(_row_spec(tm, A_COLS), _row_spec(tm, 4 * LANES), _row_spec(tm, C_COLS)),
        scratch_shapes=[pltpu.VMEM((SUBLANES, A_COLS), F32)],
        compiler_params=_cparams(("arbitrary",)),
        name="in_proj",
    )(x, g.reshape(1, -1), wa, wb, wc, mu.reshape(1, -1))


def _rwkv_pre_kernel(*refs, has_vres):
    if has_vres:
        (za_ref, w0_ref, wup_ref, a0_ref, aup_ref, gup_ref, kk_ref, ka_ref, rk_ref, bd_ref,
         vf_ref, v0_ref, vdn_ref, vup_ref,
         r_o, k_o, v_o, kk_o, a_o, lw_o, g_o, bonus_o) = refs
    else:
        (za_ref, w0_ref, wup_ref, a0_ref, aup_ref, gup_ref, kk_ref, ka_ref, rk_ref, bd_ref,
         r_o, k_o, v_o, kk_o, a_o, lw_o, g_o, bonus_o) = refs
    r = za_ref[:, 0:A_WIDTH]
    k = za_ref[:, A_WIDTH:2 * A_WIDTH]
    v = za_ref[:, 2 * A_WIDTH:3 * A_WIDTH]
    lora = za_ref[:, 3 * A_WIDTH:A_COLS]
    w = w0_ref[...] + _dot_f32(jnp.tanh(lora), wup_ref[...])
    w_log = -(jnp.maximum(-w, 0.0) + jnp.log(1.0 + jnp.exp(-jnp.abs(w)))) - 0.5
    lw_o[...] = -jnp.exp(w_log)
    a = _sigmoid(a0_ref[...] + _dot_f32(lora, aup_ref[...]))
    g_o[...] = _dot_f32(_sigmoid(lora), gup_ref[...])
    if has_vres:
        mix = _sigmoid(v0_ref[...] + _dot_f32(_dot_f32(v, vdn_ref[...]), vup_ref[...]))
        v = v + (vf_ref[...] - v) * mix
    kk = k * kk_ref[...]
    ss = _dot_f32(kk * kk, bd_ref[...])
    kk = kk / jnp.maximum(jnp.sqrt(ss), 1e-12)
    k = k * (1.0 + (a - 1.0) * ka_ref[...])
    r_o[...] = r
    k_o[...] = k
    v_o[...] = v
    kk_o[...] = kk
    a_o[...] = a
    bonus_o[...] = _dot_f32(r * k * rk_ref[...], bd_ref[...]) * v


def _head_block_diag(width, head_dim):
    idx = jnp.arange(width) // head_dim
    return (idx[:, None] == idx[None, :]).astype(F32)


def _pad_rows(w, row_off, rows=LANES):
    return jnp.pad(w, ((row_off, rows - row_off - w.shape[0]), (0, 0)))


def _rwkv_pre(za, w0, w_up, a0, a_up, g_up, k_k, k_a, r_k, v_first, v0, v_down, v_up, tm=256):
    n = za.shape[0]
    has_vres = v_first is not None
    row = lambda t: t.reshape(1, -1)
    args = [za, row(w0), _pad_rows(w_up, 0), row(a0), _pad_rows(a_up, A_W_LORA),
            _pad_rows(g_up, A_W_LORA + A_A_LORA), row(k_k), row(k_a), row(r_k),
            _head_block_diag(A_WIDTH, A_HEAD_DIM)]
    specs = [_row_spec(tm, A_COLS)] + [_const_spec(a.shape) for a in args[1:]]
    if has_vres:
        vdn = jnp.pad(v_down, ((0, 0), (0, LANES - v_down.shape[1])))
        extra = [v_first, row(v0), vdn, _pad_rows(v_up, 0)]
        args += extra
        specs += [_row_spec(tm, A_WIDTH)] + [_const_spec(a.shape) for a in extra[1:]]
    out = jax.ShapeDtypeStruct((n, A_WIDTH), F32)
    return pl.pallas_call(
        functools.partial(_rwkv_pre_kernel, has_vres=has_vres),
        out_shape=(out,) * 8,
        grid=(n // tm,),
        in_specs=specs,
        out_specs=(_row_spec(tm, A_WIDTH),) * 8,
        compiler_params=_cparams(("parallel",)),
        name="rwkv_pre",
    )(*args)


def _stack_heads(x, lane_lo):
    return jnp.concatenate([jnp.where(lane_lo, x, 0.0), jnp.where(lane_lo, 0.0, x)], axis=0)


def _rwkv_scan_kernel(r_ref, k_ref, v_ref, kk_ref, a_ref, lw_ref, g_ref, bonus_ref, lng_ref, lnb_ref,
                      bd_ref, y_ref, h_ref):
    n_batch, L, _ = r_ref.shape
    S2 = 2 * L

    @pl.when(pl.program_id(0) == 0)
    def _():
        h_ref[...] = jnp.zeros_like(h_ref)

    ti = lax.broadcasted_iota(jnp.int32, (L, L), 0)
    si = lax.broadcasted_iota(jnp.int32, (L, L), 1)
    tril_incl = (ti >= si).astype(F32)
    row2 = lax.broadcasted_iota(jnp.int32, (S2, S2), 0)
    col2 = lax.broadcasted_iota(jnp.int32, (S2, S2), 1)
    strict = row2 > col2
    incl = row2 >= col2
    eye = row2 == col2
    lane_lo = lax.broadcasted_iota(jnp.int32, (1, PAIR), 1) < A_HEAD_DIM
    mid = L // 2 - 1

    for b in range(n_batch):
        lw = lw_ref[b]
        cum = _dot_f32(tril_incl, lw)
        c_mid = cum[mid:mid + 1, :]
        c_end = cum[L - 1:L, :]
        kk = kk_ref[b]
        kb = kk * a_ref[b]
        kmod = k_ref[b]
        e_dn = jnp.exp(c_mid - cum)
        e_hat = jnp.exp(c_end - cum)
        at_all = -kk * jnp.exp(cum - lw - c_mid)
        bt_all = kb * e_dn
        kt_all = kmod * e_dn
        rt_all = r_ref[b] * jnp.exp(cum - c_mid)
        bh_all = kb * e_hat
        kh_all = kmod * e_hat
        p_end = jnp.exp(c_end)
        p_mid = jnp.exp(c_mid)
        at_true = at_all * p_mid
        rt_true = rt_all * p_mid
        v_all = v_ref[b]
        y_parts = []
        for p in range(N_PAIRS):
            sl = slice(p * PAIR, (p + 1) * PAIR)
            a_s = _stack_heads(at_all[:, sl], lane_lo)
            b_s = _stack_heads(bt_all[:, sl], lane_lo)
            k_s = _stack_heads(kt_all[:, sl], lane_lo)
            r_s = _stack_heads(rt_all[:, sl], lane_lo)
            bh_s = _stack_heads(bh_all[:, sl], lane_lo)
            kh_s = _stack_heads(kh_all[:, sl], lane_lo)
            v_s = _stack_heads(v_all[:, sl], lane_lo)
            gram = _dot_nt(jnp.concatenate([a_s, r_s], axis=0), jnp.concatenate([b_s, k_s], axis=0))
            a_ab = jnp.where(strict, gram[:S2, :S2], 0.0)
            a_ak = jnp.where(strict, gram[:S2, S2:], 0.0)
            a_rb = jnp.where(incl, gram[S2:, :S2], 0.0)
            a_rk = jnp.where(incl, gram[S2:, S2:], 0.0)
            x = a_ab
            t = jnp.where(eye, 1.0, a_ab)
            for _ in range(int(math.log2(L)) - 1):
                x = _dot(x, x)
                t = t + _dot(t, x)
            av = _dot(jnp.concatenate([a_ak, a_rk], axis=0), v_s)
            a_in = _stack_heads(at_true[:, sl], lane_lo)
            r_in = _stack_heads(rt_true[:, sl], lane_lo)
            tw = _dot(t, jnp.concatenate([a_in, av[:S2]], axis=1))
            qy = _dot(a_rb, tw)
            q = r_in + qy[:, :PAIR]
            y0 = qy[:, PAIR:] + av[S2:]
            mc = _dot(bh_s.T, tw)
            m = jnp.where(eye, p_end[:, sl], 0.0) + mc[:, :PAIR]
            c = mc[:, PAIR:] + _dot(kh_s.T, v_s)
            h = h_ref[b, p]
            ys = _dot(q, h) + y0
            h_ref[b, p] = _dot_f32(m, h) + c
            y_parts.append(ys[:L] + ys[L:])
        y = jnp.concatenate(y_parts, axis=1)
        mu = _dot_f32(y, bd_ref[...]) * (1.0 / A_HEAD_DIM)
        d = y - mu
        var = _dot_f32(d * d, bd_ref[...]) * (1.0 / A_HEAD_DIM)
        yn = d * lax.rsqrt(var + RWKV_GN_EPS) * lng_ref[...] + lnb_ref[...]
        y_ref[b] = (yn + bonus_ref[b]) * g_ref[b]


def _rwkv_scan(r, k, v, kk, a, lw, g, bonus, ln_g, ln_b, n_batch, seq_len):
    L = SCAN_CHUNK
    shp = (n_batch, seq_len, A_WIDTH)
    seq_spec = pl.BlockSpec((n_batch, L, A_WIDTH), lambda i: (0, i, 0))
    args = [t.reshape(shp) for t in (r, k, v, kk, a, lw, g, bonus)]
    consts = [ln_g.reshape(1, -1), ln_b.reshape(1, -1), _head_block_diag(A_WIDTH, A_HEAD_DIM)]
    y = pl.pallas_call(
        _rwkv_scan_kernel,
        out_shape=jax.ShapeDtypeStruct(shp, F32),
        grid=(seq_len // L,),
        in_specs=[seq_spec] * 8 + [_const_spec(c.shape) for c in consts],
        out_specs=seq_spec,
        scratch_shapes=[pltpu.VMEM((n_batch, N_PAIRS, PAIR, PAIR), F32)],
        compiler_params=_cparams(("arbitrary",)),
        name="rwkv_scan",
    )(*args, *consts)
    return y.reshape(n_batch * seq_len, A_WIDTH)


def _rope(t, cos_t, sin_t, lane):
    swapped = jnp.where(lane < B_NOPE + B_ROPE // 2,
                        pltpu.roll(t, LANES - B_ROPE // 2, 1), pltpu.roll(t, B_ROPE // 2, 1))
    return t * cos_t + swapped * sin_t


def _mla_pre_kernel(zb_ref, cos_ref, sin_ref, gq_ref, wuq_ref, gkv_ref, wk_ref, wkr_ref, wv_ref,
                    gqh_ref, gkh_ref, q_o, k_o, v_o):
    cq = zb_ref[:, 0:B_Q_LORA]
    ckv = zb_ref[:, B_Q_LORA:B_Q_LORA + B_KV_LORA]
    kr = zb_ref[:, B_Q_LORA + B_KV_LORA:]
    qn = cq * lax.rsqrt(jnp.mean(cq * cq, axis=-1, keepdims=True) + EPS) * gq_ref[...]
    kvn = ckv * lax.rsqrt(jnp.mean(ckv * ckv, axis=-1, keepdims=True) + EPS) * gkv_ref[...]
    q = _dot(qn, wuq_ref[...])
    k = _dot(kvn, wk_ref[...]) + _dot_f32(kr, wkr_ref[...])
    v_o[...] = _dot(kvn, wv_ref[...]).astype(BF16)
    cos_t = cos_ref[...]
    sin_t = sin_ref[...]
    lane = lax.broadcasted_iota(jnp.int32, cos_t.shape, 1)
    for h in range(B_HEADS):
        sl = slice(h * B_HEAD_PAD, (h + 1) * B_HEAD_PAD)
        qh = q[:, sl]
        kh = k[:, sl]
        qh = qh * lax.rsqrt(jnp.sum(qh * qh, axis=-1, keepdims=True) * (1.0 / B_QK) + EPS) * gqh_ref[...]
        kh = kh * lax.rsqrt(jnp.sum(kh * kh, axis=-1, keepdims=True) * (1.0 / B_QK) + EPS) * gkh_ref[...]
        q_o[:, sl] = (_rope(qh, cos_t, sin_t, lane) * (1.0 / math.sqrt(B_QK))).astype(BF16)
        k_o[:, sl] = _rope(kh, cos_t, sin_t, lane).astype(BF16)


def _rope_tables(positions):
    half = B_ROPE // 2
    inv_freq = ROPE_THETA ** (-jnp.arange(0, B_ROPE, 2, dtype=F32) / B_ROPE)
    ang = positions.astype(F32).reshape(-1, 1) * inv_freq
    cos, sin = jnp.cos(ang), jnp.sin(ang)
    n = ang.shape[0]
    ones = jnp.ones((n, B_NOPE), F32)
    zeros = jnp.zeros((n, B_NOPE), F32)
    tail = jnp.zeros((n, B_HEAD_PAD - B_QK), F32)
    cos_t = jnp.concatenate([ones, cos, cos, tail], axis=1)
    sin_t = jnp.concatenate([zeros, -sin, sin, tail], axis=1)
    assert cos_t.shape[1] == B_HEAD_PAD and half * 2 == B_ROPE
    return cos_t, sin_t


def _pad_heads(w, heads, dim):
    kdim = w.shape[0]
    w = w.reshape(kdim, heads, dim)
    return jnp.pad(w, ((0, 0), (0, 0), (0, B_HEAD_PAD - dim))).reshape(kdim, heads * B_HEAD_PAD)


def _mla_pre(zb, cos_t, sin_t, q_norm_g, w_uq, kv_norm_g, w_ukv, q_head_g, k_head_g, tm=256):
    n = zb.shape[0]
    wuq = _pad_heads(w_uq, B_HEADS, B_QK).astype(BF16)
    w_ukv = w_ukv.reshape(B_KV_LORA, B_HEADS, B_NOPE + B_V)
    wk = _pad_heads(w_ukv[:, :, :B_NOPE].reshape(B_KV_LORA, -1), B_HEADS, B_NOPE).astype(BF16)
    wv = w_ukv[:, :, B_NOPE:].reshape(B_KV_LORA, B_WIDTH).astype(BF16)
    place = jnp.pad(jnp.eye(B_ROPE, dtype=F32), ((0, LANES - B_ROPE), (B_NOPE, B_HEAD_PAD - B_QK)))
    wkr = jnp.tile(place, (1, B_HEADS))
    pad_g = lambda gg: jnp.pad(gg, (0, B_HEAD_PAD - B_QK)).reshape(1, -1)
    consts = [q_norm_g.reshape(1, -1), wuq, kv_norm_g.reshape(1, -1), wk, wkr, wv,
              pad_g(q_head_g), pad_g(k_head_g)]
    width = B_HEADS * B_HEAD_PAD
    return pl.pallas_call(
        _mla_pre_kernel,
        out_shape=(jax.ShapeDtypeStruct((n, width), BF16), jax.ShapeDtypeStruct((n, width), BF16),
                   jax.ShapeDtypeStruct((n, B_WIDTH), BF16)),
        grid=(n // tm,),
        in_specs=[_row_spec(tm, 4 * LANES), _row_spec(tm, LANES), _row_spec(tm, LANES)]
                 + [_const_spec(c.shape) for c in consts],
        out_specs=(_row_spec(tm, width), _row_spec(tm, width), _row_spec(tm, B_WIDTH)),
        compiler_params=_cparams(("parallel",)),
        name="mla_pre",
    )(zb, cos_t, sin_t, *consts)


def _attn_kernel(q_ref, k_ref, v_ref, o_ref, m_ref, l_ref, acc_ref, *, tile):
    i = pl.program_id(2)
    j = pl.program_id(3)

    @pl.when(j == 0)
    def _():
        m_ref[...] = jnp.full_like(m_ref, NEG_BIG)
        l_ref[...] = jnp.zeros_like(l_ref)
        acc_ref[...] = jnp.zeros_like(acc_ref)

    def update(masked):
        v = v_ref[...]
        for h in range(2):
            sl = slice(h * B_HEAD_PAD, (h + 1) * B_HEAD_PAD)
            s = lax.dot_general(q_ref[:, sl], k_ref[:, sl], (((1,), (1,)), ((), ())),
                                preferred_element_type=F32)
            if masked:
                qc = lax.broadcasted_iota(jnp.int32, s.shape, 0) // CHUNK
                kc = lax.broadcasted_iota(jnp.int32, s.shape, 1) // CHUNK
                s = jnp.where(kc <= qc, s, NEG_BIG)
            m_prev = m_ref[h]
            m_new = jnp.maximum(m_prev, jnp.max(s, axis=-1, keepdims=True))
            alpha = jnp.exp(m_prev - m_new)
            p = jnp.exp(s - m_new)
            l_ref[h] = alpha * l_ref[h] + jnp.sum(p, axis=-1, keepdims=True)
            acc_ref[h] = alpha * acc_ref[h] + jnp.dot(p.astype(BF16), v, preferred_element_type=F32)
            m_ref[h] = m_new

    @pl.when(j < i)
    def _():
        update(False)

    @pl.when(j == i)
    def _():
        update(True)
        lane = lax.broadcasted_iota(jnp.int32, o_ref.shape, 1)
        o0 = acc_ref[0] / l_ref[0]
        o1 = acc_ref[1] / l_ref[1]
        o_ref[...] = jnp.where(lane < B_V, o0, o1)


def _attention(q, k, v, n_batch, seq_len, tile=512):
    tile = min(tile, seq_len)
    nt = seq_len // tile
    width = B_HEADS * B_HEAD_PAD
    q = q.reshape(n_batch, seq_len, width)
    k = k.reshape(n_batch, seq_len, width)
    v = v.reshape(n_batch, seq_len, B_WIDTH)
    o = pl.pallas_call(
        functools.partial(_attn_kernel, tile=tile),
        out_shape=jax.ShapeDtypeStruct((n_batch, seq_len, B_WIDTH), F32),
        grid=(n_batch, B_HEADS // 2, nt, nt),
        in_specs=[pl.BlockSpec((None, tile, 2 * B_HEAD_PAD), lambda b, p, i, j: (b, i, p)),
                  pl.BlockSpec((None, tile, 2 * B_HEAD_PAD), lambda b, p, i, j: (b, jnp.minimum(i, j), p)),
                  pl.BlockSpec((None, tile, 2 * B_V), lambda b, p, i, j: (b, jnp.minimum(i, j), p))],
        out_specs=pl.BlockSpec((None, tile, 2 * B_V), lambda b, p, i, j: (b, i, p)),
        scratch_shapes=[pltpu.VMEM((2, tile, 1), F32), pltpu.VMEM((2, tile, 1), F32),
                        pltpu.VMEM((2, tile, 2 * B_V), F32)],
        compiler_params=_cparams(("parallel", "parallel", "parallel", "arbitrary")),
        name="mla_attention",
    )(q, k, v)
    return o.reshape(n_batch * seq_len, B_WIDTH)


def _gmlp_kernel(zc_ref, lng_ref, lnb_ref, ws_ref, bias_ref, og_ref, y_ref):
    z = zc_ref[...]
    z = 0.5 * z * (1.0 + lax.erf(z * (1.0 / math.sqrt(2.0))))
    u = z[:, :C_WIDTH]
    v = z[:, C_WIDTH:]
    mu = jnp.mean(v, axis=-1, keepdims=True)
    d = v - mu
    var = jnp.mean(d * d, axis=-1, keepdims=True)
    v = d * lax.rsqrt(var + EPS) * lng_ref[...] + lnb_ref[...]
    ti = lax.broadcasted_iota(jnp.int32, (C_BLOCK, C_BLOCK), 0)
    si = lax.broadcasted_iota(jnp.int32, (C_BLOCK, C_BLOCK), 1)
    group = lax.broadcasted_iota(jnp.int32, (1, C_WIDTH), 1) // C_GROUP_DIM
    ws = [jnp.where(ti >= si, ws_ref[gi], 0.0).astype(BF16) for gi in range(C_GROUPS)]
    blocks = []
    for blk in range(v.shape[0] // C_BLOCK):
        vb = v[blk * C_BLOCK:(blk + 1) * C_BLOCK]
        sv = bias_ref[...]
        for gi in range(C_GROUPS):
            sv = sv + jnp.dot(ws[gi], jnp.where(group == gi, vb, 0.0).astype(BF16),
                              preferred_element_type=F32)
        blocks.append(sv)
    sv = jnp.concatenate(blocks, axis=0) if len(blocks) > 1 else blocks[0]
    y = u * sv
    y_ref[...] = y * lax.rsqrt(jnp.mean(y * y, axis=-1, keepdims=True) + EPS) * og_ref[...]


def _gmlp(zc, ln_g, ln_b, w_s, b_s, out_g, tm=256):
    n = zc.shape[0]
    bias = jnp.repeat(jnp.transpose(b_s), C_GROUP_DIM, axis=1)
    consts = [ln_g.reshape(1, -1), ln_b.reshape(1, -1), w_s, bias, out_g.reshape(1, -1)]
    return pl.pallas_call(
        _gmlp_kernel,
        out_shape=jax.ShapeDtypeStruct((n, C_WIDTH), F32),
        grid=(n // tm,),
        in_specs=[_row_spec(tm, C_COLS)] + [_const_spec(c.shape) for c in consts],
        out_specs=_row_spec(tm, C_WIDTH),
        compiler_params=_cparams(("parallel",)),
        name="gmlp",
    )(zc, *consts)


def _out_proj_kernel(x_ref, ya_ref, o_ref, yc_ref, bg_ref, wa_ref, wb_ref, wc_ref, fg_ref,
                     xo_ref, h_ref):
    o = o_ref[...]
    yb = o * lax.rsqrt(jnp.mean(o * o, axis=-1, keepdims=True) + EPS) * bg_ref[...]
    x = (x_ref[...] + _dot(ya_ref[...], wa_ref[...]) + _dot(yb, wb_ref[...])
         + _dot(yc_ref[...], wc_ref[...]))
    xo_ref[...] = x
    h_ref[...] = (x * lax.rsqrt(jnp.mean(x * x, axis=-1, keepdims=True) + EPS) * fg_ref[...]).astype(BF16)


def _out_proj(x, ya, o, yc, b_out_g, w_out, ffn_g, tm=512):
    n = x.shape[0]
    wa = w_out[:A_WIDTH].astype(BF16)
    wb = w_out[A_WIDTH:A_WIDTH + B_WIDTH].astype(BF16)
    wc = w_out[A_WIDTH + B_WIDTH:].astype(BF16)
    consts = [b_out_g.reshape(1, -1), wa, wb, wc, ffn_g.reshape(1, -1)]
    return pl.pallas_call(
        _out_proj_kernel,
        out_shape=(jax.ShapeDtypeStruct((n, D_MODEL), F32), jax.ShapeDtypeStruct((n, D_MODEL), BF16)),
        grid=(n // tm,),
        in_specs=[_row_spec(tm, D_MODEL), _row_spec(tm, A_WIDTH), _row_spec(tm, B_WIDTH),
                  _row_spec(tm, C_WIDTH)] + [_const_spec(c.shape) for c in consts],
        out_specs=(_row_spec(tm, D_MODEL), _row_spec(tm, D_MODEL)),
        compiler_params=_cparams(("parallel",)),
        name="out_proj",
    )(x, ya, o, yc, *consts)


def _ffn_kernel(h_ref, x_ref, wg_ref, wu_ref, wd_ref, o_ref, acc_ref):
    f = pl.program_id(1)

    @pl.when(f == 0)
    def _():
        acc_ref[...] = x_ref[...]

    h = h_ref[...]
    gate = jnp.dot(h, wg_ref[...], preferred_element_type=F32)
    up = jnp.dot(h, wu_ref[...], preferred_element_type=F32)
    act = (gate * _sigmoid(gate) * up).astype(BF16)
    acc_ref[...] += jnp.dot(act, wd_ref[...], preferred_element_type=F32)

    @pl.when(f == pl.num_programs(1) - 1)
    def _():
        o_ref[...] = acc_ref[...]


def _pick_ff_tile(d_ff, target):
    best = LANES
    for t in range(LANES, d_ff + 1, LANES):
        if d_ff % t == 0 and t <= target:
            best = t
    return best


def _dense_ffn(x, h, wg, wu, wd, tm=1024, tf_target=1408):
    n = x.shape[0]
    tm = min(tm, n)
    d_ff = wg.shape[1]
    tf = _pick_ff_tile(d_ff, tf_target)
    return pl.pallas_call(
        _ffn_kernel,
        out_shape=jax.ShapeDtypeStruct((n, D_MODEL), F32),
        grid=(n // tm, d_ff // tf),
        in_specs=[pl.BlockSpec((tm, D_MODEL), lambda i, f: (i, 0)),
                  pl.BlockSpec((tm, D_MODEL), lambda i, f: (i, 0)),
                  pl.BlockSpec((D_MODEL, tf), lambda i, f: (0, f)),
                  pl.BlockSpec((D_MODEL, tf), lambda i, f: (0, f)),
                  pl.BlockSpec((tf, D_MODEL), lambda i, f: (f, 0))],
        out_specs=pl.BlockSpec((tm, D_MODEL), lambda i, f: (i, 0)),
        scratch_shapes=[pltpu.VMEM((tm, D_MODEL), F32)],
        compiler_params=_cparams(("parallel", "arbitrary")),
        name="dense_ffn",
    )(h, x, wg.astype(BF16), wu.astype(BF16), wd.astype(BF16))


def _router_kernel(x_ref, fg_ref, wr_ref, gate_ref):
    x = x_ref[...]
    h = x * lax.rsqrt(jnp.mean(x * x, axis=-1, keepdims=True) + EPS) * fg_ref[...]
    logits = _dot_f32(h, wr_ref[...])
    lane = lax.broadcasted_iota(jnp.int32, logits.shape, 1)
    logits = jnp.where(lane < N_EXPERTS, logits, -jnp.inf)
    v1 = jnp.max(logits, axis=-1, keepdims=True)
    i1 = jnp.min(jnp.where(logits == v1, lane, LANES), axis=-1, keepdims=True)
    rest = jnp.where(lane == i1, -jnp.inf, logits)
    v2 = jnp.max(rest, axis=-1, keepdims=True)
    i2 = jnp.min(jnp.where(rest == v2, lane, LANES), axis=-1, keepdims=True)
    e2 = jnp.exp(v2 - v1)
    g1 = 1.0 / (1.0 + e2)
    g2 = e2 / (1.0 + e2)
    gate_ref[...] = jnp.where(lane == i1, g1, jnp.where(lane == i2, g2, 0.0))


def _router(x, ffn_g, w_router, tm=512):
    n = x.shape[0]
    tm = min(tm, n)
    wr = jnp.pad(w_router, ((0, 0), (0, LANES - N_EXPERTS)))
    return pl.pallas_call(
        _router_kernel,
        out_shape=jax.ShapeDtypeStruct((n, LANES), F32),
        grid=(n // tm,),
        in_specs=[_row_spec(tm, D_MODEL), _const_spec((1, D_MODEL)), _const_spec(wr.shape)],
        out_specs=_row_spec(tm, LANES),
        compiler_params=_cparams(("parallel",)),
        name="moe_router",
    )(x, ffn_g.reshape(1, -1), wr)


def _moe_kernel(h_ref, x_ref, gates_ref, wg_ref, wu_ref, wd_ref, o_ref, acc_ref):
    e = pl.program_id(1)
    f = pl.program_id(2)

    @pl.when((e == 0) & (f == 0))
    def _():
        acc_ref[...] = x_ref[...]

    gates = gates_ref[...]
    lane = lax.broadcasted_iota(jnp.int32, gates.shape, 1)
    ge = jnp.sum(jnp.where(lane == e, gates, 0.0), axis=-1, keepdims=True)
    h = h_ref[...]
    gate = jnp.dot(h, wg_ref[...], preferred_element_type=F32)
    up = jnp.dot(h, wu_ref[...], preferred_element_type=F32)
    act = (gate * _sigmoid(gate) * up * ge).astype(BF16)
    acc_ref[...] += jnp.dot(act, wd_ref[...], preferred_element_type=F32)

    @pl.when((e == pl.num_programs(1) - 1) & (f == pl.num_programs(2) - 1))
    def _():
        o_ref[...] = acc_ref[...]


def _moe(x, h, gates, wg, wu, wd, tm=1024, tf_target=1792):
    n = x.shape[0]
    tm = min(tm, n)
    d_ff = wg.shape[2]
    tf = _pick_ff_tile(d_ff, tf_target)
    return pl.pallas_call(
        _moe_kernel,
        out_shape=jax.ShapeDtypeStruct((n, D_MODEL), F32),
        grid=(n // tm, N_EXPERTS, d_ff // tf),
        in_specs=[pl.BlockSpec((tm, D_MODEL), lambda i, e, f: (i, 0)),
                  pl.BlockSpec((tm, D_MODEL), lambda i, e, f: (i, 0)),
                  pl.BlockSpec((tm, LANES), lambda i, e, f: (i, 0)),
                  pl.BlockSpec((None, D_MODEL, tf), lambda i, e, f: (e, 0, f)),
                  pl.BlockSpec((None, D_MODEL, tf), lambda i, e, f: (e, 0, f)),
                  pl.BlockSpec((None, tf, D_MODEL), lambda i, e, f: (e, f, 0))],
        out_specs=pl.BlockSpec((tm, D_MODEL), lambda i, e, f: (i, 0)),
        scratch_shapes=[pltpu.VMEM((tm, D_MODEL), F32)],
        compiler_params=_cparams(("parallel", "arbitrary", "arbitrary")),
        name="moe_experts",
    )(h, x, gates, wg.astype(BF16), wu.astype(BF16), wd.astype(BF16))


def kernel(x, positions, mix_norm_g, w_in, shift_mu, a_w0, a_w_up, a_a0, a_a_up, a_g_up, a_k_k, a_k_a,
           a_r_k, a_ln_g, a_ln_b, a_v0, a_v_down, a_v_up, b_q_norm_g, b_w_uq, b_kv_norm_g, b_w_ukv,
           b_q_head_g, b_k_head_g, b_out_g, c_ln_g, c_ln_b, c_w_s, c_b_s, c_out_g, w_out, ffn_norm_g,
           dense_w_gate, dense_w_up, dense_w_down, moe_router, moe_w_gate, moe_w_up, moe_w_down):
    n_batch, seq_len, d_model = x.shape
    depth = w_in.shape[0]
    xf = x.reshape(n_batch * seq_len, d_model)
    cos_t, sin_t = _rope_tables(positions)
    v_first = None
    for l in range(depth):
        za, zb, zc = _in_proj(xf, mix_norm_g[l], w_in[l], shift_mu[l], seq_len)
        vres = (a_v0[l - 1], a_v_down[l - 1], a_v_up[l - 1]) if l > 0 else (None, None, None)
        r, k, v, kk, a, lw, g, bonus = _rwkv_pre(
            za, a_w0[l], a_w_up[l], a_a0[l], a_a_up[l], a_g_up[l], a_k_k[l], a_k_a[l],
            a_r_k[l].reshape(-1), v_first if l > 0 else None, *vres)
        if l == 0:
            v_first = v
        ya = _rwkv_scan(r, k, v, kk, a, lw, g, bonus, a_ln_g[l], a_ln_b[l], n_batch, seq_len)
        q, kq, vq = _mla_pre(zb, cos_t, sin_t, b_q_norm_g[l], b_w_uq[l], b_kv_norm_g[l], b_w_ukv[l],
                             b_q_head_g[l], b_k_head_g[l])
        o = _attention(q, kq, vq, n_batch, seq_len)
        yc = _gmlp(zc, c_ln_g[l], c_ln_b[l], c_w_s[l], c_b_s[l], c_out_g[l])
        xf, h = _out_proj(xf, ya, o, yc, b_out_g[l], w_out[l], ffn_norm_g[l])
        i = l // 2
        if l % 2 == 0:
            xf = _dense_ffn(xf, h, dense_w_gate[i], dense_w_up[i], dense_w_down[i])
        else:
            gates = _router(xf, ffn_norm_g[l], moe_router[i])
            xf = _moe(xf, h, gates, moe_w_gate[i], moe_w_up[i], moe_w_down[i])
    return xf.reshape(n_batch, seq_len, d_model)
```

```python
import functools
import math

import jax
import jax.numpy as jnp
from jax import lax
from jax.experimental import pallas as pl
from jax.experimental.pallas import tpu as pltpu

F32 = jnp.float32
BF16 = jnp.bfloat16
HIGHEST = lax.Precision.HIGHEST

D_MODEL = 1024
CHUNK = 64
A_HEADS = 6
A_HEAD_DIM = 64
A_WIDTH = A_HEADS * A_HEAD_DIM
A_W_LORA = 32
A_A_LORA = 32
A_G_LORA = 64
A_COLS = 3 * A_WIDTH + A_W_LORA + A_A_LORA + A_G_LORA
RWKV_GN_EPS = 64e-5
B_HEADS = 6
B_NOPE = 64
B_ROPE = 32
B_QK = B_NOPE + B_ROPE
B_V = 64
B_WIDTH = B_HEADS * B_V
B_Q_LORA = 256
B_KV_LORA = 128
B_COLS = B_Q_LORA + B_KV_LORA + B_ROPE
ROPE_THETA = 10000.0
C_GROUPS = 4
C_GROUP_DIM = 64
C_WIDTH = C_GROUPS * C_GROUP_DIM
C_BLOCK = 128
C_COLS = 2 * C_WIDTH
N_EXPERTS = 8
EPS = 1e-6

LANES = 128
SUBLANES = 8
VMEM_LIMIT_BYTES = 56 * 1024 * 1024

SCAN_CHUNK = 64
PAIR = 2 * A_HEAD_DIM
N_PAIRS = A_WIDTH // PAIR
B_HEAD_PAD = LANES
NEG_BIG = -1e30


def _cparams(sem):
    return pltpu.CompilerParams(dimension_semantics=sem, vmem_limit_bytes=VMEM_LIMIT_BYTES)


def _dot(a, b):
    return jnp.dot(a.astype(BF16), b.astype(BF16), preferred_element_type=F32)


def _dot_nt(a, b):
    return lax.dot_general(a.astype(BF16), b.astype(BF16), (((1,), (1,)), ((), ())),
                           preferred_element_type=F32)


def _dot_f32(a, b):
    return jnp.dot(a, b, precision=HIGHEST, preferred_element_type=F32)


def _dot_split(a, b, split_lhs=True):
    x = a if split_lhs else b
    hi = x.astype(BF16)
    lo = (x - hi.astype(F32)).astype(BF16)
    if split_lhs:
        e = b.astype(BF16)
        return jnp.dot(hi, e, preferred_element_type=F32) + jnp.dot(lo, e, preferred_element_type=F32)
    e = a.astype(BF16)
    return jnp.dot(e, hi, preferred_element_type=F32) + jnp.dot(e, lo, preferred_element_type=F32)


def _dot_state(m, h):
    return _dot(m, h)


def _sigmoid(x):
    return 1.0 / (1.0 + jnp.exp(-x))


def _const_spec(shape):
    nd = len(shape)
    return pl.BlockSpec(shape, lambda *_: (0,) * nd)


def _row_spec(tm, width):
    return pl.BlockSpec((tm, width), lambda i: (i, 0))


def _in_proj_kernel(x_ref, g_ref, wa_ref, wb_ref, wc_ref, mu_ref, za_ref, zb_ref, zc_ref, carry_ref,
                    *, tiles_per_seq):
    x = x_ref[...]
    ms = jnp.mean(x * x, axis=-1, keepdims=True)
    h = (x * lax.rsqrt(ms + EPS) * g_ref[...]).astype(BF16)
    za = jnp.dot(h, wa_ref[...], preferred_element_type=F32)
    zb_ref[...] = jnp.dot(h, wb_ref[...], preferred_element_type=F32)
    zc_ref[...] = jnp.dot(h, wc_ref[...], preferred_element_type=F32)
    tm = za.shape[0]
    first = jnp.where(pl.program_id(0) % tiles_per_seq == 0, 0.0, carry_ref[0:1, :])
    row = lax.broadcasted_iota(jnp.int32, za.shape, 0)
    prev = jnp.where(row == 0, first, pltpu.roll(za, 1, 0))
    carry_ref[0:1, :] = za[tm - 1:tm, :]
    za_ref[...] = za + (prev - za) * mu_ref[...]


def _in_proj(x, g, w_in, mu, seq_len, tm=256):
    n = x.shape[0]
    wa = w_in[:, :A_COLS].astype(BF16)
    wb = jnp.pad(w_in[:, A_COLS:A_COLS + B_COLS], ((0, 0), (0, 4 * LANES - B_COLS))).astype(BF16)
    wc = w_in[:, A_COLS + B_COLS:].astype(BF16)
    return pl.pallas_call(
        functools.partial(_in_proj_kernel, tiles_per_seq=seq_len // tm),
        out_shape=(jax.ShapeDtypeStruct((n, A_COLS), F32),
                   jax.ShapeDtypeStruct((n, 4 * LANES), F32),
                   jax.ShapeDtypeStruct((n, C_COLS), F32)),
        grid=(n // tm,),
        in_specs=[_row_spec(tm, D_MODEL), _const_spec((1, D_MODEL)), _const_spec(wa.shape),
                  _const_spec(wb.shape), _const_spec(wc.shape), _const_spec((1, A_COLS))],
        out_specs=(_row_spec(tm, A_COLS), _row_spec(tm, 4 * LANES), _row_spec(tm, C_COLS)),
        scratch_shapes=[pltpu.VMEM((SUBLANES, A_COLS), F32)],
        compiler_params=_cparams(("arbitrary",)),
        name="in_proj",
    )(x, g.reshape(1, -1), wa, wb, wc, mu.reshape(1, -1))


def _rwkv_pre_kernel(*refs, has_vres):
    if has_vres:
        (za_ref, w0_ref, wup_ref, a0_ref, aup_ref, gup_ref, kk_ref, ka_ref, rk_ref, bd_ref,
         vf_ref, v0_ref, vdn_ref, vup_ref,
         r_o, k_o, v_o, kk_o, a_o, lw_o, g_o, bonus_o) = refs
    else:
        (za_ref, w0_ref, wup_ref, a0_ref, aup_ref, gup_ref, kk_ref, ka_ref, rk_ref, bd_ref,
         r_o, k_o, v_o, kk_o, a_o, lw_o, g_o, bonus_o) = refs
    r = za_ref[:, 0:A_WIDTH]
    k = za_ref[:, A_WIDTH:2 * A_WIDTH]
    v = za_ref[:, 2 * A_WIDTH:3 * A_WIDTH]
    lora = za_ref[:, 3 * A_WIDTH:A_COLS]
    w = w0_ref[...] + _dot_f32(jnp.tanh(lora), wup_ref[...])
    w_log = -(jnp.maximum(-w, 0.0) + jnp.log(1.0 + jnp.exp(-jnp.abs(w)))) - 0.5
    lw_o[...] = -jnp.exp(w_log)
    a = _sigmoid(a0_ref[...] + _dot_f32(lora, aup_ref[...]))
    g_o[...] = _dot_f32(_sigmoid(lora), gup_ref[...])
    if has_vres:
        mix = _sigmoid(v0_ref[...] + _dot_f32(_dot_f32(v, vdn_ref[...]), vup_ref[...]))
        v = v + (vf_ref[...] - v) * mix
    kk = k * kk_ref[...]
    ss = _dot_f32(kk * kk, bd_ref[...])
    kk = kk / jnp.maximum(jnp.sqrt(ss), 1e-12)
    k = k * (1.0 + (a - 1.0) * ka_ref[...])
    r_o[...] = r
    k_o[...] = k
    v_o[...] = v
    kk_o[...] = kk
    a_o[...] = a
    bonus_o[...] = _dot_f32(r * k * rk_ref[...], bd_ref[...]) * v


def _head_block_diag(width, head_dim):
    idx = jnp.arange(width) // head_dim
    return (idx[:, None] == idx[None, :]).astype(F32)


def _pad_rows(w, row_off, rows=LANES):
    return jnp.pad(w, ((row_off, rows - row_off - w.shape[0]), (0, 0)))


def _rwkv_pre(za, w0, w_up, a0, a_up, g_up, k_k, k_a, r_k, v_first, v0, v_down, v_up, tm=256):
    n = za.shape[0]
    has_vres = v_first is not None
    row = lambda t: t.reshape(1, -1)
    args = [za, row(w0), _pad_rows(w_up, 0), row(a0), _pad_rows(a_up, A_W_LORA),
            _pad_rows(g_up, A_W_LORA + A_A_LORA), row(k_k), row(k_a), row(r_k),
            _head_block_diag(A_WIDTH, A_HEAD_DIM)]
    specs = [_row_spec(tm, A_COLS)] + [_const_spec(a.shape) for a in args[1:]]
    if has_vres:
        vdn = jnp.pad(v_down, ((0, 0), (0, LANES - v_down.shape[1])))
        extra = [v_first, row(v0), vdn, _pad_rows(v_up, 0)]
        args += extra
        specs += [_row_spec(tm, A_WIDTH)] + [_const_spec(a.shape) for a in extra[1:]]
    out = jax.ShapeDtypeStruct((n, A_WIDTH), F32)
    return pl.pallas_call(
        functools.partial(_rwkv_pre_kernel, has_vres=has_vres),
        out_shape=(out,) * 8,
        grid=(n // tm,),
        in_specs=specs,
        out_specs=(_row_spec(tm, A_WIDTH),) * 8,
        compiler_params=_cparams(("parallel",)),
        name="rwkv_pre",
    )(*args)


def _stack_heads(x, lane_lo):
    return jnp.concatenate([jnp.where(lane_lo, x, 0.0), jnp.where(lane_lo, 0.0, x)], axis=0)


def _rwkv_scan_kernel(r_ref, k_ref, v_ref, kk_ref, a_ref, lw_ref, g_ref, bonus_ref, lng_ref, lnb_ref,
                      bd_ref, y_ref, h_ref):
    n_batch, L, _ = r_ref.shape
    S2 = 2 * L

    @pl.when(pl.program_id(0) == 0)
    def _():
        h_ref[...] = jnp.zeros_like(h_ref)

    ti = lax.broadcasted_iota(jnp.int32, (L, L), 0)
    si = lax.broadcasted_iota(jnp.int32, (L, L), 1)
    tril_incl = (ti >= si).astype(F32)
    row2 = lax.broadcasted_iota(jnp.int32, (S2, S2), 0)
    col2 = lax.broadcasted_iota(jnp.int32, (S2, S2), 1)
    strict = row2 > col2
    incl = row2 >= col2
    eye = row2 == col2
    lane_lo = lax.broadcasted_iota(jnp.int32, (1, PAIR), 1) < A_HEAD_DIM
    mid = L // 2 - 1

    chains = []
    for b in range(n_batch):
        lw = lw_ref[b]
        cum = _dot_split(tril_incl, lw, split_lhs=False)
        c_mid = cum[mid:mid + 1, :]
        c_end = cum[L - 1:L, :]
        kk = kk_ref[b]
        kb = kk * a_ref[b]
        kmod = k_ref[b]
        e_dn = jnp.exp(c_mid - cum)
        e_hat = jnp.exp(c_end - cum)
        at_all = -kk * jnp.exp(cum - lw - c_mid)
        bt_all = kb * e_dn
        kt_all = kmod * e_dn
        rt_all = r_ref[b] * jnp.exp(cum - c_mid)
        bh_all = kb * e_hat
        kh_all = kmod * e_hat
        p_end = jnp.exp(c_end)
        p_mid = jnp.exp(c_mid)
        at_true = at_all * p_mid
        rt_true = rt_all * p_mid
        v_all = v_ref[b]
        for p in range(N_PAIRS):
            sl = slice(p * PAIR, (p + 1) * PAIR)
            stack = lambda t: _stack_heads(t[:, sl], lane_lo)
            chains.append(dict(
                b=b, p=p, a_s=stack(at_all), b_s=stack(bt_all), k_s=stack(kt_all), r_s=stack(rt_all),
                bh_s=stack(bh_all), kh_s=stack(kh_all), v_s=stack(v_all), a_in=stack(at_true),
                r_in=stack(rt_true), p_end=p_end[:, sl]))
    for c in chains:
        gram = _dot_nt(jnp.concatenate([c["a_s"], c["r_s"]], axis=0),
                       jnp.concatenate([c["b_s"], c["k_s"]], axis=0))
        a_ab = jnp.where(strict, gram[:S2, :S2], 0.0)
        c["a_kv"] = jnp.concatenate([jnp.where(strict, gram[:S2, S2:], 0.0),
                                     jnp.where(incl, gram[S2:, S2:], 0.0)], axis=0)
        c["a_rb"] = jnp.where(incl, gram[S2:, :S2], 0.0)
        c["x"] = a_ab
        c["t"] = jnp.where(eye, 1.0, a_ab)
    for _ in range(int(math.log2(L)) - 1):
        for c in chains:
            c["x"] = _dot(c["x"], c["x"])
        for c in chains:
            c["t"] = c["t"] + _dot(c["t"], c["x"])
    for c in chains:
        c["av"] = _dot(c["a_kv"], c["v_s"])
        c["kv"] = _dot(c["kh_s"].T, c["v_s"])
    for c in chains:
        c["tw"] = _dot(c["t"], jnp.concatenate([c["a_in"], c["av"][:S2]], axis=1))
    for c in chains:
        qy = _dot(c["a_rb"], c["tw"])
        mc = _dot(c["bh_s"].T, c["tw"])
        c["q"] = c["r_in"] + qy[:, :PAIR]
        c["y0"] = qy[:, PAIR:] + c["av"][S2:]
        c["m"] = jnp.where(eye, c["p_end"], 0.0) + mc[:, :PAIR]
        c["c"] = mc[:, PAIR:] + c["kv"]
    for c in chains:
        h = h_ref[c["b"], c["p"]]
        ys = _dot(c["q"], h) + c["y0"]
        h_ref[c["b"], c["p"]] = _dot_state(c["m"], h) + c["c"]
        c["y"] = ys[:L] + ys[L:]
    for b in range(n_batch):
        y = jnp.concatenate([c["y"] for c in chains if c["b"] == b], axis=1)
        mu = _dot_split(y, bd_ref[...]) * (1.0 / A_HEAD_DIM)
        d = y - mu
        var = _dot_split(d * d, bd_ref[...]) * (1.0 / A_HEAD_DIM)
        yn = d * lax.rsqrt(var + RWKV_GN_EPS) * lng_ref[...] + lnb_ref[...]
        y_ref[b] = (yn + bonus_ref[b]) * g_ref[b]


def _rwkv_scan(r, k, v, kk, a, lw, g, bonus, ln_g, ln_b, n_batch, seq_len):
    L = SCAN_CHUNK
    shp = (n_batch, seq_len, A_WIDTH)
    seq_spec = pl.BlockSpec((n_batch, L, A_WIDTH), lambda i: (0, i, 0))
    args = [t.reshape(shp) for t in (r, k, v, kk, a, lw, g, bonus)]
    consts = [ln_g.reshape(1, -1), ln_b.reshape(1, -1), _head_block_diag(A_WIDTH, A_HEAD_DIM)]
    y = pl.pallas_call(
        _rwkv_scan_kernel,
        out_shape=jax.ShapeDtypeStruct(shp, F32),
        grid=(seq_len // L,),
        in_specs=[seq_spec] * 8 + [_const_spec(c.shape) for c in consts],
        out_specs=seq_spec,
        scratch_shapes=[pltpu.VMEM((n_batch, N_PAIRS, PAIR, PAIR), F32)],
        compiler_params=_cparams(("arbitrary",)),
        name="rwkv_scan",
    )(*args, *consts)
    return y.reshape(n_batch * seq_len, A_WIDTH)


def _rope(t, cos_t, sin_t, lane):
    swapped = jnp.where(lane < B_NOPE + B_ROPE // 2,
                        pltpu.roll(t, LANES - B_ROPE // 2, 1), pltpu.roll(t, B_ROPE // 2, 1))
    return t * cos_t + swapped * sin_t


def _mla_pre_kernel(zb_ref, cos_ref, sin_ref, gq_ref, wuq_ref, gkv_ref, wk_ref, wkr_ref, wv_ref,
                    gqh_ref, gkh_ref, q_o, k_o, v_o):
    cq = zb_ref[:, 0:B_Q_LORA]
    ckv = zb_ref[:, B_Q_LORA:B_Q_LORA + B_KV_LORA]
    kr = zb_ref[:, B_Q_LORA + B_KV_LORA:]
    qn = cq * lax.rsqrt(jnp.mean(cq * cq, axis=-1, keepdims=True) + EPS) * gq_ref[...]
    kvn = ckv * lax.rsqrt(jnp.mean(ckv * ckv, axis=-1, keepdims=True) + EPS) * gkv_ref[...]
    q = _dot(qn, wuq_ref[...])
    k = _dot(kvn, wk_ref[...]) + _dot_f32(kr, wkr_ref[...])
    v_o[...] = _dot(kvn, wv_ref[...]).astype(BF16)
    cos_t = cos_ref[...]
    sin_t = sin_ref[...]
    lane = lax.broadcasted_iota(jnp.int32, cos_t.shape, 1)
    for h in range(B_HEADS):
        sl = slice(h * B_HEAD_PAD, (h + 1) * B_HEAD_PAD)
        qh = q[:, sl]
        kh = k[:, sl]
        qh = qh * lax.rsqrt(jnp.sum(qh * qh, axis=-1, keepdims=True) * (1.0 / B_QK) + EPS) * gqh_ref[...]
        kh = kh * lax.rsqrt(jnp.sum(kh * kh, axis=-1, keepdims=True) * (1.0 / B_QK) + EPS) * gkh_ref[...]
        q_o[:, sl] = (_rope(qh, cos_t, sin_t, lane) * (1.0 / math.sqrt(B_QK))).astype(BF16)
        k_o[:, sl] = _rope(kh, cos_t, sin_t, lane).astype(BF16)


def _rope_tables(positions):
    half = B_ROPE // 2
    inv_freq = ROPE_THETA ** (-jnp.arange(0, B_ROPE, 2, dtype=F32) / B_ROPE)
    ang = positions.astype(F32).reshape(-1, 1) * inv_freq
    cos, sin = jnp.cos(ang), jnp.sin(ang)
    n = ang.shape[0]
    ones = jnp.ones((n, B_NOPE), F32)
    zeros = jnp.zeros((n, B_NOPE), F32)
    tail = jnp.zeros((n, B_HEAD_PAD - B_QK), F32)
    cos_t = jnp.concatenate([ones, cos, cos, tail], axis=1)
    sin_t = jnp.concatenate([zeros, -sin, sin, tail], axis=1)
    assert cos_t.shape[1] == B_HEAD_PAD and half * 2 == B_ROPE
    return cos_t, sin_t


def _pad_heads(w, heads, dim):
    kdim = w.shape[0]
    w = w.reshape(kdim, heads, dim)
    return jnp.pad(w, ((0, 0), (0, 0), (0, B_HEAD_PAD - dim))).reshape(kdim, heads * B_HEAD_PAD)


def _mla_pre(zb, cos_t, sin_t, q_norm_g, w_uq, kv_norm_g, w_ukv, q_head_g, k_head_g, tm=256):
    n = zb.shape[0]
    wuq = _pad_heads(w_uq, B_HEADS, B_QK).astype(BF16)
    w_ukv = w_ukv.reshape(B_KV_LORA, B_HEADS, B_NOPE + B_V)
    wk = _pad_heads(w_ukv[:, :, :B_NOPE].reshape(B_KV_LORA, -1), B_HEADS, B_NOPE).astype(BF16)
    wv = w_ukv[:, :, B_NOPE:].reshape(B_KV_LORA, B_WIDTH).astype(BF16)
    place = jnp.pad(jnp.eye(B_ROPE, dtype=F32), ((0, LANES - B_ROPE), (B_NOPE, B_HEAD_PAD - B_QK)))
    wkr = jnp.tile(place, (1, B_HEADS))
    pad_g = lambda gg: jnp.pad(gg, (0, B_HEAD_PAD - B_QK)).reshape(1, -1)
    consts = [q_norm_g.reshape(1, -1), wuq, kv_norm_g.reshape(1, -1), wk, wkr, wv,
              pad_g(q_head_g), pad_g(k_head_g)]
    width = B_HEADS * B_HEAD_PAD
    return pl.pallas_call(
        _mla_pre_kernel,
        out_shape=(jax.ShapeDtypeStruct((n, width), BF16), jax.ShapeDtypeStruct((n, width), BF16),
                   jax.ShapeDtypeStruct((n, B_WIDTH), BF16)),
        grid=(n // tm,),
        in_specs=[_row_spec(tm, 4 * LANES), _row_spec(tm, LANES), _row_spec(tm, LANES)]
                 + [_const_spec(c.shape) for c in consts],
        out_specs=(_row_spec(tm, width), _row_spec(tm, width), _row_spec(tm, B_WIDTH)),
        compiler_params=_cparams(("parallel",)),
        name="mla_pre",
    )(zb, cos_t, sin_t, *consts)


def _attn_kernel(qt_ref, kt_ref, q_ref, k_ref, v_ref, o_ref, m_ref, acc_ref):
    t = pl.program_id(2)
    i = qt_ref[t]
    j = kt_ref[t]
    tk = k_ref.shape[0]

    @pl.when(j == 0)
    def _():
        m_ref[...] = jnp.full_like(m_ref, NEG_BIG)
        acc_ref[...] = jnp.zeros_like(acc_ref)

    def update(masked):
        v = v_ref[...]
        lane_v = lax.broadcasted_iota(jnp.int32, v.shape, 1)
        one = jnp.ones_like(v)
        for h in range(2):
            sl = slice(h * B_HEAD_PAD, (h + 1) * B_HEAD_PAD)
            s = lax.dot_general(q_ref[:, sl], k_ref[:, sl], (((1,), (1,)), ((), ())),
                                preferred_element_type=F32)
            if masked:
                qc = lax.broadcasted_iota(jnp.int32, s.shape, 0) // CHUNK
                kc = lax.broadcasted_iota(jnp.int32, s.shape, 1) // CHUNK
                s = jnp.where(kc <= qc, s, NEG_BIG)
            m_prev = m_ref[h]
            m_new = jnp.maximum(m_prev, jnp.max(s, axis=-1, keepdims=True))
            alpha = jnp.exp(m_prev - m_new)
            p = jnp.exp(s - jnp.concatenate([m_new] * (tk // LANES), axis=1))
            vh = jnp.where((lane_v < B_V) == (h == 0), v, one)
            acc_ref[h] = alpha * acc_ref[h] + jnp.dot(p.astype(BF16), vh, preferred_element_type=F32)
            m_ref[h] = m_new

    @pl.when(j < i)
    def _():
        update(False)

    @pl.when(j == i)
    def _():
        update(True)
        lane = lax.broadcasted_iota(jnp.int32, o_ref.shape, 1)
        a0 = acc_ref[0]
        a1 = acc_ref[1]
        o_ref[...] = jnp.where(lane < B_V, a0 / pltpu.roll(a0, B_V, 1), a1 / pltpu.roll(a1, B_V, 1))


def _attention(q, k, v, n_batch, seq_len, tile=512):
    tile = min(tile, seq_len)
    nt = seq_len // tile
    width = B_HEADS * B_HEAD_PAD
    q = q.reshape(n_batch, seq_len, width)
    k = k.reshape(n_batch, seq_len, width)
    v = v.reshape(n_batch, seq_len, B_WIDTH)
    q_tiles = jnp.asarray([i for i in range(nt) for _ in range(i + 1)], jnp.int32)
    k_tiles = jnp.asarray([j for i in range(nt) for j in range(i + 1)], jnp.int32)
    o = pl.pallas_call(
        _attn_kernel,
        out_shape=jax.ShapeDtypeStruct((n_batch, seq_len, B_WIDTH), F32),
        grid_spec=pltpu.PrefetchScalarGridSpec(
            num_scalar_prefetch=2,
            grid=(n_batch, B_HEADS // 2, nt * (nt + 1) // 2),
            in_specs=[pl.BlockSpec((None, tile, 2 * B_HEAD_PAD), lambda b, p, t, qt, kt: (b, qt[t], p)),
                      pl.BlockSpec((None, tile, 2 * B_HEAD_PAD), lambda b, p, t, qt, kt: (b, kt[t], p)),
                      pl.BlockSpec((None, tile, 2 * B_V), lambda b, p, t, qt, kt: (b, kt[t], p))],
            out_specs=pl.BlockSpec((None, tile, 2 * B_V), lambda b, p, t, qt, kt: (b, qt[t], p)),
            scratch_shapes=[pltpu.VMEM((2, tile, LANES), F32), pltpu.VMEM((2, tile, 2 * B_V), F32)]),
        compiler_params=_cparams(("parallel", "parallel", "arbitrary")),
        name="mla_attention",
    )(q_tiles, k_tiles, q, k, v)
    return o.reshape(n_batch * seq_len, B_WIDTH)


def _gmlp_kernel(zc_ref, lng_ref, lnb_ref, ws_ref, bias_ref, og_ref, y_ref):
    z = zc_ref[...]
    z = 0.5 * z * (1.0 + lax.erf(z * (1.0 / math.sqrt(2.0))))
    u = z[:, :C_WIDTH]
    v = z[:, C_WIDTH:]
    mu = jnp.mean(v, axis=-1, keepdims=True)
    d = v - mu
    var = jnp.mean(d * d, axis=-1, keepdims=True)
    v = d * lax.rsqrt(var + EPS) * lng_ref[...] + lnb_ref[...]
    ti = lax.broadcasted_iota(jnp.int32, (C_BLOCK, C_BLOCK), 0)
    si = lax.broadcasted_iota(jnp.int32, (C_BLOCK, C_BLOCK), 1)
    group = lax.broadcasted_iota(jnp.int32, (1, C_WIDTH), 1) // C_GROUP_DIM
    ws = [jnp.where(ti >= si, ws_ref[gi], 0.0).astype(BF16) for gi in range(C_GROUPS)]
    blocks = []
    for blk in range(v.shape[0] // C_BLOCK):
        vb = v[blk * C_BLOCK:(blk + 1) * C_BLOCK]
        sv = bias_ref[...]
        for gi in range(C_GROUPS):
            sv = sv + jnp.dot(ws[gi], jnp.where(group == gi, vb, 0.0).astype(BF16),
                              preferred_element_type=F32)
        blocks.append(sv)
    sv = jnp.concatenate(blocks, axis=0) if len(blocks) > 1 else blocks[0]
    y = u * sv
    y_ref[...] = y * lax.rsqrt(jnp.mean(y * y, axis=-1, keepdims=True) + EPS) * og_ref[...]


def _gmlp(zc, ln_g, ln_b, w_s, b_s, out_g, tm=256):
    n = zc.shape[0]
    bias = jnp.repeat(jnp.transpose(b_s), C_GROUP_DIM, axis=1)
    consts = [ln_g.reshape(1, -1), ln_b.reshape(1, -1), w_s, bias, out_g.reshape(1, -1)]
    return pl.pallas_call(
        _gmlp_kernel,
        out_shape=jax.ShapeDtypeStruct((n, C_WIDTH), F32),
        grid=(n // tm,),
        in_specs=[_row_spec(tm, C_COLS)] + [_const_spec(c.shape) for c in consts],
        out_specs=_row_spec(tm, C_WIDTH),
        compiler_params=_cparams(("parallel",)),
        name="gmlp",
    )(zc, *consts)


def _out_proj_kernel(x_ref, ya_ref, o_ref, yc_ref, bg_ref, wa_ref, wb_ref, wc_ref, fg_ref,
                     xo_ref, h_ref):
    o = o_ref[...]
    yb = o * lax.rsqrt(jnp.mean(o * o, axis=-1, keepdims=True) + EPS) * bg_ref[...]
    x = (x_ref[...] + _dot(ya_ref[...], wa_ref[...]) + _dot(yb, wb_ref[...])
         + _dot(yc_ref[...], wc_ref[...]))
    xo_ref[...] = x
    h_ref[...] = (x * lax.rsqrt(jnp.mean(x * x, axis=-1, keepdims=True) + EPS) * fg_ref[...]).astype(BF16)


def _out_proj(x, ya, o, yc, b_out_g, w_out, ffn_g, tm=512):
    n = x.shape[0]
    wa = w_out[:A_WIDTH].astype(BF16)
    wb = w_out[A_WIDTH:A_WIDTH + B_WIDTH].astype(BF16)
    wc = w_out[A_WIDTH + B_WIDTH:].astype(BF16)
    consts = [b_out_g.reshape(1, -1), wa, wb, wc, ffn_g.reshape(1, -1)]
    return pl.pallas_call(
        _out_proj_kernel,
        out_shape=(jax.ShapeDtypeStruct((n, D_MODEL), F32), jax.ShapeDtypeStruct((n, D_MODEL), BF16)),
        grid=(n // tm,),
        in_specs=[_row_spec(tm, D_MODEL), _row_spec(tm, A_WIDTH), _row_spec(tm, B_WIDTH),
                  _row_spec(tm, C_WIDTH)] + [_const_spec(c.shape) for c in consts],
        out_specs=(_row_spec(tm, D_MODEL), _row_spec(tm, D_MODEL)),
        compiler_params=_cparams(("parallel",)),
        name="out_proj",
    )(x, ya, o, yc, *consts)


def _ffn_kernel(h_ref, x_ref, wg_ref, wu_ref, wd_ref, o_ref, acc_ref):
    f = pl.program_id(1)

    @pl.when(f == 0)
    def _():
        acc_ref[...] = x_ref[...]

    h = h_ref[...]
    gate = jnp.dot(h, wg_ref[...], preferred_element_type=F32)
    up = jnp.dot(h, wu_ref[...], preferred_element_type=F32)
    act = (gate * _sigmoid(gate) * up).astype(BF16)
    acc_ref[...] += jnp.dot(act, wd_ref[...], preferred_element_type=F32)

    @pl.when(f == pl.num_programs(1) - 1)
    def _():
        o_ref[...] = acc_ref[...]


def _pick_ff_tile(d_ff, target):
    best = LANES
    for t in range(LANES, d_ff + 1, LANES):
        if d_ff % t == 0 and t <= target:
            best = t
    return best


def _dense_ffn(x, h, wg, wu, wd, tm=1024, tf_target=1408):
    n = x.shape[0]
    tm = min(tm, n)
    d_ff = wg.shape[1]
    tf = _pick_ff_tile(d_ff, tf_target)
    return pl.pallas_call(
        _ffn_kernel,
        out_shape=jax.ShapeDtypeStruct((n, D_MODEL), F32),
        grid=(n // tm, d_ff // tf),
        in_specs=[pl.BlockSpec((tm, D_MODEL), lambda i, f: (i, 0)),
                  pl.BlockSpec((tm, D_MODEL), lambda i, f: (i, 0)),
                  pl.BlockSpec((D_MODEL, tf), lambda i, f: (0, f)),
                  pl.BlockSpec((D_MODEL, tf), lambda i, f: (0, f)),
                  pl.BlockSpec((tf, D_MODEL), lambda i, f: (f, 0))],
        out_specs=pl.BlockSpec((tm, D_MODEL), lambda i, f: (i, 0)),
        scratch_shapes=[pltpu.VMEM((tm, D_MODEL), F32)],
        compiler_params=_cparams(("parallel", "arbitrary")),
        name="dense_ffn",
    )(h, x, wg.astype(BF16), wu.astype(BF16), wd.astype(BF16))


def _router_kernel(x_ref, fg_ref, wr_ref, route_ref):
    x = x_ref[...]
    h = x * lax.rsqrt(jnp.mean(x * x, axis=-1, keepdims=True) + EPS) * fg_ref[...]
    logits = _dot_f32(h, wr_ref[...])
    lane = lax.broadcasted_iota(jnp.int32, logits.shape, 1)
    logits = jnp.where(lane < N_EXPERTS, logits, -jnp.inf)
    v1 = jnp.max(logits, axis=-1, keepdims=True)
    i1 = jnp.min(jnp.where(logits == v1, lane, LANES), axis=-1, keepdims=True)
    rest = jnp.where(lane == i1, -jnp.inf, logits)
    v2 = jnp.max(rest, axis=-1, keepdims=True)
    i2 = jnp.min(jnp.where(rest == v2, lane, LANES), axis=-1, keepdims=True)
    e2 = jnp.exp(v2 - v1)
    g1 = 1.0 / (1.0 + e2)
    g2 = e2 / (1.0 + e2)
    route_ref[...] = jnp.where(lane == 0, i1.astype(F32),
                               jnp.where(lane == 1, i2.astype(F32),
                                         jnp.where(lane == 2, g1, jnp.where(lane == 3, g2, 0.0))))


def _router(x, ffn_g, w_router, tm=512):
    n = x.shape[0]
    tm = min(tm, n)
    wr = jnp.pad(w_router, ((0, 0), (0, LANES - N_EXPERTS)))
    return pl.pallas_call(
        _router_kernel,
        out_shape=jax.ShapeDtypeStruct((n, LANES), F32),
        grid=(n // tm,),
        in_specs=[_row_spec(tm, D_MODEL), _const_spec((1, D_MODEL)), _const_spec(wr.shape)],
        out_specs=_row_spec(tm, LANES),
        compiler_params=_cparams(("parallel",)),
        name="moe_router",
    )(x, ffn_g.reshape(1, -1), wr)


def _moe_plan(route, tm):
    n = route.shape[0]
    experts = jnp.concatenate([route[:, 0], route[:, 1]]).astype(jnp.int32)
    onehot = (experts[:, None] == jnp.arange(N_EXPERTS, dtype=jnp.int32)[None, :]).astype(jnp.int32)
    csum = jnp.cumsum(onehot, axis=0)
    rank = jnp.sum(csum * onehot, axis=1) - 1
    counts = csum[-1]
    padded = (counts + tm - 1) // tm * tm
    ends = jnp.cumsum(padded)
    pos = (ends - padded)[experts] + rank
    n_rows = 2 * n + N_EXPERTS * tm
    tokens = jnp.concatenate([jnp.arange(n, dtype=jnp.int32)] * 2)
    src = jnp.zeros((n_rows,), jnp.int32).at[pos].set(tokens)
    tile_start = jnp.arange(n_rows // tm, dtype=jnp.int32) * tm
    tile_expert = jnp.minimum(jnp.sum(tile_start[:, None] >= ends[None, :], axis=1), N_EXPERTS - 1)
    n_used = (ends[-1] // tm).reshape(1)
    return pos.astype(jnp.int32), src, tile_expert.astype(jnp.int32), n_used.astype(jnp.int32)


def _moe_expert_kernel(te_ref, src_ref, nu_ref, x_hbm, fg_ref, wg_hbm, wu_hbm, wd_hbm, y_ref,
                       rows_ref, wg_ref, wu_ref, wd_ref, row_sem, w_sem, *, ff_chunk):
    t = pl.program_id(0)
    tm = y_ref.shape[0]
    d_ff = wg_ref.shape[1]
    n_used = nu_ref[0]
    slot = t % 2
    e = te_ref[t]

    def start_rows(tile, s):
        def body(r, c):
            pltpu.make_async_copy(x_hbm.at[pl.ds(src_ref[tile * tm + r], 1)],
                                  rows_ref.at[s, pl.ds(r, 1)], row_sem.at[s]).start()
            return c
        lax.fori_loop(0, tm, body, 0, unroll=8)

    def weight_copies():
        return (pltpu.make_async_copy(wg_hbm.at[e], wg_ref, w_sem.at[0]),
                pltpu.make_async_copy(wu_hbm.at[e], wu_ref, w_sem.at[1]),
                pltpu.make_async_copy(wd_hbm.at[e], wd_ref, w_sem.at[2]))

    new_expert = (t == 0) | (e != te_ref[jnp.maximum(t - 1, 0)])

    @pl.when(t == 0)
    def _():
        start_rows(0, 0)

    @pl.when((t < n_used) & new_expert)
    def _():
        for c in weight_copies():
            c.start()

    @pl.when(t + 1 < n_used)
    def _():
        start_rows(t + 1, 1 - slot)

    @pl.when(t < n_used)
    def _():
        pltpu.make_async_copy(x_hbm.at[pl.ds(0, tm)], rows_ref.at[slot], row_sem.at[slot]).wait()
        x = rows_ref[slot]
        h = (x * lax.rsqrt(jnp.mean(x * x, axis=-1, keepdims=True) + EPS) * fg_ref[...]).astype(BF16)

        @pl.when(new_expert)
        def _():
            for c in weight_copies():
                c.wait()

        acc = jnp.zeros((tm, D_MODEL), F32)
        for c0 in range(0, d_ff, ff_chunk):
            gate = jnp.dot(h, wg_ref[:, c0:c0 + ff_chunk], preferred_element_type=F32)
            up = jnp.dot(h, wu_ref[:, c0:c0 + ff_chunk], preferred_element_type=F32)
            act = (gate * _sigmoid(gate) * up).astype(BF16)
            acc = acc + jnp.dot(act, wd_ref[c0:c0 + ff_chunk, :], preferred_element_type=F32)
        y_ref[...] = acc

    @pl.when(t >= n_used)
    def _():
        y_ref[...] = jnp.zeros_like(y_ref)


def _moe_experts(x, src, tile_expert, n_used, ffn_g, wg, wu, wd, tm, ff_chunk=512):
    n_rows = src.shape[0]
    d_ff = wg.shape[2]
    any_spec = pl.BlockSpec(memory_space=pl.ANY)
    return pl.pallas_call(
        functools.partial(_moe_expert_kernel, ff_chunk=min(ff_chunk, d_ff)),
        out_shape=jax.ShapeDtypeStruct((n_rows, D_MODEL), F32),
        grid_spec=pltpu.PrefetchScalarGridSpec(
            num_scalar_prefetch=3,
            grid=(n_rows // tm,),
            in_specs=[any_spec, pl.BlockSpec((1, D_MODEL), lambda t, *_: (0, 0)),
                      any_spec, any_spec, any_spec],
            out_specs=pl.BlockSpec((tm, D_MODEL), lambda t, *_: (t, 0)),
            scratch_shapes=[pltpu.VMEM((2, tm, D_MODEL), F32),
                            pltpu.VMEM((D_MODEL, d_ff), BF16), pltpu.VMEM((D_MODEL, d_ff), BF16),
                            pltpu.VMEM((d_ff, D_MODEL), BF16),
                            pltpu.SemaphoreType.DMA((2,)), pltpu.SemaphoreType.DMA((3,))]),
        compiler_params=_cparams(("arbitrary",)),
        name="moe_experts",
    )(tile_expert, src, n_used, x, ffn_g.reshape(1, -1), wg.astype(BF16), wu.astype(BF16), wd.astype(BF16))


def _moe_combine_kernel(pos_ref, x_ref, route_ref, y_hbm, o_ref, buf_ref, sem, *, n_tokens):
    i = pl.program_id(0)
    tm = x_ref.shape[0]
    slot = i % 2

    def start_rows(tile, s):
        def body(r, c):
            tok = tile * tm + r
            for choice in range(2):
                pltpu.make_async_copy(y_hbm.at[pl.ds(pos_ref[choice * n_tokens + tok], 1)],
                                      buf_ref.at[s, choice, pl.ds(r, 1)], sem.at[s]).start()
            return c
        lax.fori_loop(0, tm, body, 0, unroll=4)

    @pl.when(i == 0)
    def _():
        start_rows(0, 0)

    @pl.when(i + 1 < pl.num_programs(0))
    def _():
        start_rows(i + 1, 1 - slot)

    for choice in range(2):
        pltpu.make_async_copy(y_hbm.at[pl.ds(0, tm)], buf_ref.at[slot, choice], sem.at[slot]).wait()
    route = route_ref[...]
    lane = lax.broadcasted_iota(jnp.int32, route.shape, 1)
    g1 = jnp.sum(jnp.where(lane == 2, route, 0.0), axis=-1, keepdims=True)
    g2 = jnp.sum(jnp.where(lane == 3, route, 0.0), axis=-1, keepdims=True)
    o_ref[...] = x_ref[...] + g1 * buf_ref[slot, 0] + g2 * buf_ref[slot, 1]


def _moe_combine(x, route, pos, y, tm=256):
    n = x.shape[0]
    tm = min(tm, n)
    return pl.pallas_call(
        functools.partial(_moe_combine_kernel, n_tokens=n),
        out_shape=jax.ShapeDtypeStruct((n, D_MODEL), F32),
        grid_spec=pltpu.PrefetchScalarGridSpec(
            num_scalar_prefetch=1,
            grid=(n // tm,),
            in_specs=[pl.BlockSpec((tm, D_MODEL), lambda i, *_: (i, 0)),
                      pl.BlockSpec((tm, LANES), lambda i, *_: (i, 0)),
                      pl.BlockSpec(memory_space=pl.ANY)],
            out_specs=pl.BlockSpec((tm, D_MODEL), lambda i, *_: (i, 0)),
            scratch_shapes=[pltpu.VMEM((2, 2, tm, D_MODEL), F32), pltpu.SemaphoreType.DMA((2,))]),
        compiler_params=_cparams(("arbitrary",)),
        name="moe_combine",
    )(pos, x, route, y)


def _moe(x, ffn_g, w_router, wg, wu, wd, tm=512):
    tm = min(tm, x.shape[0])
    route = _router(x, ffn_g, w_router)
    pos, src, tile_expert, n_used = _moe_plan(route, tm)
    y = _moe_experts(x, src, tile_expert, n_used, ffn_g, wg, wu, wd, tm)
    return _moe_combine(x, route, pos, y)


def kernel(x, positions, mix_norm_g, w_in, shift_mu, a_w0, a_w_up, a_a0, a_a_up, a_g_up, a_k_k, a_k_a,
           a_r_k, a_ln_g, a_ln_b, a_v0, a_v_down, a_v_up, b_q_norm_g, b_w_uq, b_kv_norm_g, b_w_ukv,
           b_q_head_g, b_k_head_g, b_out_g, c_ln_g, c_ln_b, c_w_s, c_b_s, c_out_g, w_out, ffn_norm_g,
           dense_w_gate, dense_w_up, dense_w_down, moe_router, moe_w_gate, moe_w_up, moe_w_down):
    n_batch, seq_len, d_model = x.shape
    depth = w_in.shape[0]
    xf = x.reshape(n_batch * seq_len, d_model)
    cos_t, sin_t = _rope_tables(positions)
    v_first = None
    for l in range(depth):
        za, zb, zc = _in_proj(xf, mix_norm_g[l], w_in[l], shift_mu[l], seq_len)
        vres = (a_v0[l - 1], a_v_down[l - 1], a_v_up[l - 1]) if l > 0 else (None, None, None)
        r, k, v, kk, a, lw, g, bonus = _rwkv_pre(
            za, a_w0[l], a_w_up[l], a_a0[l], a_a_up[l], a_g_up[l], a_k_k[l], a_k_a[l],
            a_r_k[l].reshape(-1), v_first if l > 0 else None, *vres)
        if l == 0:
            v_first = v
        ya = _rwkv_scan(r, k, v, kk, a, lw, g, bonus, a_ln_g[l], a_ln_b[l], n_batch, seq_len)
        q, kq, vq = _mla_pre(zb, cos_t, sin_t, b_q_norm_g[l], b_w_uq[l], b_kv_norm_g[l], b_w_ukv[l],
                             b_q_head_g[l], b_k_head_g[l])
        o = _attention(q, kq, vq, n_batch, seq_len)
        yc = _gmlp(zc, c_ln_g[l], c_ln_b[l], c_w_s[l], c_b_s[l], c_out_g[l])
        xf, h = _out_proj(xf, ya, o, yc, b_out_g[l], w_out[l], ffn_norm_g[l])
        i = l // 2
        if l % 2 == 0:
            xf = _dense_ffn(xf, h, dense_w_gate[i], dense_w_up[i], dense_w_down[i])
        else:
            xf = _moe(xf, ffn_norm_g[l], moe_router[i], moe_w_gate[i], moe_w_up[i], moe_w_down[i])
    return xf.reshape(n_batch, seq_len, d_model)
```

```python
import functools
import math

import jax
import jax.numpy as jnp
from jax import lax
from jax.experimental import pallas as pl
from jax.experimental.pallas import tpu as pltpu

F32 = jnp.float32
BF16 = jnp.bfloat16
HIGHEST = lax.Precision.HIGHEST

D_MODEL = 1024
CHUNK = 64
A_HEADS = 6
A_HEAD_DIM = 64
A_WIDTH = A_HEADS * A_HEAD_DIM
A_W_LORA = 32
A_A_LORA = 32
A_G_LORA = 64
A_COLS = 3 * A_WIDTH + A_W_LORA + A_A_LORA + A_G_LORA
RWKV_GN_EPS = 64e-5
B_HEADS = 6
B_NOPE = 64
B_ROPE = 32
B_QK = B_NOPE + B_ROPE
B_V = 64
B_WIDTH = B_HEADS * B_V
B_Q_LORA = 256
B_KV_LORA = 128
B_COLS = B_Q_LORA + B_KV_LORA + B_ROPE
ROPE_THETA = 10000.0
C_GROUPS = 4
C_GROUP_DIM = 64
C_WIDTH = C_GROUPS * C_GROUP_DIM
C_BLOCK = 128
C_COLS = 2 * C_WIDTH
N_EXPERTS = 8
EPS = 1e-6

LANES = 128
SUBLANES = 8
VMEM_LIMIT_BYTES = 56 * 1024 * 1024

SCAN_CHUNK = 64
PAIR = 2 * A_HEAD_DIM
N_PAIRS = A_WIDTH // PAIR
B_HEAD_PAD = LANES
NEG_BIG = -1e30
LOG2_E = math.log2(math.e)


def _cparams(sem):
    return pltpu.CompilerParams(dimension_semantics=sem, vmem_limit_bytes=VMEM_LIMIT_BYTES)


def _dot(a, b):
    return jnp.dot(a.astype(BF16), b.astype(BF16), preferred_element_type=F32)


def _dot_nt(a, b):
    return lax.dot_general(a.astype(BF16), b.astype(BF16), (((1,), (1,)), ((), ())),
                           preferred_element_type=F32)


def _dot_f32(a, b):
    return jnp.dot(a, b, precision=HIGHEST, preferred_element_type=F32)


def _dot_split(a, b, split_lhs=True):
    x = a if split_lhs else b
    hi = x.astype(BF16)
    lo = (x - hi.astype(F32)).astype(BF16)
    if split_lhs:
        e = b.astype(BF16)
        return jnp.dot(hi, e, preferred_element_type=F32) + jnp.dot(lo, e, preferred_element_type=F32)
    e = a.astype(BF16)
    return jnp.dot(e, hi, preferred_element_type=F32) + jnp.dot(e, lo, preferred_element_type=F32)


def _dot3(a, b):
    a_hi = a.astype(BF16)
    a_lo = (a - a_hi.astype(F32)).astype(BF16)
    b_hi = b.astype(BF16)
    b_lo = (b - b_hi.astype(F32)).astype(BF16)
    return (jnp.dot(a_hi, b_hi, preferred_element_type=F32) + jnp.dot(a_lo, b_hi, preferred_element_type=F32)
            + jnp.dot(a_hi, b_lo, preferred_element_type=F32))


def _dot_state(m, h):
    return _dot(m, h)


def _sigmoid(x):
    return 1.0 / (1.0 + jnp.exp(-x))


def _const_spec(shape):
    nd = len(shape)
    return pl.BlockSpec(shape, lambda *_: (0,) * nd)


def _row_spec(tm, width):
    return pl.BlockSpec((tm, width), lambda i: (i, 0))


def _in_proj_kernel(x_ref, g_ref, wa_ref, wb_ref, wc_ref, mu_ref, za_ref, zb_ref, zc_ref, carry_ref,
                    *, tiles_per_seq):
    x = x_ref[...]
    ms = jnp.mean(x * x, axis=-1, keepdims=True)
    h = (x * lax.rsqrt(ms + EPS) * g_ref[...]).astype(BF16)
    za = jnp.dot(h, wa_ref[...], preferred_element_type=F32)
    zb_ref[...] = jnp.dot(h, wb_ref[...], preferred_element_type=F32)
    zc_ref[...] = jnp.dot(h, wc_ref[...], preferred_element_type=F32)
    tm = za.shape[0]
    first = jnp.where(pl.program_id(0) % tiles_per_seq == 0, 0.0, carry_ref[0:1, :])
    row = lax.broadcasted_iota(jnp.int32, za.shape, 0)
    prev = jnp.where(row == 0, first, pltpu.roll(za, 1, 0))
    carry_ref[0:1, :] = za[tm - 1:tm, :]
    za_ref[...] = za + (prev - za) * mu_ref[...]


def _in_proj(x, g, w_in, mu, seq_len, tm=256):
    n = x.shape[0]
    wa = w_in[:, :A_COLS].astype(BF16)
    wb = jnp.pad(w_in[:, A_COLS:A_COLS + B_COLS], ((0, 0), (0, 4 * LANES - B_COLS))).astype(BF16)
    wc = w_in[:, A_COLS + B_COLS:].astype(BF16)
    return pl.pallas_call(
        functools.partial(_in_proj_kernel, tiles_per_seq=seq_len // tm),
        out_shape=(jax.ShapeDtypeStruct((n, A_COLS), F32),
                   jax.ShapeDtypeStruct((n, 4 * LANES), F32),
                   jax.ShapeDtypeStruct((n, C_COLS), F32)),
        grid=(n // tm,),
        in_specs=[_row_spec(tm, D_MODEL), _const_spec((1, D_MODEL)), _const_spec(wa.shape),
                  _const_spec(wb.shape), _const_spec(wc.shape), _const_spec((1, A_COLS))],
        out_specs=(_row_spec(tm, A_COLS), _row_spec(tm, 4 * LANES), _row_spec(tm, C_COLS)),
        scratch_shapes=[pltpu.VMEM((SUBLANES, A_COLS), F32)],
        compiler_params=_cparams(("arbitrary",)),
        name="in_proj",
    )(x, g.reshape(1, -1), wa, wb, wc, mu.reshape(1, -1))


def _rwkv_pre_kernel(*refs, has_vres):
    if has_vres:
        (za_ref, w0_ref, wup_ref, a0_ref, aup_ref, gup_ref, kk_ref, ka_ref, rk_ref, bd_ref,
         vf_ref, v0_ref, vdn_ref, vup_ref,
         r_o, k_o, v_o, kk_o, a_o, lw_o, g_o, bonus_o) = refs
    else:
        (za_ref, w0_ref, wup_ref, a0_ref, aup_ref, gup_ref, kk_ref, ka_ref, rk_ref, bd_ref,
         r_o, k_o, v_o, kk_o, a_o, lw_o, g_o, bonus_o) = refs
    r = za_ref[:, 0:A_WIDTH]
    k = za_ref[:, A_WIDTH:2 * A_WIDTH]
    v = za_ref[:, 2 * A_WIDTH:3 * A_WIDTH]
    lora = za_ref[:, 3 * A_WIDTH:A_COLS]
    w = w0_ref[...] + _dot3(jnp.tanh(lora), wup_ref[...])
    w_log = -(jnp.maximum(-w, 0.0) + jnp.log(1.0 + jnp.exp(-jnp.abs(w)))) - 0.5
    lw_o[...] = -jnp.exp(w_log)
    a = _sigmoid(a0_ref[...] + _dot3(lora, aup_ref[...]))
    g_o[...] = _dot(_sigmoid(lora), gup_ref[...])
    if has_vres:
        mix = _sigmoid(v0_ref[...] + _dot3(_dot3(v, vdn_ref[...]), vup_ref[...]))
        v = v + (vf_ref[...] - v) * mix
    kk = k * kk_ref[...]
    ss = _dot_split(kk * kk, bd_ref[...])
    kk = kk / jnp.maximum(jnp.sqrt(ss), 1e-12)
    k = k * (1.0 + (a - 1.0) * ka_ref[...])
    r_o[...] = r
    k_o[...] = k
    v_o[...] = v
    kk_o[...] = kk
    a_o[...] = a
    bonus_o[...] = _dot_split(r * k * rk_ref[...], bd_ref[...]) * v


def _head_block_diag(width, head_dim):
    idx = jnp.arange(width) // head_dim
    return (idx[:, None] == idx[None, :]).astype(F32)


def _pad_rows(w, row_off, rows=LANES):
    return jnp.pad(w, ((row_off, rows - row_off - w.shape[0]), (0, 0)))


def _rwkv_pre(za, w0, w_up, a0, a_up, g_up, k_k, k_a, r_k, v_first, v0, v_down, v_up, tm=256):
    n = za.shape[0]
    has_vres = v_first is not None
    row = lambda t: t.reshape(1, -1)
    args = [za, row(w0), _pad_rows(w_up, 0), row(a0), _pad_rows(a_up, A_W_LORA),
            _pad_rows(g_up, A_W_LORA + A_A_LORA), row(k_k), row(k_a), row(r_k),
            _head_block_diag(A_WIDTH, A_HEAD_DIM)]
    specs = [_row_spec(tm, A_COLS)] + [_const_spec(a.shape) for a in args[1:]]
    if has_vres:
        vdn = jnp.pad(v_down, ((0, 0), (0, LANES - v_down.shape[1])))
        extra = [v_first, row(v0), vdn, _pad_rows(v_up, 0)]
        args += extra
        specs += [_row_spec(tm, A_WIDTH)] + [_const_spec(a.shape) for a in extra[1:]]
    out = jax.ShapeDtypeStruct((n, A_WIDTH), F32)
    return pl.pallas_call(
        functools.partial(_rwkv_pre_kernel, has_vres=has_vres),
        out_shape=(out,) * 8,
        grid=(n // tm,),
        in_specs=specs,
        out_specs=(_row_spec(tm, A_WIDTH),) * 8,
        compiler_params=_cparams(("parallel",)),
        name="rwkv_pre",
    )(*args)


def _stack_heads(x, lane_lo):
    return jnp.concatenate([jnp.where(lane_lo, x, 0.0), jnp.where(lane_lo, 0.0, x)], axis=0)


def _rwkv_scan_kernel(r_ref, k_ref, v_ref, kk_ref, a_ref, lw_ref, g_ref, bonus_ref, lng_ref, lnb_ref,
                      bd_ref, y_ref, h_ref):
    n_batch, L, _ = r_ref.shape
    S2 = 2 * L

    @pl.when(pl.program_id(0) == 0)
    def _():
        h_ref[...] = jnp.zeros_like(h_ref)

    ti = lax.broadcasted_iota(jnp.int32, (L, L), 0)
    si = lax.broadcasted_iota(jnp.int32, (L, L), 1)
    tril_incl = (ti >= si).astype(F32)
    row2 = lax.broadcasted_iota(jnp.int32, (S2, S2), 0)
    col2 = lax.broadcasted_iota(jnp.int32, (S2, S2), 1)
    strict = row2 > col2
    incl = row2 >= col2
    eye = row2 == col2
    lane_lo = lax.broadcasted_iota(jnp.int32, (1, PAIR), 1) < A_HEAD_DIM
    mid = L // 2 - 1

    chains = []
    for b in range(n_batch):
        lw = lw_ref[b]
        cum = _dot_split(tril_incl, lw, split_lhs=False)
        c_mid = cum[mid:mid + 1, :]
        c_end = cum[L - 1:L, :]
        kk = kk_ref[b]
        kb = kk * a_ref[b]
        kmod = k_ref[b]
        e_dn = jnp.exp(c_mid - cum)
        e_hat = jnp.exp(c_end - cum)
        at_all = -kk * jnp.exp(cum - lw - c_mid)
        bt_all = kb * e_dn
        kt_all = kmod * e_dn
        rt_all = r_ref[b] * jnp.exp(cum - c_mid)
        bh_all = kb * e_hat
        kh_all = kmod * e_hat
        p_end = jnp.exp(c_end)
        p_mid = jnp.exp(c_mid)
        at_true = at_all * p_mid
        rt_true = rt_all * p_mid
        v_all = v_ref[b]
        for p in range(N_PAIRS):
            sl = slice(p * PAIR, (p + 1) * PAIR)
            stack = lambda t: _stack_heads(t[:, sl], lane_lo)
            chains.append(dict(
                b=b, p=p, a_s=stack(at_all), b_s=stack(bt_all), k_s=stack(kt_all), r_s=stack(rt_all),
                bh_s=stack(bh_all), kh_s=stack(kh_all), v_s=stack(v_all), a_in=stack(at_true),
                r_in=stack(rt_true), p_end=p_end[:, sl]))
    for c in chains:
        gram = _dot_nt(jnp.concatenate([c["a_s"], c["r_s"]], axis=0),
                       jnp.concatenate([c["b_s"], c["k_s"]], axis=0))
        a_ab = jnp.where(strict, gram[:S2, :S2], 0.0)
        c["a_kv"] = jnp.concatenate([jnp.where(strict, gram[:S2, S2:], 0.0),
                                     jnp.where(incl, gram[S2:, S2:], 0.0)], axis=0)
        c["a_rb"] = jnp.where(incl, gram[S2:, :S2], 0.0)
        c["x"] = a_ab
        c["t"] = jnp.where(eye, 1.0, a_ab)
    n_doublings = int(math.log2(L)) - 1
    for c in chains:
        c["x"] = _dot(c["x"], c["x"])
    for it in range(n_doublings):
        for c in chains:
            if it + 1 < n_doublings:
                both = _dot(c["x"], jnp.concatenate([c["t"], c["x"]], axis=1))
                c["t"] = c["t"] + both[:, :S2]
                c["x"] = both[:, S2:]
            else:
                c["t"] = c["t"] + _dot(c["x"], c["t"])
    for c in chains:
        c["av"] = _dot(c["a_kv"], c["v_s"])
        c["kv"] = _dot(c["kh_s"].T, c["v_s"])
    for c in chains:
        c["tw"] = _dot(c["t"], jnp.concatenate([c["a_in"], c["av"][:S2]], axis=1))
    for c in chains:
        qy = _dot(c["a_rb"], c["tw"])
        mc = _dot(c["bh_s"].T, c["tw"])
        c["q"] = c["r_in"] + qy[:, :PAIR]
        c["y0"] = qy[:, PAIR:] + c["av"][S2:]
        c["m"] = jnp.where(eye, c["p_end"], 0.0) + mc[:, :PAIR]
        c["c"] = mc[:, PAIR:] + c["kv"]
    for c in chains:
        h = h_ref[c["b"], c["p"]]
        ys = _dot(c["q"], h) + c["y0"]
        h_ref[c["b"], c["p"]] = _dot_state(c["m"], h) + c["c"]
        c["y"] = ys[:L] + ys[L:]
    for b in range(n_batch):
        y = jnp.concatenate([c["y"] for c in chains if c["b"] == b], axis=1)
        mu = _dot_split(y, bd_ref[...]) * (1.0 / A_HEAD_DIM)
        d = y - mu
        var = _dot_split(d * d, bd_ref[...]) * (1.0 / A_HEAD_DIM)
        yn = d * lax.rsqrt(var + RWKV_GN_EPS) * lng_ref[...] + lnb_ref[...]
        y_ref[b] = (yn + bonus_ref[b]) * g_ref[b]


def _rwkv_scan(r, k, v, kk, a, lw, g, bonus, ln_g, ln_b, n_batch, seq_len):
    L = SCAN_CHUNK
    shp = (n_batch, seq_len, A_WIDTH)
    seq_spec = pl.BlockSpec((n_batch, L, A_WIDTH), lambda i: (0, i, 0))
    args = [t.reshape(shp) for t in (r, k, v, kk, a, lw, g, bonus)]
    consts = [ln_g.reshape(1, -1), ln_b.reshape(1, -1), _head_block_diag(A_WIDTH, A_HEAD_DIM)]
    y = pl.pallas_call(
        _rwkv_scan_kernel,
        out_shape=jax.ShapeDtypeStruct(shp, F32),
        grid=(seq_len // L,),
        in_specs=[seq_spec] * 8 + [_const_spec(c.shape) for c in consts],
        out_specs=seq_spec,
        scratch_shapes=[pltpu.VMEM((n_batch, N_PAIRS, PAIR, PAIR), F32)],
        compiler_params=_cparams(("arbitrary",)),
        name="rwkv_scan",
    )(*args, *consts)
    return y.reshape(n_batch * seq_len, A_WIDTH)


def _rope(t, cos_t, sin_t, lane):
    swapped = jnp.where(lane < B_NOPE + B_ROPE // 2,
                        pltpu.roll(t, LANES - B_ROPE // 2, 1), pltpu.roll(t, B_ROPE // 2, 1))
    return t * cos_t + swapped * sin_t


def _mla_pre_kernel(zb_ref, cos_ref, sin_ref, gq_ref, wuq_ref, gkv_ref, wk_ref, wv_ref,
                    gqh_ref, gkh_ref, q_o, k_o, v_o):
    cq = zb_ref[:, 0:B_Q_LORA]
    ckv = zb_ref[:, B_Q_LORA:B_Q_LORA + B_KV_LORA]
    kr = zb_ref[:, B_Q_LORA + B_KV_LORA:]
    kr = pltpu.roll(kr, B_NOPE, 1)
    qn = cq * lax.rsqrt(jnp.mean(cq * cq, axis=-1, keepdims=True) + EPS) * gq_ref[...]
    kvn = ckv * lax.rsqrt(jnp.mean(ckv * ckv, axis=-1, keepdims=True) + EPS) * gkv_ref[...]
    q = _dot(qn, wuq_ref[...])
    k = _dot(kvn, wk_ref[...])
    v_o[...] = _dot(kvn, wv_ref[...]).astype(BF16)
    cos_t = cos_ref[...]
    sin_t = sin_ref[...]
    lane = lax.broadcasted_iota(jnp.int32, cos_t.shape, 1)
    for h in range(B_HEADS):
        sl = slice(h * B_HEAD_PAD, (h + 1) * B_HEAD_PAD)
        qh = q[:, sl]
        kh = k[:, sl] + kr
        qh = qh * lax.rsqrt(jnp.sum(qh * qh, axis=-1, keepdims=True) * (1.0 / B_QK) + EPS) * gqh_ref[...]
        kh = kh * lax.rsqrt(jnp.sum(kh * kh, axis=-1, keepdims=True) * (1.0 / B_QK) + EPS) * gkh_ref[...]
        q_o[:, sl] = (_rope(qh, cos_t, sin_t, lane) * (LOG2_E / math.sqrt(B_QK))).astype(BF16)
        k_o[:, sl] = _rope(kh, cos_t, sin_t, lane).astype(BF16)


def _rope_tables(positions):
    half = B_ROPE // 2
    inv_freq = ROPE_THETA ** (-jnp.arange(0, B_ROPE, 2, dtype=F32) / B_ROPE)
    ang = positions.astype(F32).reshape(-1, 1) * inv_freq
    cos, sin = jnp.cos(ang), jnp.sin(ang)
    n = ang.shape[0]
    ones = jnp.ones((n, B_NOPE), F32)
    zeros = jnp.zeros((n, B_NOPE), F32)
    tail = jnp.zeros((n, B_HEAD_PAD - B_QK), F32)
    cos_t = jnp.concatenate([ones, cos, cos, tail], axis=1)
    sin_t = jnp.concatenate([zeros, -sin, sin, tail], axis=1)
    assert cos_t.shape[1] == B_HEAD_PAD and half * 2 == B_ROPE
    return cos_t, sin_t


def _pad_heads(w, heads, dim):
    kdim = w.shape[0]
    w = w.reshape(kdim, heads, dim)
    return jnp.pad(w, ((0, 0), (0, 0), (0, B_HEAD_PAD - dim))).reshape(kdim, heads * B_HEAD_PAD)


def _mla_pre(zb, cos_t, sin_t, q_norm_g, w_uq, kv_norm_g, w_ukv, q_head_g, k_head_g, tm=256):
    n = zb.shape[0]
    wuq = _pad_heads(w_uq, B_HEADS, B_QK).astype(BF16)
    w_ukv = w_ukv.reshape(B_KV_LORA, B_HEADS, B_NOPE + B_V)
    wk = _pad_heads(w_ukv[:, :, :B_NOPE].reshape(B_KV_LORA, -1), B_HEADS, B_NOPE).astype(BF16)
    wv = w_ukv[:, :, B_NOPE:].reshape(B_KV_LORA, B_WIDTH).astype(BF16)
    pad_g = lambda gg: jnp.pad(gg, (0, B_HEAD_PAD - B_QK)).reshape(1, -1)
    consts = [q_norm_g.reshape(1, -1), wuq, kv_norm_g.reshape(1, -1), wk, wv,
              pad_g(q_head_g), pad_g(k_head_g)]
    width = B_HEADS * B_HEAD_PAD
    return pl.pallas_call(
        _mla_pre_kernel,
        out_shape=(jax.ShapeDtypeStruct((n, width), BF16), jax.ShapeDtypeStruct((n, width), BF16),
                   jax.ShapeDtypeStruct((n, B_WIDTH), BF16)),
        grid=(n // tm,),
        in_specs=[_row_spec(tm, 4 * LANES), _row_spec(tm, LANES), _row_spec(tm, LANES)]
                 + [_const_spec(c.shape) for c in consts],
        out_specs=(_row_spec(tm, width), _row_spec(tm, width), _row_spec(tm, B_WIDTH)),
        compiler_params=_cparams(("parallel",)),
        name="mla_pre",
    )(zb, cos_t, sin_t, *consts)


def _attn_kernel(qt_ref, kt_ref, q_ref, k_ref, v_ref, o_ref, m_ref, acc_ref):
    t = pl.program_id(2)
    i = qt_ref[t]
    j = kt_ref[t]
    tk = k_ref.shape[0]

    @pl.when(j == 0)
    def _():
        m_ref[...] = jnp.full_like(m_ref, NEG_BIG)
        acc_ref[...] = jnp.zeros_like(acc_ref)

    def update(masked):
        v = v_ref[...]
        lane_v = lax.broadcasted_iota(jnp.int32, v.shape, 1)
        one = jnp.ones_like(v)
        for h in range(2):
            sl = slice(h * B_HEAD_PAD, (h + 1) * B_HEAD_PAD)
            s = lax.dot_general(q_ref[:, sl], k_ref[:, sl], (((1,), (1,)), ((), ())),
                                preferred_element_type=F32)
            if masked:
                qc = lax.broadcasted_iota(jnp.int32, s.shape, 0) // CHUNK
                kc = lax.broadcasted_iota(jnp.int32, s.shape, 1) // CHUNK
                s = jnp.where(kc <= qc, s, NEG_BIG)
            m_prev = m_ref[h]
            m_new = jnp.maximum(m_prev, jnp.max(s, axis=-1, keepdims=True))
            alpha = jnp.exp2(m_prev - m_new)
            p = jnp.exp2((s - jnp.concatenate([m_new] * (tk // LANES), axis=1)).astype(BF16))
            vh = jnp.where((lane_v < B_V) == (h == 0), v, one)
            acc_ref[h] = alpha * acc_ref[h] + jnp.dot(p, vh, preferred_element_type=F32)
            m_ref[h] = m_new

    @pl.when(j < i)
    def _():
        update(False)

    @pl.when(j == i)
    def _():
        update(True)
        lane = lax.broadcasted_iota(jnp.int32, o_ref.shape, 1)
        a0 = acc_ref[0]
        a1 = acc_ref[1]
        o_ref[...] = jnp.where(lane < B_V, a0 / pltpu.roll(a0, B_V, 1), a1 / pltpu.roll(a1, B_V, 1))


def _attention(q, k, v, n_batch, seq_len, tile=1024):
    tile = min(tile, seq_len)
    nt = seq_len // tile
    width = B_HEADS * B_HEAD_PAD
    q = q.reshape(n_batch, seq_len, width)
    k = k.reshape(n_batch, seq_len, width)
    v = v.reshape(n_batch, seq_len, B_WIDTH)
    q_tiles = jnp.asarray([i for i in range(nt) for _ in range(i + 1)], jnp.int32)
    k_tiles = jnp.asarray([j for i in range(nt) for j in range(i + 1)], jnp.int32)
    o = pl.pallas_call(
        _attn_kernel,
        out_shape=jax.ShapeDtypeStruct((n_batch, seq_len, B_WIDTH), F32),
        grid_spec=pltpu.PrefetchScalarGridSpec(
            num_scalar_prefetch=2,
            grid=(n_batch, B_HEADS // 2, nt * (nt + 1) // 2),
            in_specs=[pl.BlockSpec((None, tile, 2 * B_HEAD_PAD), lambda b, p, t, qt, kt: (b, qt[t], p)),
                      pl.BlockSpec((None, tile, 2 * B_HEAD_PAD), lambda b, p, t, qt, kt: (b, kt[t], p)),
                      pl.BlockSpec((None, tile, 2 * B_V), lambda b, p, t, qt, kt: (b, kt[t], p))],
            out_specs=pl.BlockSpec((None, tile, 2 * B_V), lambda b, p, t, qt, kt: (b, qt[t], p)),
            scratch_shapes=[pltpu.VMEM((2, tile, LANES), F32), pltpu.VMEM((2, tile, 2 * B_V), F32)]),
        compiler_params=_cparams(("parallel", "parallel", "arbitrary")),
        name="mla_attention",
    )(q_tiles, k_tiles, q, k, v)
    return o.reshape(n_batch * seq_len, B_WIDTH)


def _gmlp_kernel(zc_ref, lng_ref, lnb_ref, ws_ref, bias_ref, og_ref, y_ref):
    z = zc_ref[...]
    z = 0.5 * z * (1.0 + lax.erf(z * (1.0 / math.sqrt(2.0))))
    u = z[:, :C_WIDTH]
    v = z[:, C_WIDTH:]
    mu = jnp.mean(v, axis=-1, keepdims=True)
    d = v - mu
    var = jnp.mean(d * d, axis=-1, keepdims=True)
    v = d * lax.rsqrt(var + EPS) * lng_ref[...] + lnb_ref[...]
    ti = lax.broadcasted_iota(jnp.int32, (C_BLOCK, C_BLOCK), 0)
    si = lax.broadcasted_iota(jnp.int32, (C_BLOCK, C_BLOCK), 1)
    group = lax.broadcasted_iota(jnp.int32, (1, C_WIDTH), 1) // C_GROUP_DIM
    ws = [jnp.where(ti >= si, ws_ref[gi], 0.0).astype(BF16) for gi in range(C_GROUPS)]
    blocks = []
    for blk in range(v.shape[0] // C_BLOCK):
        vb = v[blk * C_BLOCK:(blk + 1) * C_BLOCK]
        sv = bias_ref[...]
        for gi in range(C_GROUPS):
            sv = sv + jnp.dot(ws[gi], jnp.where(group == gi, vb, 0.0).astype(BF16),
                              preferred_element_type=F32)
        blocks.append(sv)
    sv = jnp.concatenate(blocks, axis=0) if len(blocks) > 1 else blocks[0]
    y = u * sv
    y_ref[...] = y * lax.rsqrt(jnp.mean(y * y, axis=-1, keepdims=True) + EPS) * og_ref[...]


def _gmlp(zc, ln_g, ln_b, w_s, b_s, out_g, tm=256):
    n = zc.shape[0]
    bias = jnp.repeat(jnp.transpose(b_s), C_GROUP_DIM, axis=1)
    consts = [ln_g.reshape(1, -1), ln_b.reshape(1, -1), w_s, bias, out_g.reshape(1, -1)]
    return pl.pallas_call(
        _gmlp_kernel,
        out_shape=jax.ShapeDtypeStruct((n, C_WIDTH), F32),
        grid=(n // tm,),
        in_specs=[_row_spec(tm, C_COLS)] + [_const_spec(c.shape) for c in consts],
        out_specs=_row_spec(tm, C_WIDTH),
        compiler_params=_cparams(("parallel",)),
        name="gmlp",
    )(zc, *consts)


def _out_proj_kernel(x_ref, ya_ref, o_ref, yc_ref, bg_ref, wa_ref, wb_ref, wc_ref, fg_ref,
                     xo_ref, h_ref):
    o = o_ref[...]
    yb = o * lax.rsqrt(jnp.mean(o * o, axis=-1, keepdims=True) + EPS) * bg_ref[...]
    x = (x_ref[...] + _dot(ya_ref[...], wa_ref[...]) + _dot(yb, wb_ref[...])
         + _dot(yc_ref[...], wc_ref[...]))
    xo_ref[...] = x
    h_ref[...] = (x * lax.rsqrt(jnp.mean(x * x, axis=-1, keepdims=True) + EPS) * fg_ref[...]).astype(BF16)


def _out_proj(x, ya, o, yc, b_out_g, w_out, ffn_g, tm=512):
    n = x.shape[0]
    wa = w_out[:A_WIDTH].astype(BF16)
    wb = w_out[A_WIDTH:A_WIDTH + B_WIDTH].astype(BF16)
    wc = w_out[A_WIDTH + B_WIDTH:].astype(BF16)
    consts = [b_out_g.reshape(1, -1), wa, wb, wc, ffn_g.reshape(1, -1)]
    return pl.pallas_call(
        _out_proj_kernel,
        out_shape=(jax.ShapeDtypeStruct((n, D_MODEL), F32), jax.ShapeDtypeStruct((n, D_MODEL), BF16)),
        grid=(n // tm,),
        in_specs=[_row_spec(tm, D_MODEL), _row_spec(tm, A_WIDTH), _row_spec(tm, B_WIDTH),
                  _row_spec(tm, C_WIDTH)] + [_const_spec(c.shape) for c in consts],
        out_specs=(_row_spec(tm, D_MODEL), _row_spec(tm, D_MODEL)),
        compiler_params=_cparams(("parallel",)),
        name="out_proj",
    )(x, ya, o, yc, *consts)


def _ffn_kernel(h_ref, x_ref, wg_ref, wu_ref, wd_ref, o_ref, acc_ref):
    f = pl.program_id(1)

    @pl.when(f == 0)
    def _():
        acc_ref[...] = x_ref[...]

    h = h_ref[...]
    gate = jnp.dot(h, wg_ref[...], preferred_element_type=F32)
    up = jnp.dot(h, wu_ref[...], preferred_element_type=F32)
    act = (gate * _sigmoid(gate) * up).astype(BF16)
    acc_ref[...] += jnp.dot(act, wd_ref[...], preferred_element_type=F32)

    @pl.when(f == pl.num_programs(1) - 1)
    def _():
        o_ref[...] = acc_ref[...]


def _pick_ff_tile(d_ff, target):
    best = LANES
    for t in range(LANES, d_ff + 1, LANES):
        if d_ff % t == 0 and t <= target:
            best = t
    return best


def _dense_ffn(x, h, wg, wu, wd, tm=1024, tf_target=1408):
    n = x.shape[0]
    tm = min(tm, n)
    d_ff = wg.shape[1]
    tf = _pick_ff_tile(d_ff, tf_target)
    return pl.pallas_call(
        _ffn_kernel,
        out_shape=jax.ShapeDtypeStruct((n, D_MODEL), F32),
        grid=(n // tm, d_ff // tf),
        in_specs=[pl.BlockSpec((tm, D_MODEL), lambda i, f: (i, 0)),
                  pl.BlockSpec((tm, D_MODEL), lambda i, f: (i, 0)),
                  pl.BlockSpec((D_MODEL, tf), lambda i, f: (0, f)),
                  pl.BlockSpec((D_MODEL, tf), lambda i, f: (0, f)),
                  pl.BlockSpec((tf, D_MODEL), lambda i, f: (f, 0))],
        out_specs=pl.BlockSpec((tm, D_MODEL), lambda i, f: (i, 0)),
        scratch_shapes=[pltpu.VMEM((tm, D_MODEL), F32)],
        compiler_params=_cparams(("parallel", "arbitrary")),
        name="dense_ffn",
    )(h, x, wg.astype(BF16), wu.astype(BF16), wd.astype(BF16))


def _router_kernel(x_ref, fg_ref, wr_ref, route_ref):
    x = x_ref[...]
    h = x * lax.rsqrt(jnp.mean(x * x, axis=-1, keepdims=True) + EPS) * fg_ref[...]
    logits = _dot_f32(h, wr_ref[...])
    lane = lax.broadcasted_iota(jnp.int32, logits.shape, 1)
    logits = jnp.where(lane < N_EXPERTS, logits, -jnp.inf)
    v1 = jnp.max(logits, axis=-1, keepdims=True)
    i1 = jnp.min(jnp.where(logits == v1, lane, LANES), axis=-1, keepdims=True)
    rest = jnp.where(lane == i1, -jnp.inf, logits)
    v2 = jnp.max(rest, axis=-1, keepdims=True)
    i2 = jnp.min(jnp.where(rest == v2, lane, LANES), axis=-1, keepdims=True)
    e2 = jnp.exp(v2 - v1)
    g1 = 1.0 / (1.0 + e2)
    g2 = e2 / (1.0 + e2)
    route_ref[...] = jnp.where(lane == 0, i1.astype(F32),
                               jnp.where(lane == 1, i2.astype(F32),
                                         jnp.where(lane == 2, g1, jnp.where(lane == 3, g2, 0.0))))


def _router(x, ffn_g, w_router, tm=512):
    n = x.shape[0]
    tm = min(tm, n)
    wr = jnp.pad(w_router, ((0, 0), (0, LANES - N_EXPERTS)))
    return pl.pallas_call(
        _router_kernel,
        out_shape=jax.ShapeDtypeStruct((n, LANES), F32),
        grid=(n // tm,),
        in_specs=[_row_spec(tm, D_MODEL), _const_spec((1, D_MODEL)), _const_spec(wr.shape)],
        out_specs=_row_spec(tm, LANES),
        compiler_params=_cparams(("parallel",)),
        name="moe_router",
    )(x, ffn_g.reshape(1, -1), wr)


def _moe_plan(route, tm):
    n = route.shape[0]
    experts = jnp.concatenate([route[:, 0], route[:, 1]]).astype(jnp.int32)
    onehot = (experts[:, None] == jnp.arange(N_EXPERTS, dtype=jnp.int32)[None, :]).astype(jnp.int32)
    csum = jnp.cumsum(onehot, axis=0)
    rank = jnp.sum(csum * onehot, axis=1) - 1
    counts = csum[-1]
    padded = (counts + tm - 1) // tm * tm
    ends = jnp.cumsum(padded)
    pos = (ends - padded)[experts] + rank
    n_rows = 2 * n + N_EXPERTS * tm
    tokens = jnp.concatenate([jnp.arange(n, dtype=jnp.int32)] * 2)
    src = jnp.zeros((n_rows,), jnp.int32).at[pos].set(tokens)
    tile_start = jnp.arange(n_rows // tm, dtype=jnp.int32) * tm
    tile_expert = jnp.minimum(jnp.sum(tile_start[:, None] >= ends[None, :], axis=1), N_EXPERTS - 1)
    n_used = (ends[-1] // tm).reshape(1)
    return pos.astype(jnp.int32), src, tile_expert.astype(jnp.int32), n_used.astype(jnp.int32)


def _moe_expert_kernel(te_ref, src_ref, nu_ref, x_hbm, fg_ref, wg_hbm, wu_hbm, wd_hbm, y_ref,
                       rows_ref, wg_ref, wu_ref, wd_ref, row_sem, w_sem, *, ff_chunk):
    t = pl.program_id(0)
    tm = y_ref.shape[0]
    d_ff = wg_ref.shape[1]
    n_used = nu_ref[0]
    slot = t % 2
    e = te_ref[t]

    def start_rows(tile, s):
        def body(r, c):
            pltpu.make_async_copy(x_hbm.at[pl.ds(src_ref[tile * tm + r], 1)],
                                  rows_ref.at[s, pl.ds(r, 1)], row_sem.at[s]).start()
            return c
        lax.fori_loop(0, tm, body, 0, unroll=8)

    def weight_copies():
        return (pltpu.make_async_copy(wg_hbm.at[e], wg_ref, w_sem.at[0]),
                pltpu.make_async_copy(wu_hbm.at[e], wu_ref, w_sem.at[1]),
                pltpu.make_async_copy(wd_hbm.at[e], wd_ref, w_sem.at[2]))

    new_expert = (t == 0) | (e != te_ref[jnp.maximum(t - 1, 0)])

    @pl.when(t == 0)
    def _():
        start_rows(0, 0)

    @pl.when((t < n_used) & new_expert)
    def _():
        for c in weight_copies():
            c.start()

    @pl.when(t + 1 < n_used)
    def _():
        start_rows(t + 1, 1 - slot)

    @pl.when(t < n_used)
    def _():
        pltpu.make_async_copy(x_hbm.at[pl.ds(0, tm)], rows_ref.at[slot], row_sem.at[slot]).wait()
        x = rows_ref[slot]
        h = (x * lax.rsqrt(jnp.mean(x * x, axis=-1, keepdims=True) + EPS) * fg_ref[...]).astype(BF16)

        @pl.when(new_expert)
        def _():
            for c in weight_copies():
                c.wait()

        acc = jnp.zeros((tm, D_MODEL), F32)
        for c0 in range(0, d_ff, ff_chunk):
            gate = jnp.dot(h, wg_ref[:, c0:c0 + ff_chunk], preferred_element_type=F32)
            up = jnp.dot(h, wu_ref[:, c0:c0 + ff_chunk], preferred_element_type=F32)
            act = (gate * _sigmoid(gate) * up).astype(BF16)
            acc = acc + jnp.dot(act, wd_ref[c0:c0 + ff_chunk, :], preferred_element_type=F32)
        y_ref[...] = acc

    @pl.when(t >= n_used)
    def _():
        y_ref[...] = jnp.zeros_like(y_ref)


def _moe_experts(x, src, tile_expert, n_used, ffn_g, wg, wu, wd, tm, ff_chunk=512):
    n_rows = src.shape[0]
    d_ff = wg.shape[2]
    any_spec = pl.BlockSpec(memory_space=pl.ANY)
    return pl.pallas_call(
        functools.partial(_moe_expert_kernel, ff_chunk=min(ff_chunk, d_ff)),
        out_shape=jax.ShapeDtypeStruct((n_rows, D_MODEL), F32),
        grid_spec=pltpu.PrefetchScalarGridSpec(
            num_scalar_prefetch=3,
            grid=(n_rows // tm,),
            in_specs=[any_spec, pl.BlockSpec((1, D_MODEL), lambda t, *_: (0, 0)),
                      any_spec, any_spec, any_spec],
            out_specs=pl.BlockSpec((tm, D_MODEL), lambda t, *_: (t, 0)),
            scratch_shapes=[pltpu.VMEM((2, tm, D_MODEL), F32),
                            pltpu.VMEM((D_MODEL, d_ff), BF16), pltpu.VMEM((D_MODEL, d_ff), BF16),
                            pltpu.VMEM((d_ff, D_MODEL), BF16),
                            pltpu.SemaphoreType.DMA((2,)), pltpu.SemaphoreType.DMA((3,))]),
        compiler_params=_cparams(("arbitrary",)),
        name="moe_experts",
    )(tile_expert, src, n_used, x, ffn_g.reshape(1, -1), wg.astype(BF16), wu.astype(BF16), wd.astype(BF16))


def _moe_combine_kernel(pos_ref, x_ref, route_ref, y_hbm, o_ref, buf_ref, sem, *, n_tokens):
    i = pl.program_id(0)
    tm = x_ref.shape[0]
    slot = i % 2

    def start_rows(tile, s):
        def body(r, c):
            tok = tile * tm + r
            for choice in range(2):
                pltpu.make_async_copy(y_hbm.at[pl.ds(pos_ref[choice * n_tokens + tok], 1)],
                                      buf_ref.at[s, choice, pl.ds(r, 1)], sem.at[s]).start()
            return c
        lax.fori_loop(0, tm, body, 0, unroll=4)

    @pl.when(i == 0)
    def _():
        start_rows(0, 0)

    @pl.when(i + 1 < pl.num_programs(0))
    def _():
        start_rows(i + 1, 1 - slot)

    for choice in range(2):
        pltpu.make_async_copy(y_hbm.at[pl.ds(0, tm)], buf_ref.at[slot, choice], sem.at[slot]).wait()
    route = route_ref[...]
    lane = lax.broadcasted_iota(jnp.int32, route.shape, 1)
    g1 = jnp.sum(jnp.where(lane == 2, route, 0.0), axis=-1, keepdims=True)
    g2 = jnp.sum(jnp.where(lane == 3, route, 0.0), axis=-1, keepdims=True)
    o_ref[...] = x_ref[...] + g1 * buf_ref[slot, 0] + g2 * buf_ref[slot, 1]


def _moe_combine(x, route, pos, y, tm=256):
    n = x.shape[0]
    tm = min(tm, n)
    return pl.pallas_call(
        functools.partial(_moe_combine_kernel, n_tokens=n),
        out_shape=jax.ShapeDtypeStruct((n, D_MODEL), F32),
        grid_spec=pltpu.PrefetchScalarGridSpec(
            num_scalar_prefetch=1,
            grid=(n // tm,),
            in_specs=[pl.BlockSpec((tm, D_MODEL), lambda i, *_: (i, 0)),
                      pl.BlockSpec((tm, LANES), lambda i, *_: (i, 0)),
                      pl.BlockSpec(memory_space=pl.ANY)],
            out_specs=pl.BlockSpec((tm, D_MODEL), lambda i, *_: (i, 0)),
            scratch_shapes=[pltpu.VMEM((2, 2, tm, D_MODEL), F32), pltpu.SemaphoreType.DMA((2,))]),
        compiler_params=_cparams(("arbitrary",)),
        name="moe_combine",
    )(pos, x, route, y)


def _moe(x, ffn_g, w_router, wg, wu, wd, tm=512):
    tm = min(tm, x.shape[0])
    route = _router(x, ffn_g, w_router)
    pos, src, tile_expert, n_used = _moe_plan(route, tm)
    y = _moe_experts(x, src, tile_expert, n_used, ffn_g, wg, wu, wd, tm)
    return _moe_combine(x, route, pos, y)


def kernel(x, positions, mix_norm_g, w_in, shift_mu, a_w0, a_w_up, a_a0, a_a_up, a_g_up, a_k_k, a_k_a,
           a_r_k, a_ln_g, a_ln_b, a_v0, a_v_down, a_v_up, b_q_norm_g, b_w_uq, b_kv_norm_g, b_w_ukv,
           b_q_head_g, b_k_head_g, b_out_g, c_ln_g, c_ln_b, c_w_s, c_b_s, c_out_g, w_out, ffn_norm_g,
           dense_w_gate, dense_w_up, dense_w_down, moe_router, moe_w_gate, moe_w_up, moe_w_down):
    n_batch, seq_len, d_model = x.shape
    depth = w_in.shape[0]
    xf = x.reshape(n_batch * seq_len, d_model)
    cos_t, sin_t = _rope_tables(positions)
    v_first = None
    for l in range(depth):
        za, zb, zc = _in_proj(xf, mix_norm_g[l], w_in[l], shift_mu[l], seq_len)
        vres = (a_v0[l - 1], a_v_down[l - 1], a_v_up[l - 1]) if l > 0 else (None, None, None)
        r, k, v, kk, a, lw, g, bonus = _rwkv_pre(
            za, a_w0[l], a_w_up[l], a_a0[l], a_a_up[l], a_g_up[l], a_k_k[l], a_k_a[l],
            a_r_k[l].reshape(-1), v_first if l > 0 else None, *vres)
        if l == 0:
            v_first = v
        ya = _rwkv_scan(r, k, v, kk, a, lw, g, bonus, a_ln_g[l], a_ln_b[l], n_batch, seq_len)
        q, kq, vq = _mla_pre(zb, cos_t, sin_t, b_q_norm_g[l], b_w_uq[l], b_kv_norm_g[l], b_w_ukv[l],
                             b_q_head_g[l], b_k_head_g[l])
        o = _attention(q, kq, vq, n_batch, seq_len)
        yc = _gmlp(zc, c_ln_g[l], c_ln_b[l], c_w_s[l], c_b_s[l], c_out_g[l])
        xf, h = _out_proj(xf, ya, o, yc, b_out_g[l], w_out[l], ffn_norm_g[l])
        i = l // 2
        if l % 2 == 0:
            xf = _dense_ffn(xf, h, dense_w_gate[i], dense_w_up[i], dense_w_down[i])
        else:
            xf = _moe(xf, ffn_norm_g[l], moe_router[i], moe_w_gate[i], moe_w_up[i], moe_w_down[i])
    return xf.reshape(n_batch, seq_len, d_model)
```

```python
import functools
import math

import jax
import jax.numpy as jnp
from jax import lax
from jax.experimental import pallas as pl
from jax.experimental.pallas import tpu as pltpu

F32 = jnp.float32
BF16 = jnp.bfloat16

D_MODEL = 1024
CHUNK = 64
A_HEADS = 6
A_HEAD_DIM = 64
A_WIDTH = A_HEADS * A_HEAD_DIM
A_W_LORA = 32
A_A_LORA = 32
A_G_LORA = 64
A_COLS = 3 * A_WIDTH + A_W_LORA + A_A_LORA + A_G_LORA
RWKV_GN_EPS = 64e-5
B_HEADS = 6
B_NOPE = 64
B_ROPE = 32
B_QK = B_NOPE + B_ROPE
B_V = 64
B_WIDTH = B_HEADS * B_V
B_Q_LORA = 256
B_KV_LORA = 128
B_COLS = B_Q_LORA + B_KV_LORA + B_ROPE
ROPE_THETA = 10000.0
C_GROUPS = 4
C_GROUP_DIM = 64
C_WIDTH = C_GROUPS * C_GROUP_DIM
C_BLOCK = 128
C_COLS = 2 * C_WIDTH
N_EXPERTS = 8
EPS = 1e-6

LANES = 128
SUBLANES = 8
VMEM_LIMIT_BYTES = 56 * 1024 * 1024

SCAN_CHUNK = 64
PAIR = 2 * A_HEAD_DIM
N_PAIRS = A_WIDTH // PAIR
B_HEAD_PAD = LANES
NEG_BIG = -1e30
LOG2_E = math.log2(math.e)


def _cparams(sem):
    return pltpu.CompilerParams(dimension_semantics=sem, vmem_limit_bytes=VMEM_LIMIT_BYTES)


def _dot(a, b):
    return jnp.dot(a.astype(BF16), b.astype(BF16), preferred_element_type=F32)


def _dot_nt(a, b):
    return lax.dot_general(a.astype(BF16), b.astype(BF16), (((1,), (1,)), ((), ())),
                           preferred_element_type=F32)


def _dot_split(a, b, split_lhs=True):
    x = a if split_lhs else b
    hi = x.astype(BF16)
    lo = (x - hi.astype(F32)).astype(BF16)
    if split_lhs:
        e = b.astype(BF16)
        return jnp.dot(hi, e, preferred_element_type=F32) + jnp.dot(lo, e, preferred_element_type=F32)
    e = a.astype(BF16)
    return jnp.dot(e, hi, preferred_element_type=F32) + jnp.dot(e, lo, preferred_element_type=F32)


def _dot3(a, b):
    a_hi = a.astype(BF16)
    a_lo = (a - a_hi.astype(F32)).astype(BF16)
    b_hi = b.astype(BF16)
    b_lo = (b - b_hi.astype(F32)).astype(BF16)
    return (jnp.dot(a_hi, b_hi, preferred_element_type=F32) + jnp.dot(a_lo, b_hi, preferred_element_type=F32)
            + jnp.dot(a_hi, b_lo, preferred_element_type=F32))


def _dot_state(m, h):
    return _dot(m, h)


def _sigmoid(x):
    return 1.0 / (1.0 + jnp.exp(-x))


def _const_spec(shape):
    nd = len(shape)
    return pl.BlockSpec(shape, lambda *_: (0,) * nd)


def _row_spec(tm, width):
    return pl.BlockSpec((tm, width), lambda i: (i, 0))


def _in_proj_kernel(x_ref, g_ref, wa_ref, wb_ref, wc_ref, mu_ref, za_ref, zb_ref, zc_ref, carry_ref,
                    *, tiles_per_seq):
    x = x_ref[...]
    ms = jnp.mean(x * x, axis=-1, keepdims=True)
    h = (x * lax.rsqrt(ms + EPS) * g_ref[...]).astype(BF16)
    za = jnp.dot(h, wa_ref[...], preferred_element_type=F32)
    zb_ref[...] = jnp.dot(h, wb_ref[...], preferred_element_type=F32)
    zc_ref[...] = jnp.dot(h, wc_ref[...], preferred_element_type=F32)
    tm = za.shape[0]
    first = jnp.where(pl.program_id(0) % tiles_per_seq == 0, 0.0, carry_ref[0:1, :])
    row = lax.broadcasted_iota(jnp.int32, za.shape, 0)
    prev = jnp.where(row == 0, first, pltpu.roll(za, 1, 0))
    carry_ref[0:1, :] = za[tm - 1:tm, :]
    za_ref[...] = za + (prev - za) * mu_ref[...]


def _in_proj(x, g, w_in, mu, seq_len, tm=256):
    n = x.shape[0]
    wa = w_in[:, :A_COLS].astype(BF16)
    wb = jnp.pad(w_in[:, A_COLS:A_COLS + B_COLS], ((0, 0), (0, 4 * LANES - B_COLS))).astype(BF16)
    wc = w_in[:, A_COLS + B_COLS:].astype(BF16)
    return pl.pallas_call(
        functools.partial(_in_proj_kernel, tiles_per_seq=seq_len // tm),
        out_shape=(jax.ShapeDtypeStruct((n, A_COLS), F32),
                   jax.ShapeDtypeStruct((n, 4 * LANES), F32),
                   jax.ShapeDtypeStruct((n, C_COLS), F32)),
        grid=(n // tm,),
        in_specs=[_row_spec(tm, D_MODEL), _const_spec((1, D_MODEL)), _const_spec(wa.shape),
                  _const_spec(wb.shape), _const_spec(wc.shape), _const_spec((1, A_COLS))],
        out_specs=(_row_spec(tm, A_COLS), _row_spec(tm, 4 * LANES), _row_spec(tm, C_COLS)),
        scratch_shapes=[pltpu.VMEM((SUBLANES, A_COLS), F32)],
        compiler_params=_cparams(("arbitrary",)),
        name="in_proj",
    )(x, g.reshape(1, -1), wa, wb, wc, mu.reshape(1, -1))


def _rwkv_pre_kernel(*refs, has_vres):
    if has_vres:
        (za_ref, w0_ref, wup_ref, a0_ref, aup_ref, gup_ref, kk_ref, ka_ref, rk_ref, bd_ref,
         vf_ref, v0_ref, vdn_ref, vup_ref,
         r_o, k_o, v_o, kk_o, a_o, lw_o, g_o, bonus_o) = refs
    else:
        (za_ref, w0_ref, wup_ref, a0_ref, aup_ref, gup_ref, kk_ref, ka_ref, rk_ref, bd_ref,
         r_o, k_o, v_o, kk_o, a_o, lw_o, g_o, bonus_o) = refs
    r = za_ref[:, 0:A_WIDTH]
    k = za_ref[:, A_WIDTH:2 * A_WIDTH]
    v = za_ref[:, 2 * A_WIDTH:3 * A_WIDTH]
    lora = za_ref[:, 3 * A_WIDTH:A_COLS]
    w = w0_ref[...] + _dot3(jnp.tanh(lora), wup_ref[...])
    w_log = -(jnp.maximum(-w, 0.0) + jnp.log(1.0 + jnp.exp(-jnp.abs(w)))) - 0.5
    lw_o[...] = -jnp.exp(w_log)
    a = _sigmoid(a0_ref[...] + _dot3(lora, aup_ref[...]))
    g_o[...] = _dot(_sigmoid(lora), gup_ref[...])
    if has_vres:
        mix = _sigmoid(v0_ref[...] + _dot3(_dot3(v, vdn_ref[...]), vup_ref[...]))
        v = v + (vf_ref[...] - v) * mix
    kk = k * kk_ref[...]
    ss = _dot_split(kk * kk, bd_ref[...])
    kk = kk / jnp.maximum(jnp.sqrt(ss), 1e-12)
    k = k * (1.0 + (a - 1.0) * ka_ref[...])
    r_o[...] = r
    k_o[...] = k
    v_o[...] = v
    kk_o[...] = kk
    a_o[...] = a
    bonus_o[...] = _dot_split(r * k * rk_ref[...], bd_ref[...]) * v


def _head_block_diag(width, head_dim):
    idx = jnp.arange(width) // head_dim
    return (idx[:, None] == idx[None, :]).astype(F32)


def _pad_rows(w, row_off, rows=LANES):
    return jnp.pad(w, ((row_off, rows - row_off - w.shape[0]), (0, 0)))


def _rwkv_pre(za, w0, w_up, a0, a_up, g_up, k_k, k_a, r_k, v_first, v0, v_down, v_up, tm=256):
    n = za.shape[0]
    has_vres = v_first is not None
    row = lambda t: t.reshape(1, -1)
    args = [za, row(w0), _pad_rows(w_up, 0), row(a0), _pad_rows(a_up, A_W_LORA),
            _pad_rows(g_up, A_W_LORA + A_A_LORA), row(k_k), row(k_a), row(r_k),
            _head_block_diag(A_WIDTH, A_HEAD_DIM)]
    specs = [_row_spec(tm, A_COLS)] + [_const_spec(a.shape) for a in args[1:]]
    if has_vres:
        vdn = jnp.pad(v_down, ((0, 0), (0, LANES - v_down.shape[1])))
        extra = [v_first, row(v0), vdn, _pad_rows(v_up, 0)]
        args += extra
        specs += [_row_spec(tm, A_WIDTH)] + [_const_spec(a.shape) for a in extra[1:]]
    out = jax.ShapeDtypeStruct((n, A_WIDTH), F32)
    return pl.pallas_call(
        functools.partial(_rwkv_pre_kernel, has_vres=has_vres),
        out_shape=(out,) * 8,
        grid=(n // tm,),
        in_specs=specs,
        out_specs=(_row_spec(tm, A_WIDTH),) * 8,
        compiler_params=_cparams(("parallel",)),
        name="rwkv_pre",
    )(*args)


def _stack_heads(x, lane_lo):
    return jnp.concatenate([jnp.where(lane_lo, x, 0.0), jnp.where(lane_lo, 0.0, x)], axis=0)


def _rwkv_scan_kernel(r_ref, k_ref, v_ref, kk_ref, a_ref, lw_ref, g_ref, bonus_ref, lng_ref, lnb_ref,
                      bd_ref, y_ref, h_ref):
    n_batch, rows_per_step, _ = r_ref.shape
    L = SCAN_CHUNK
    n_chunks = rows_per_step // L
    S2 = 2 * L

    @pl.when(pl.program_id(0) == 0)
    def _():
        h_ref[...] = jnp.zeros_like(h_ref)

    ti = lax.broadcasted_iota(jnp.int32, (L, L), 0)
    si = lax.broadcasted_iota(jnp.int32, (L, L), 1)
    tril_incl = (ti >= si).astype(F32)
    row_w = lax.broadcasted_iota(jnp.int32, (L, S2), 0)
    col_w = lax.broadcasted_iota(jnp.int32, (L, S2), 1) % L
    strict = row_w > col_w
    incl = row_w >= col_w
    eye_w = row_w == col_w
    eye = (lax.broadcasted_iota(jnp.int32, (PAIR, PAIR), 0)
           == lax.broadcasted_iota(jnp.int32, (PAIR, PAIR), 1))
    lane_lo = lax.broadcasted_iota(jnp.int32, (1, PAIR), 1) < A_HEAD_DIM
    zeros_s = jnp.zeros((S2, PAIR), F32)
    mid = L // 2 - 1

    chains = []
    for b, ck in [(b, ck) for b in range(n_batch) for ck in range(n_chunks)]:
        rows = slice(ck * L, (ck + 1) * L)
        lw = lw_ref[b, rows, :]
        cum = _dot_split(tril_incl, lw, split_lhs=False)
        c_mid = cum[mid:mid + 1, :]
        c_end = cum[L - 1:L, :]
        kk = kk_ref[b, rows, :]
        kb = kk * a_ref[b, rows, :]
        kmod = k_ref[b, rows, :]
        e_dn = jnp.exp(c_mid - cum)
        e_hat = jnp.exp(c_end - cum)
        at_all = -kk * jnp.exp(cum - lw - c_mid)
        bt_all = kb * e_dn
        kt_all = kmod * e_dn
        rt_all = r_ref[b, rows, :] * jnp.exp(cum - c_mid)
        bh_all = kb * e_hat
        kh_all = kmod * e_hat
        p_end = jnp.exp(c_end)
        p_mid = jnp.exp(c_mid)
        at_true = at_all * p_mid
        rt_true = rt_all * p_mid
        v_all = v_ref[b, rows, :]
        for p in range(N_PAIRS):
            sl = slice(p * PAIR, (p + 1) * PAIR)
            stack = lambda t: _stack_heads(t[:, sl], lane_lo)
            chains.append(dict(
                b=b, p=p, ck=ck, ar=jnp.concatenate([at_all[:, sl], rt_all[:, sl]], axis=0),
                bk_s=jnp.concatenate([stack(bt_all), stack(kt_all)], axis=0),
                bkh_t=jnp.concatenate([stack(bh_all).T, stack(kh_all).T], axis=1),
                v_s=stack(v_all), a_in_s=stack(at_true), r_in=rt_true[:, sl], p_end=p_end[:, sl]))
    stack = lambda t: _stack_heads(t, lane_lo)
    for c in chains:
        gram = _dot_nt(c["ar"], c["bk_s"])
        a_ab = jnp.where(strict, gram[:L, :S2], 0.0)
        c["a_ak"] = jnp.where(strict, gram[:L, S2:], 0.0)
        c["a_r"] = jnp.where(jnp.concatenate([incl, incl], axis=1), gram[L:], 0.0)
        c["x"] = a_ab
        c["t"] = jnp.where(eye_w, 1.0, a_ab)
    n_doublings = int(math.log2(L)) - 1
    for c in chains:
        c["x"] = _dot(c["x"], stack(c["x"]))
    for it in range(n_doublings):
        for c in chains:
            if it + 1 < n_doublings:
                both = _dot(c["x"], jnp.concatenate([stack(c["t"]), stack(c["x"])], axis=1))
                c["t"] = c["t"] + both[:, :S2]
                c["x"] = both[:, S2:]
            else:
                c["t"] = c["t"] + _dot(c["x"], stack(c["t"]))
    for c in chains:
        c["av"] = _dot(c["a_ak"], c["v_s"])
    for c in chains:
        tw = _dot(c["t"], jnp.concatenate([c["a_in_s"], stack(c["av"])], axis=1))
        c["rhs"] = jnp.concatenate(
            [jnp.concatenate([stack(tw[:, :PAIR]), stack(tw[:, PAIR:])], axis=1),
             jnp.concatenate([zeros_s, c["v_s"]], axis=1)], axis=0)
    for c in chains:
        out = _dot(jnp.concatenate([c["a_r"], c["bkh_t"]], axis=0), c["rhs"])
        c["q"] = c["r_in"] + out[:L, :PAIR]
        c["y0"] = out[:L, PAIR:]
        c["m"] = jnp.where(eye, c["p_end"], 0.0) + out[L:, :PAIR]
        c["c"] = out[L:, PAIR:]
    state = {(b, p): h_ref[b, p] for b in range(n_batch) for p in range(N_PAIRS)}
    for ck in range(n_chunks):
        for c in [c for c in chains if c["ck"] == ck]:
            out = _dot_state(jnp.concatenate([c["q"], c["m"]], axis=0), state[c["b"], c["p"]])
            c["y"] = out[:L] + c["y0"]
            state[c["b"], c["p"]] = out[L:] + c["c"]
    for (b, p), h in state.items():
        h_ref[b, p] = h
    for b in range(n_batch):
        y = jnp.concatenate(
            [jnp.concatenate([c["y"] for c in chains if c["b"] == b and c["ck"] == ck], axis=1)
             for ck in range(n_chunks)], axis=0)
        mu = _dot_split(y, bd_ref[...]) * (1.0 / A_HEAD_DIM)
        d = y - mu
        var = _dot_split(d * d, bd_ref[...]) * (1.0 / A_HEAD_DIM)
        yn = d * lax.rsqrt(var + RWKV_GN_EPS) * lng_ref[...] + lnb_ref[...]
        y_ref[b] = (yn + bonus_ref[b]) * g_ref[b]


def _rwkv_scan(r, k, v, kk, a, lw, g, bonus, ln_g, ln_b, n_batch, seq_len, chunks_per_step=4):
    L = SCAN_CHUNK * chunks_per_step
    shp = (n_batch, seq_len, A_WIDTH)
    seq_spec = pl.BlockSpec((n_batch, L, A_WIDTH), lambda i: (0, i, 0))
    args = [t.reshape(shp) for t in (r, k, v, kk, a, lw, g, bonus)]
    consts = [ln_g.reshape(1, -1), ln_b.reshape(1, -1), _head_block_diag(A_WIDTH, A_HEAD_DIM)]
    y = pl.pallas_call(
        _rwkv_scan_kernel,
        out_shape=jax.ShapeDtypeStruct(shp, F32),
        grid=(seq_len // L,),
        in_specs=[seq_spec] * 8 + [_const_spec(c.shape) for c in consts],
        out_specs=seq_spec,
        scratch_shapes=[pltpu.VMEM((n_batch, N_PAIRS, PAIR, PAIR), F32)],
        compiler_params=_cparams(("arbitrary",)),
        name="rwkv_scan",
    )(*args, *consts)
    return y.reshape(n_batch * seq_len, A_WIDTH)


def _exchange_rope_halves(t):
    half = B_ROPE // 2
    nope = jnp.zeros_like(t[..., :B_NOPE])
    return jnp.concatenate([nope, t[..., B_NOPE + half:], t[..., B_NOPE:B_NOPE + half]], axis=-1)


def _mla_pre_kernel(zb_ref, cos_ref, sin_ref, gq_ref, wuq_ref, gkv_ref, wk_ref, wv_ref,
                    gqh_ref, gkh_ref, q_o, k_o, v_o):
    cq = zb_ref[:, 0:B_Q_LORA]
    ckv = zb_ref[:, B_Q_LORA:B_Q_LORA + B_KV_LORA]
    kr_raw = zb_ref[:, B_Q_LORA + B_KV_LORA:]
    half = B_ROPE // 2
    lane = lax.broadcasted_iota(jnp.int32, kr_raw.shape, 1)
    kr = pltpu.roll(kr_raw, B_NOPE, 1)
    kr_sw = jnp.where(lane < B_NOPE + half, pltpu.roll(kr_raw, B_NOPE - half, 1),
                      pltpu.roll(kr_raw, B_NOPE + half, 1))
    qn = cq * lax.rsqrt(jnp.mean(cq * cq, axis=-1, keepdims=True) + EPS) * gq_ref[...]
    kvn = ckv * lax.rsqrt(jnp.mean(ckv * ckv, axis=-1, keepdims=True) + EPS) * gkv_ref[...]
    width = B_HEADS * B_HEAD_PAD
    q2 = _dot(qn, wuq_ref[...])
    k = _dot(kvn, wk_ref[...])
    v_o[...] = _dot(kvn, wv_ref[...]).astype(BF16)
    q_scale = LOG2_E / math.sqrt(B_QK)
    cos_t = cos_ref[...]
    sin_t = sin_ref[...]
    gq_cos = cos_t * (gqh_ref[0:1, :] * q_scale)
    gq_sin = sin_t * (gqh_ref[1:2, :] * q_scale)
    gk_cos = cos_t * gkh_ref[0:1, :]
    gk_sin = kr_sw * sin_t * gkh_ref[1:2, :]
    for h in range(B_HEADS):
        sl = slice(h * B_HEAD_PAD, (h + 1) * B_HEAD_PAD)
        qh = q2[:, sl]
        q_sw = q2[:, width + h * B_HEAD_PAD:width + (h + 1) * B_HEAD_PAD]
        kh = k[:, sl] + kr
        rs_q = lax.rsqrt(jnp.sum(qh * qh, axis=-1, keepdims=True) * (1.0 / B_QK) + EPS)
        rs_k = lax.rsqrt(jnp.sum(kh * kh, axis=-1, keepdims=True) * (1.0 / B_QK) + EPS)
        q_o[:, sl] = ((qh * gq_cos + q_sw * gq_sin) * rs_q).astype(BF16)
        k_o[:, sl] = ((kh * gk_cos + gk_sin) * rs_k).astype(BF16)


def _rope_tables(positions):
    half = B_ROPE // 2
    inv_freq = ROPE_THETA ** (-jnp.arange(0, B_ROPE, 2, dtype=F32) / B_ROPE)
    ang = positions.astype(F32).reshape(-1, 1) * inv_freq
    cos, sin = jnp.cos(ang), jnp.sin(ang)
    n = ang.shape[0]
    ones = jnp.ones((n, B_NOPE), F32)
    zeros = jnp.zeros((n, B_NOPE), F32)
    tail = jnp.zeros((n, B_HEAD_PAD - B_QK), F32)
    cos_t = jnp.concatenate([ones, cos, cos, tail], axis=1)
    sin_t = jnp.concatenate([zeros, -sin, sin, tail], axis=1)
    assert cos_t.shape[1] == B_HEAD_PAD and half * 2 == B_ROPE
    return cos_t, sin_t


def _pad_heads(w, heads, dim):
    kdim = w.shape[0]
    w = w.reshape(kdim, heads, dim)
    return jnp.pad(w, ((0, 0), (0, 0), (0, B_HEAD_PAD - dim))).reshape(kdim, heads * B_HEAD_PAD)


def _mla_pre(zb, cos_t, sin_t, q_norm_g, w_uq, kv_norm_g, w_ukv, q_head_g, k_head_g, tm=256):
    n = zb.shape[0]
    w_uq_x = _exchange_rope_halves(w_uq.reshape(B_Q_LORA, B_HEADS, B_QK)).reshape(B_Q_LORA, -1)
    wuq = jnp.concatenate([_pad_heads(w_uq, B_HEADS, B_QK), _pad_heads(w_uq_x, B_HEADS, B_QK)],
                          axis=1).astype(BF16)
    w_ukv = w_ukv.reshape(B_KV_LORA, B_HEADS, B_NOPE + B_V)
    wk = _pad_heads(w_ukv[:, :, :B_NOPE].reshape(B_KV_LORA, -1), B_HEADS, B_NOPE).astype(BF16)
    wv = w_ukv[:, :, B_NOPE:].reshape(B_KV_LORA, B_WIDTH).astype(BF16)
    pad_g = lambda gg: jnp.pad(jnp.stack([gg, _exchange_rope_halves(gg)]), ((0, 0), (0, B_HEAD_PAD - B_QK)))
    consts = [q_norm_g.reshape(1, -1), wuq, kv_norm_g.reshape(1, -1), wk, wv,
              pad_g(q_head_g), pad_g(k_head_g)]
    width = B_HEADS * B_HEAD_PAD
    return pl.pallas_call(
        _mla_pre_kernel,
        out_shape=(jax.ShapeDtypeStruct((n, width), BF16), jax.ShapeDtypeStruct((n, width), BF16),
                   jax.ShapeDtypeStruct((n, B_WIDTH), BF16)),
        grid=(n // tm,),
        in_specs=[_row_spec(tm, 4 * LANES), _row_spec(tm, LANES), _row_spec(tm, LANES)]
                 + [_const_spec(c.shape) for c in consts],
        out_specs=(_row_spec(tm, width), _row_spec(tm, width), _row_spec(tm, B_WIDTH)),
        compiler_params=_cparams(("parallel",)),
        name="mla_pre",
    )(zb, cos_t, sin_t, *consts)


def _attn_kernel(qt_ref, kt_ref, q_ref, k_ref, v_ref, o_ref, m_ref, acc_ref):
    t = pl.program_id(2)
    i = qt_ref[t]
    j = kt_ref[t]
    tk = k_ref.shape[0]

    @pl.when(j == 0)
    def _():
        m_ref[...] = jnp.full_like(m_ref, NEG_BIG)
        acc_ref[...] = jnp.zeros_like(acc_ref)

    def update(masked):
        v = v_ref[...]
        lane_v = lax.broadcasted_iota(jnp.int32, v.shape, 1)
        one = jnp.ones_like(v)
        for h in range(2):
            sl = slice(h * B_HEAD_PAD, (h + 1) * B_HEAD_PAD)
            s = lax.dot_general(q_ref[:, sl], k_ref[:, sl], (((1,), (1,)), ((), ())),
                                preferred_element_type=F32)
            if masked:
                qc = lax.broadcasted_iota(jnp.int32, s.shape, 0) // CHUNK
                kc = lax.broadcasted_iota(jnp.int32, s.shape, 1) // CHUNK
                s = jnp.where(kc <= qc, s, NEG_BIG)
            m_prev = m_ref[h]
            m_new = jnp.maximum(m_prev, jnp.max(s, axis=-1, keepdims=True))
            alpha = jnp.exp2(m_prev - m_new)
            p = jnp.exp2((s - jnp.concatenate([m_new] * (tk // LANES), axis=1)).astype(BF16))
            vh = jnp.where((lane_v < B_V) == (h == 0), v, one)
            acc_ref[h] = alpha * acc_ref[h] + jnp.dot(p, vh, preferred_element_type=F32)
            m_ref[h] = m_new

    @pl.when(j < i)
    def _():
        update(False)

    @pl.when(j == i)
    def _():
        update(True)
        lane = lax.broadcasted_iota(jnp.int32, o_ref.shape, 1)
        a0 = acc_ref[0]
        a1 = acc_ref[1]
        o_ref[...] = jnp.where(lane < B_V, a0 / pltpu.roll(a0, B_V, 1), a1 / pltpu.roll(a1, B_V, 1))


def _attention(q, k, v, n_batch, seq_len, tile=1024):
    tile = min(tile, seq_len)
    nt = seq_len // tile
    width = B_HEADS * B_HEAD_PAD
    q = q.reshape(n_batch, seq_len, width)
    k = k.reshape(n_batch, seq_len, width)
    v = v.reshape(n_batch, seq_len, B_WIDTH)
    q_tiles = jnp.asarray([i for i in range(nt) for _ in range(i + 1)], jnp.int32)
    k_tiles = jnp.asarray([j for i in range(nt) for j in range(i + 1)], jnp.int32)
    o = pl.pallas_call(
        _attn_kernel,
        out_shape=jax.ShapeDtypeStruct((n_batch, seq_len, B_WIDTH), F32),
        grid_spec=pltpu.PrefetchScalarGridSpec(
            num_scalar_prefetch=2,
            grid=(n_batch, B_HEADS // 2, nt * (nt + 1) // 2),
            in_specs=[pl.BlockSpec((None, tile, 2 * B_HEAD_PAD), lambda b, p, t, qt, kt: (b, qt[t], p)),
                      pl.BlockSpec((None, tile, 2 * B_HEAD_PAD), lambda b, p, t, qt, kt: (b, kt[t], p)),
                      pl.BlockSpec((None, tile, 2 * B_V), lambda b, p, t, qt, kt: (b, kt[t], p))],
            out_specs=pl.BlockSpec((None, tile, 2 * B_V), lambda b, p, t, qt, kt: (b, qt[t], p)),
            scratch_shapes=[pltpu.VMEM((2, tile, LANES), F32), pltpu.VMEM((2, tile, 2 * B_V), F32)]),
        compiler_params=_cparams(("parallel", "parallel", "arbitrary")),
        name="mla_attention",
    )(q_tiles, k_tiles, q, k, v)
    return o.reshape(n_batch * seq_len, B_WIDTH)


def _gmlp_kernel(zc_ref, lng_ref, lnb_ref, ws_ref, bias_ref, og_ref, y_ref):
    z = zc_ref[...]
    z = 0.5 * z * (1.0 + lax.erf(z * (1.0 / math.sqrt(2.0))))
    u = z[:, :C_WIDTH]
    v = z[:, C_WIDTH:]
    mu = jnp.mean(v, axis=-1, keepdims=True)
    d = v - mu
    var = jnp.mean(d * d, axis=-1, keepdims=True)
    v = d * lax.rsqrt(var + EPS) * lng_ref[...] + lnb_ref[...]
    ti = lax.broadcasted_iota(jnp.int32, (C_BLOCK, C_BLOCK), 0)
    si = lax.broadcasted_iota(jnp.int32, (C_BLOCK, C_BLOCK), 1)
    group = lax.broadcasted_iota(jnp.int32, (1, C_WIDTH), 1) // C_GROUP_DIM
    ws = [jnp.where(ti >= si, ws_ref[gi], 0.0).astype(BF16) for gi in range(C_GROUPS)]
    blocks = []
    for blk in range(v.shape[0] // C_BLOCK):
        vb = v[blk * C_BLOCK:(blk + 1) * C_BLOCK]
        sv = bias_ref[...]
        for gi in range(C_GROUPS):
            sv = sv + jnp.dot(ws[gi], jnp.where(group == gi, vb, 0.0).astype(BF16),
                              preferred_element_type=F32)
        blocks.append(sv)
    sv = jnp.concatenate(blocks, axis=0) if len(blocks) > 1 else blocks[0]
    y = u * sv
    y_ref[...] = y * lax.rsqrt(jnp.mean(y * y, axis=-1, keepdims=True) + EPS) * og_ref[...]


def _gmlp(zc, ln_g, ln_b, w_s, b_s, out_g, tm=256):
    n = zc.shape[0]
    bias = jnp.repeat(jnp.transpose(b_s), C_GROUP_DIM, axis=1)
    consts = [ln_g.reshape(1, -1), ln_b.reshape(1, -1), w_s, bias, out_g.reshape(1, -1)]
    return pl.pallas_call(
        _gmlp_kernel,
        out_shape=jax.ShapeDtypeStruct((n, C_WIDTH), F32),
        grid=(n // tm,),
        in_specs=[_row_spec(tm, C_COLS)] + [_const_spec(c.shape) for c in consts],
        out_specs=_row_spec(tm, C_WIDTH),
        compiler_params=_cparams(("parallel",)),
        name="gmlp",
    )(zc, *consts)


def _out_proj_kernel(x_ref, ya_ref, o_ref, yc_ref, bg_ref, wa_ref, wb_ref, wc_ref, fg_ref,
                     xo_ref, h_ref):
    o = o_ref[...]
    yb = o * lax.rsqrt(jnp.mean(o * o, axis=-1, keepdims=True) + EPS) * bg_ref[...]
    x = (x_ref[...] + _dot(ya_ref[...], wa_ref[...]) + _dot(yb, wb_ref[...])
         + _dot(yc_ref[...], wc_ref[...]))
    xo_ref[...] = x
    h_ref[...] = (x * lax.rsqrt(jnp.mean(x * x, axis=-1, keepdims=True) + EPS) * fg_ref[...]).astype(BF16)


def _out_proj(x, ya, o, yc, b_out_g, w_out, ffn_g, tm=512):
    n = x.shape[0]
    wa = w_out[:A_WIDTH].astype(BF16)
    wb = w_out[A_WIDTH:A_WIDTH + B_WIDTH].astype(BF16)
    wc = w_out[A_WIDTH + B_WIDTH:].astype(BF16)
    consts = [b_out_g.reshape(1, -1), wa, wb, wc, ffn_g.reshape(1, -1)]
    return pl.pallas_call(
        _out_proj_kernel,
        out_shape=(jax.ShapeDtypeStruct((n, D_MODEL), F32), jax.ShapeDtypeStruct((n, D_MODEL), BF16)),
        grid=(n // tm,),
        in_specs=[_row_spec(tm, D_MODEL), _row_spec(tm, A_WIDTH), _row_spec(tm, B_WIDTH),
                  _row_spec(tm, C_WIDTH)] + [_const_spec(c.shape) for c in consts],
        out_specs=(_row_spec(tm, D_MODEL), _row_spec(tm, D_MODEL)),
        compiler_params=_cparams(("parallel",)),
        name="out_proj",
    )(x, ya, o, yc, *consts)


def _ffn_kernel(h_ref, x_ref, wg_ref, wu_ref, wd_ref, o_ref, acc_ref):
    f = pl.program_id(1)

    @pl.when(f == 0)
    def _():
        acc_ref[...] = x_ref[...]

    h = h_ref[...]
    gate = jnp.dot(h, wg_ref[...], preferred_element_type=F32)
    up = jnp.dot(h, wu_ref[...], preferred_element_type=F32)
    act = (gate * _sigmoid(gate) * up).astype(BF16)
    acc_ref[...] += jnp.dot(act, wd_ref[...], preferred_element_type=F32)

    @pl.when(f == pl.num_programs(1) - 1)
    def _():
        o_ref[...] = acc_ref[...]


def _pick_ff_tile(d_ff, target):
    best = LANES
    for t in range(LANES, d_ff + 1, LANES):
        if d_ff % t == 0 and t <= target:
            best = t
    return best


def _dense_ffn(x, h, wg, wu, wd, tm=1024, tf_target=1408):
    n = x.shape[0]
    tm = min(tm, n)
    d_ff = wg.shape[1]
    tf = _pick_ff_tile(d_ff, tf_target)
    return pl.pallas_call(
        _ffn_kernel,
        out_shape=jax.ShapeDtypeStruct((n, D_MODEL), F32),
        grid=(n // tm, d_ff // tf),
        in_specs=[pl.BlockSpec((tm, D_MODEL), lambda i, f: (i, 0)),
                  pl.BlockSpec((tm, D_MODEL), lambda i, f: (i, 0)),
                  pl.BlockSpec((D_MODEL, tf), lambda i, f: (0, f)),
                  pl.BlockSpec((D_MODEL, tf), lambda i, f: (0, f)),
                  pl.BlockSpec((tf, D_MODEL), lambda i, f: (f, 0))],
        out_specs=pl.BlockSpec((tm, D_MODEL), lambda i, f: (i, 0)),
        scratch_shapes=[pltpu.VMEM((tm, D_MODEL), F32)],
        compiler_params=_cparams(("parallel", "arbitrary")),
        name="dense_ffn",
    )(h, x, wg.astype(BF16), wu.astype(BF16), wd.astype(BF16))


def _router_kernel(x_ref, fg_ref, wr_ref, route_ref):
    x = x_ref[...]
    h = x * lax.rsqrt(jnp.mean(x * x, axis=-1, keepdims=True) + EPS) * fg_ref[...]
    logits = _dot3(h, wr_ref[...])
    lane = lax.broadcasted_iota(jnp.int32, logits.shape, 1)
    logits = jnp.where(lane < N_EXPERTS, logits, -jnp.inf)
    v1 = jnp.max(logits, axis=-1, keepdims=True)
    i1 = jnp.min(jnp.where(logits == v1, lane, LANES), axis=-1, keepdims=True)
    rest = jnp.where(lane == i1, -jnp.inf, logits)
    v2 = jnp.max(rest, axis=-1, keepdims=True)
    i2 = jnp.min(jnp.where(rest == v2, lane, LANES), axis=-1, keepdims=True)
    e2 = jnp.exp(v2 - v1)
    g1 = 1.0 / (1.0 + e2)
    g2 = e2 / (1.0 + e2)
    route_ref[...] = jnp.where(lane == 0, i1.astype(F32),
                               jnp.where(lane == 1, i2.astype(F32),
                                         jnp.where(lane == 2, g1, jnp.where(lane == 3, g2, 0.0))))


def _router(x, ffn_g, w_router, tm=512):
    n = x.shape[0]
    tm = min(tm, n)
    wr = jnp.pad(w_router, ((0, 0), (0, LANES - N_EXPERTS)))
    return pl.pallas_call(
        _router_kernel,
        out_shape=jax.ShapeDtypeStruct((n, LANES), F32),
        grid=(n // tm,),
        in_specs=[_row_spec(tm, D_MODEL), _const_spec((1, D_MODEL)), _const_spec(wr.shape)],
        out_specs=_row_spec(tm, LANES),
        compiler_params=_cparams(("parallel",)),
        name="moe_router",
    )(x, ffn_g.reshape(1, -1), wr)


def _moe_plan(route, tm):
    n = route.shape[0]
    experts = jnp.concatenate([route[:, 0], route[:, 1]]).astype(jnp.int32)
    onehot = (experts[:, None] == jnp.arange(N_EXPERTS, dtype=jnp.int32)[None, :]).astype(jnp.int32)
    csum = jnp.cumsum(onehot, axis=0)
    rank = jnp.sum(csum * onehot, axis=1) - 1
    counts = csum[-1]
    padded = (counts + tm - 1) // tm * tm
    ends = jnp.cumsum(padded)
    pos = (ends - padded)[experts] + rank
    n_rows = 2 * n + N_EXPERTS * tm
    tokens = jnp.concatenate([jnp.arange(n, dtype=jnp.int32)] * 2)
    src = jnp.zeros((n_rows,), jnp.int32).at[pos].set(tokens)
    tile_start = jnp.arange(n_rows // tm, dtype=jnp.int32) * tm
    tile_expert = jnp.minimum(jnp.sum(tile_start[:, None] >= ends[None, :], axis=1), N_EXPERTS - 1)
    n_used = (ends[-1] // tm).reshape(1)
    return pos.astype(jnp.int32), src, tile_expert.astype(jnp.int32), n_used.astype(jnp.int32)


def _moe_expert_kernel(te_ref, src_ref, nu_ref, x_hbm, fg_ref, wg_hbm, wu_hbm, wd_hbm, y_ref,
                       rows_ref, wg_ref, wu_ref, wd_ref, row_sem, w_sem, *, ff_chunk):
    t = pl.program_id(0)
    tm = y_ref.shape[0]
    d_ff = wg_ref.shape[1]
    n_used = nu_ref[0]
    slot = t % 2
    e = te_ref[t]

    def row_copy(tile, s, r):
        return pltpu.make_async_copy(x_hbm.at[pl.ds(src_ref[tile * tm + r], 1)],
                                     rows_ref.at[s, pl.ds(r, 1)], row_sem.at[s])

    def wait_rows(s):
        pltpu.make_async_copy(x_hbm.at[pl.ds(0, tm)], rows_ref.at[s], row_sem.at[s]).wait()

    def weight_copies():
        return (pltpu.make_async_copy(wg_hbm.at[e], wg_ref, w_sem.at[0]),
                pltpu.make_async_copy(wu_hbm.at[e], wu_ref, w_sem.at[1]),
                pltpu.make_async_copy(wd_hbm.at[e], wd_ref, w_sem.at[2]))

    new_expert = (t == 0) | (e != te_ref[jnp.maximum(t - 1, 0)])

    @pl.when(t == 0)
    def _():
        def body(r, c):
            row_copy(0, 0, r).start()
            return c
        lax.fori_loop(0, tm, body, 0, unroll=8)

    @pl.when((t < n_used) & new_expert)
    def _():
        for c in weight_copies():
            c.start()

    @pl.when(t < n_used)
    def _():
        wait_rows(slot)
        x = rows_ref[slot]
        h = (x * lax.rsqrt(jnp.mean(x * x, axis=-1, keepdims=True) + EPS) * fg_ref[...]).astype(BF16)

        @pl.when(new_expert)
        def _():
            for c in weight_copies():
                c.wait()

        acc = jnp.zeros((tm, D_MODEL), F32)
        chunk_starts = list(range(0, d_ff, ff_chunk))
        rows_per_chunk = -(-tm // len(chunk_starts))
        for ci, c0 in enumerate(chunk_starts):
            for r in range(ci * rows_per_chunk, min((ci + 1) * rows_per_chunk, tm)):
                row_copy(t + 1, 1 - slot, r).start()
            gate = jnp.dot(h, wg_ref[:, c0:c0 + ff_chunk], preferred_element_type=F32)
            up = jnp.dot(h, wu_ref[:, c0:c0 + ff_chunk], preferred_element_type=F32)
            act = (gate * _sigmoid(gate) * up).astype(BF16)
            acc = acc + jnp.dot(act, wd_ref[c0:c0 + ff_chunk, :], preferred_element_type=F32)
        y_ref[...] = acc

    @pl.when(t == n_used)
    def _():
        wait_rows(slot)

    @pl.when((t == pl.num_programs(0) - 1) & (t < n_used))
    def _():
        wait_rows(1 - slot)

    @pl.when(t >= n_used)
    def _():
        y_ref[...] = jnp.zeros_like(y_ref)


def _moe_experts(x, src, tile_expert, n_used, ffn_g, wg, wu, wd, tm, ff_chunk=512):
    n_rows = src.shape[0]
    src = jnp.pad(src, (0, tm))
    d_ff = wg.shape[2]
    any_spec = pl.BlockSpec(memory_space=pl.ANY)
    return pl.pallas_call(
        functools.partial(_moe_expert_kernel, ff_chunk=min(ff_chunk, d_ff)),
        out_shape=jax.ShapeDtypeStruct((n_rows, D_MODEL), F32),
        grid_spec=pltpu.PrefetchScalarGridSpec(
            num_scalar_prefetch=3,
            grid=(n_rows // tm,),
            in_specs=[any_spec, pl.BlockSpec((1, D_MODEL), lambda t, *_: (0, 0)),
                      any_spec, any_spec, any_spec],
            out_specs=pl.BlockSpec((tm, D_MODEL), lambda t, *_: (t, 0)),
            scratch_shapes=[pltpu.VMEM((2, tm, D_MODEL), F32),
                            pltpu.VMEM((D_MODEL, d_ff), BF16), pltpu.VMEM((D_MODEL, d_ff), BF16),
                            pltpu.VMEM((d_ff, D_MODEL), BF16),
                            pltpu.SemaphoreType.DMA((2,)), pltpu.SemaphoreType.DMA((3,))]),
        compiler_params=_cparams(("arbitrary",)),
        name="moe_experts",
    )(tile_expert, src, n_used, x, ffn_g.reshape(1, -1), wg.astype(BF16), wu.astype(BF16), wd.astype(BF16))


def _moe_combine_kernel(pos_ref, x_ref, route_ref, y_hbm, o_ref, buf_ref, sem, *, n_tokens):
    i = pl.program_id(0)
    tm = x_ref.shape[0]
    slot = i % 2

    def start_rows(tile, s):
        def body(r, c):
            tok = tile * tm + r
            for choice in range(2):
                pltpu.make_async_copy(y_hbm.at[pl.ds(pos_ref[choice * n_tokens + tok], 1)],
                                      buf_ref.at[s, choice, pl.ds(r, 1)], sem.at[s]).start()
            return c
        lax.fori_loop(0, tm, body, 0, unroll=4)

    @pl.when(i == 0)
    def _():
        start_rows(0, 0)

    @pl.when(i + 1 < pl.num_programs(0))
    def _():
        start_rows(i + 1, 1 - slot)

    for choice in range(2):
        pltpu.make_async_copy(y_hbm.at[pl.ds(0, tm)], buf_ref.at[slot, choice], sem.at[slot]).wait()
    route = route_ref[...]
    lane = lax.broadcasted_iota(jnp.int32, route.shape, 1)
    g1 = jnp.sum(jnp.where(lane == 2, route, 0.0), axis=-1, keepdims=True)
    g2 = jnp.sum(jnp.where(lane == 3, route, 0.0), axis=-1, keepdims=True)
    o_ref[...] = x_ref[...] + g1 * buf_ref[slot, 0] + g2 * buf_ref[slot, 1]


def _moe_combine(x, route, pos, y, tm=256):
    n = x.shape[0]
    tm = min(tm, n)
    return pl.pallas_call(
        functools.partial(_moe_combine_kernel, n_tokens=n),
        out_shape=jax.ShapeDtypeStruct((n, D_MODEL), F32),
        grid_spec=pltpu.PrefetchScalarGridSpec(
            num_scalar_prefetch=1,
            grid=(n // tm,),
            in_specs=[pl.BlockSpec((tm, D_MODEL), lambda i, *_: (i, 0)),
                      pl.BlockSpec((tm, LANES), lambda i, *_: (i, 0)),
                      pl.BlockSpec(memory_space=pl.ANY)],
            out_specs=pl.BlockSpec((tm, D_MODEL), lambda i, *_: (i, 0)),
            scratch_shapes=[pltpu.VMEM((2, 2, tm, D_MODEL), F32), pltpu.SemaphoreType.DMA((2,))]),
        compiler_params=_cparams(("arbitrary",)),
        name="moe_combine",
    )(pos, x, route, y)


def _moe(x, ffn_g, w_router, wg, wu, wd, tm=512):
    tm = min(tm, x.shape[0])
    route = _router(x, ffn_g, w_router)
    pos, src, tile_expert, n_used = _moe_plan(route, tm)
    y = _moe_experts(x, src, tile_expert, n_used, ffn_g, wg, wu, wd, tm)
    return _moe_combine(x, route, pos, y)


def kernel(x, positions, mix_norm_g, w_in, shift_mu, a_w0, a_w_up, a_a0, a_a_up, a_g_up, a_k_k, a_k_a,
           a_r_k, a_ln_g, a_ln_b, a_v0, a_v_down, a_v_up, b_q_norm_g, b_w_uq, b_kv_norm_g, b_w_ukv,
           b_q_head_g, b_k_head_g, b_out_g, c_ln_g, c_ln_b, c_w_s, c_b_s, c_out_g, w_out, ffn_norm_g,
           dense_w_gate, dense_w_up, dense_w_down, moe_router, moe_w_gate, moe_w_up, moe_w_down):
    n_batch, seq_len, d_model = x.shape
    depth = w_in.shape[0]
    xf = x.reshape(n_batch * seq_len, d_model)
    cos_t, sin_t = _rope_tables(positions)
    v_first = None
    for l in range(depth):
        za, zb, zc = _in_proj(xf, mix_norm_g[l], w_in[l], shift_mu[l], seq_len)
        vres = (a_v0[l - 1], a_v_down[l - 1], a_v_up[l - 1]) if l > 0 else (None, None, None)
        r, k, v, kk, a, lw, g, bonus = _rwkv_pre(
            za, a_w0[l], a_w_up[l], a_a0[l], a_a_up[l], a_g_up[l], a_k_k[l], a_k_a[l],
            a_r_k[l].reshape(-1), v_first if l > 0 else None, *vres)
        if l == 0:
            v_first = v
        ya = _rwkv_scan(r, k, v, kk, a, lw, g, bonus, a_ln_g[l], a_ln_b[l], n_batch, seq_len)
        q, kq, vq = _mla_pre(zb, cos_t, sin_t, b_q_norm_g[l], b_w_uq[l], b_kv_norm_g[l], b_w_ukv[l],
                             b_q_head_g[l], b_k_head_g[l])
        o = _attention(q, kq, vq, n_batch, seq_len)
        yc = _gmlp(zc, c_ln_g[l], c_ln_b[l], c_w_s[l], c_b_s[l], c_out_g[l])
        xf, h = _out_proj(xf, ya, o, yc, b_out_g[l], w_out[l], ffn_norm_g[l])
        i = l // 2
        if l % 2 == 0:
            xf = _dense_ffn(xf, h, dense_w_gate[i], dense_w_up[i], dense_w_down[i])
        else:
            xf = _moe(xf, ffn_norm_g[l], moe_router[i], moe_w_gate[i], moe_w_up[i], moe_w_down[i])
    return xf.reshape(n_batch, seq_len, d_model)
```

```python
import functools
import math

import jax
import jax.numpy as jnp
from jax import lax
from jax.experimental import pallas as pl
from jax.experimental.pallas import tpu as pltpu

F32 = jnp.float32
BF16 = jnp.bfloat16

D_MODEL = 1024
CHUNK = 64
A_HEADS = 6
A_HEAD_DIM = 64
A_WIDTH = A_HEADS * A_HEAD_DIM
A_W_LORA = 32
A_A_LORA = 32
A_G_LORA = 64
A_COLS = 3 * A_WIDTH + A_W_LORA + A_A_LORA + A_G_LORA
RWKV_GN_EPS = 64e-5
B_HEADS = 6
B_NOPE = 64
B_ROPE = 32
B_QK = B_NOPE + B_ROPE
B_V = 64
B_WIDTH = B_HEADS * B_V
B_Q_LORA = 256
B_KV_LORA = 128
B_COLS = B_Q_LORA + B_KV_LORA + B_ROPE
ROPE_THETA = 10000.0
C_GROUPS = 4
C_GROUP_DIM = 64
C_WIDTH = C_GROUPS * C_GROUP_DIM
C_BLOCK = 128
C_COLS = 2 * C_WIDTH
N_EXPERTS = 8
EPS = 1e-6

LANES = 128
SUBLANES = 8
VMEM_LIMIT_BYTES = 56 * 1024 * 1024

SCAN_CHUNK = 64
PAIR = 2 * A_HEAD_DIM
N_PAIRS = A_WIDTH // PAIR
B_HEAD_PAD = LANES
NEG_BIG = -1e30
LOG2_E = math.log2(math.e)


def _cparams(sem):
    return pltpu.CompilerParams(dimension_semantics=sem, vmem_limit_bytes=VMEM_LIMIT_BYTES)


def _dot(a, b):
    return jnp.dot(a.astype(BF16), b.astype(BF16), preferred_element_type=F32)


def _dot_nt(a, b):
    return lax.dot_general(a.astype(BF16), b.astype(BF16), (((1,), (1,)), ((), ())),
                           preferred_element_type=F32)


def _dot_split(a, b, split_lhs=True):
    x = a if split_lhs else b
    hi = x.astype(BF16)
    lo = (x - hi.astype(F32)).astype(BF16)
    if split_lhs:
        e = b.astype(BF16)
        return jnp.dot(hi, e, preferred_element_type=F32) + jnp.dot(lo, e, preferred_element_type=F32)
    e = a.astype(BF16)
    return jnp.dot(e, hi, preferred_element_type=F32) + jnp.dot(e, lo, preferred_element_type=F32)


def _dot3(a, b):
    a_hi = a.astype(BF16)
    a_lo = (a - a_hi.astype(F32)).astype(BF16)
    b_hi = b.astype(BF16)
    b_lo = (b - b_hi.astype(F32)).astype(BF16)
    return (jnp.dot(a_hi, b_hi, preferred_element_type=F32) + jnp.dot(a_lo, b_hi, preferred_element_type=F32)
            + jnp.dot(a_hi, b_lo, preferred_element_type=F32))


def _dot_state(m, h):
    return _dot(m, h)


def _sigmoid(x):
    return 1.0 / (1.0 + jnp.exp(-x))


def _const_spec(shape):
    nd = len(shape)
    return pl.BlockSpec(shape, lambda *_: (0,) * nd)


def _row_spec(tm, width):
    return pl.BlockSpec((tm, width), lambda i: (i, 0))


def _in_proj_kernel(x_ref, g_ref, wa_ref, wb_ref, wc_ref, mu_ref, za_ref, zb_ref, zc_ref, carry_ref,
                    *, tiles_per_seq):
    x = x_ref[...]
    ms = jnp.mean(x * x, axis=-1, keepdims=True)
    h = (x * lax.rsqrt(ms + EPS) * g_ref[...]).astype(BF16)
    za = jnp.dot(h, wa_ref[...], preferred_element_type=F32)
    zb_ref[...] = jnp.dot(h, wb_ref[...], preferred_element_type=F32)
    zc_ref[...] = jnp.dot(h, wc_ref[...], preferred_element_type=F32)
    tm = za.shape[0]
    first = jnp.where(pl.program_id(0) % tiles_per_seq == 0, 0.0, carry_ref[0:1, :])
    row = lax.broadcasted_iota(jnp.int32, za.shape, 0)
    prev = jnp.where(row == 0, first, pltpu.roll(za, 1, 0))
    carry_ref[0:1, :] = za[tm - 1:tm, :]
    za_ref[...] = za + (prev - za) * mu_ref[...]


def _in_proj(x, g, w_in, mu, seq_len, tm=256):
    n = x.shape[0]
    wa = w_in[:, :A_COLS].astype(BF16)
    wb = jnp.pad(w_in[:, A_COLS:A_COLS + B_COLS], ((0, 0), (0, 4 * LANES - B_COLS))).astype(BF16)
    wc = w_in[:, A_COLS + B_COLS:].astype(BF16)
    return pl.pallas_call(
        functools.partial(_in_proj_kernel, tiles_per_seq=seq_len // tm),
        out_shape=(jax.ShapeDtypeStruct((n, A_COLS), F32),
                   jax.ShapeDtypeStruct((n, 4 * LANES), F32),
                   jax.ShapeDtypeStruct((n, C_COLS), F32)),
        grid=(n // tm,),
        in_specs=[_row_spec(tm, D_MODEL), _const_spec((1, D_MODEL)), _const_spec(wa.shape),
                  _const_spec(wb.shape), _const_spec(wc.shape), _const_spec((1, A_COLS))],
        out_specs=(_row_spec(tm, A_COLS), _row_spec(tm, 4 * LANES), _row_spec(tm, C_COLS)),
        scratch_shapes=[pltpu.VMEM((SUBLANES, A_COLS), F32)],
        compiler_params=_cparams(("arbitrary",)),
        name="in_proj",
    )(x, g.reshape(1, -1), wa, wb, wc, mu.reshape(1, -1))


def _rwkv_pre_kernel(*refs, has_vres):
    if has_vres:
        (za_ref, w0_ref, wup_ref, a0_ref, aup_ref, gup_ref, kk_ref, ka_ref, rk_ref, bd_ref,
         vf_ref, v0_ref, vdn_ref, vup_ref,
         r_o, k_o, v_o, kk_o, a_o, lw_o, g_o, bonus_o) = refs
    else:
        (za_ref, w0_ref, wup_ref, a0_ref, aup_ref, gup_ref, kk_ref, ka_ref, rk_ref, bd_ref,
         r_o, k_o, v_o, kk_o, a_o, lw_o, g_o, bonus_o) = refs
    r = za_ref[:, 0:A_WIDTH]
    k = za_ref[:, A_WIDTH:2 * A_WIDTH]
    v = za_ref[:, 2 * A_WIDTH:3 * A_WIDTH]
    lora = za_ref[:, 3 * A_WIDTH:A_COLS]
    w = w0_ref[...] + _dot3(jnp.tanh(lora), wup_ref[...])
    w_log = -(jnp.maximum(-w, 0.0) + jnp.log(1.0 + jnp.exp(-jnp.abs(w)))) - 0.5
    lw_o[...] = -jnp.exp(w_log)
    a = _sigmoid(a0_ref[...] + _dot3(lora, aup_ref[...]))
    g_o[...] = _dot(_sigmoid(lora), gup_ref[...])
    if has_vres:
        mix = _sigmoid(v0_ref[...] + _dot3(_dot3(v, vdn_ref[...]), vup_ref[...]))
        v = v + (vf_ref[...] - v) * mix
    kk = k * kk_ref[...]
    ss = _dot_split(kk * kk, bd_ref[...])
    kk = kk / jnp.maximum(jnp.sqrt(ss), 1e-12)
    k = k * (1.0 + (a - 1.0) * ka_ref[...])
    r_o[...] = r
    k_o[...] = k
    v_o[...] = v
    kk_o[...] = kk
    a_o[...] = a
    bonus_o[...] = _dot_split(r * k * rk_ref[...], bd_ref[...]) * v


def _head_block_diag(width, head_dim):
    idx = jnp.arange(width) // head_dim
    return (idx[:, None] == idx[None, :]).astype(F32)


def _pad_rows(w, row_off, rows=LANES):
    return jnp.pad(w, ((row_off, rows - row_off - w.shape[0]), (0, 0)))


def _rwkv_pre(za, w0, w_up, a0, a_up, g_up, k_k, k_a, r_k, v_first, v0, v_down, v_up, tm=256):
    n = za.shape[0]
    has_vres = v_first is not None
    row = lambda t: t.reshape(1, -1)
    args = [za, row(w0), _pad_rows(w_up, 0), row(a0), _pad_rows(a_up, A_W_LORA),
            _pad_rows(g_up, A_W_LORA + A_A_LORA), row(k_k), row(k_a), row(r_k),
            _head_block_diag(A_WIDTH, A_HEAD_DIM)]
    specs = [_row_spec(tm, A_COLS)] + [_const_spec(a.shape) for a in args[1:]]
    if has_vres:
        vdn = jnp.pad(v_down, ((0, 0), (0, LANES - v_down.shape[1])))
        extra = [v_first, row(v0), vdn, _pad_rows(v_up, 0)]
        args += extra
        specs += [_row_spec(tm, A_WIDTH)] + [_const_spec(a.shape) for a in extra[1:]]
    out = jax.ShapeDtypeStruct((n, A_WIDTH), F32)
    return pl.pallas_call(
        functools.partial(_rwkv_pre_kernel, has_vres=has_vres),
        out_shape=(out,) * 8,
        grid=(n // tm,),
        in_specs=specs,
        out_specs=(_row_spec(tm, A_WIDTH),) * 8,
        compiler_params=_cparams(("parallel",)),
        name="rwkv_pre",
    )(*args)


def _stack_heads(x, lane_lo):
    return jnp.concatenate([jnp.where(lane_lo, x, 0.0), jnp.where(lane_lo, 0.0, x)], axis=0)


def _rwkv_scan_kernel(r_ref, k_ref, v_ref, kk_ref, a_ref, lw_ref, g_ref, bonus_ref, lng_ref, lnb_ref,
                      bd_ref, y_ref, h_ref):
    n_batch, rows_per_step, _ = r_ref.shape
    L = SCAN_CHUNK
    n_chunks = rows_per_step // L
    S2 = 2 * L

    @pl.when(pl.program_id(0) == 0)
    def _():
        h_ref[...] = jnp.zeros_like(h_ref)

    ti = lax.broadcasted_iota(jnp.int32, (L, L), 0)
    si = lax.broadcasted_iota(jnp.int32, (L, L), 1)
    tril_incl = (ti >= si).astype(F32)
    row_w = lax.broadcasted_iota(jnp.int32, (L, S2), 0)
    col_w = lax.broadcasted_iota(jnp.int32, (L, S2), 1) % L
    strict = row_w > col_w
    incl = row_w >= col_w
    eye_w = row_w == col_w
    eye = (lax.broadcasted_iota(jnp.int32, (PAIR, PAIR), 0)
           == lax.broadcasted_iota(jnp.int32, (PAIR, PAIR), 1))
    lane_lo = lax.broadcasted_iota(jnp.int32, (1, PAIR), 1) < A_HEAD_DIM
    zeros_s = jnp.zeros((S2, PAIR), F32)
    mid = L // 2 - 1

    chains = []
    for b, ck in [(b, ck) for b in range(n_batch) for ck in range(n_chunks)]:
        rows = slice(ck * L, (ck + 1) * L)
        lw = lw_ref[b, rows, :]
        cum = _dot_split(tril_incl, lw, split_lhs=False)
        c_mid = cum[mid:mid + 1, :]
        c_end = cum[L - 1:L, :]
        kk = kk_ref[b, rows, :]
        kb = kk * a_ref[b, rows, :]
        kmod = k_ref[b, rows, :]
        e_dn = jnp.exp(c_mid - cum)
        e_hat = jnp.exp(c_end - cum)
        at_all = -kk * jnp.exp(cum - lw - c_mid)
        bt_all = kb * e_dn
        kt_all = kmod * e_dn
        rt_all = r_ref[b, rows, :] * jnp.exp(cum - c_mid)
        bh_all = kb * e_hat
        kh_all = kmod * e_hat
        p_end = jnp.exp(c_end)
        p_mid = jnp.exp(c_mid)
        at_true = at_all * p_mid
        rt_true = rt_all * p_mid
        v_all = v_ref[b, rows, :]
        for p in range(N_PAIRS):
            sl = slice(p * PAIR, (p + 1) * PAIR)
            stack = lambda t: _stack_heads(t[:, sl], lane_lo)
            chains.append(dict(
                b=b, p=p, ck=ck, ar=jnp.concatenate([at_all[:, sl], rt_all[:, sl]], axis=0),
                bk_s=jnp.concatenate([stack(bt_all), stack(kt_all)], axis=0),
                bkh_t=jnp.concatenate([stack(bh_all).T, stack(kh_all).T], axis=1),
                v_s=stack(v_all), a_in_s=stack(at_true), r_in=rt_true[:, sl], p_end=p_end[:, sl]))
    stack = lambda t: _stack_heads(t, lane_lo)
    for c in chains:
        gram = _dot_nt(c["ar"], c["bk_s"])
        a_ab = jnp.where(strict, gram[:L, :S2], 0.0)
        c["a_ak"] = jnp.where(strict, gram[:L, S2:], 0.0)
        c["a_r"] = jnp.where(jnp.concatenate([incl, incl], axis=1), gram[L:], 0.0)
        c["x"] = a_ab
        c["t"] = jnp.where(eye_w, 1.0, a_ab)
    n_doublings = int(math.log2(L)) - 1
    for c in chains:
        c["x"] = _dot(c["x"], stack(c["x"]))
    for it in range(n_doublings):
        for c in chains:
            if it + 1 < n_doublings:
                both = _dot(c["x"], jnp.concatenate([stack(c["t"]), stack(c["x"])], axis=1))
                c["t"] = c["t"] + both[:, :S2]
                c["x"] = both[:, S2:]
            else:
                c["t"] = c["t"] + _dot(c["x"], stack(c["t"]))
    for c in chains:
        c["av"] = _dot(c["a_ak"], c["v_s"])
    for c in chains:
        tw = _dot(c["t"], jnp.concatenate([c["a_in_s"], stack(c["av"])], axis=1))
        c["rhs"] = jnp.concatenate(
            [jnp.concatenate([stack(tw[:, :PAIR]), stack(tw[:, PAIR:])], axis=1),
             jnp.concatenate([zeros_s, c["v_s"]], axis=1)], axis=0)
    for c in chains:
        out = _dot(jnp.concatenate([c["a_r"], c["bkh_t"]], axis=0), c["rhs"])
        c["q"] = c["r_in"] + out[:L, :PAIR]
        c["y0"] = out[:L, PAIR:]
        c["m"] = jnp.where(eye, c["p_end"], 0.0) + out[L:, :PAIR]
        c["c"] = out[L:, PAIR:]
    state = {(b, p): h_ref[b, p] for b in range(n_batch) for p in range(N_PAIRS)}
    for ck in range(n_chunks):
        for c in [c for c in chains if c["ck"] == ck]:
            out = _dot_state(jnp.concatenate([c["q"], c["m"]], axis=0), state[c["b"], c["p"]])
            c["y"] = out[:L] + c["y0"]
            state[c["b"], c["p"]] = out[L:] + c["c"]
    for (b, p), h in state.items():
        h_ref[b, p] = h
    for b in range(n_batch):
        y = jnp.concatenate(
            [jnp.concatenate([c["y"] for c in chains if c["b"] == b and c["ck"] == ck], axis=1)
             for ck in range(n_chunks)], axis=0)
        mu = _dot_split(y, bd_ref[...]) * (1.0 / A_HEAD_DIM)
        d = y - mu
        var = _dot_split(d * d, bd_ref[...]) * (1.0 / A_HEAD_DIM)
        yn = d * lax.rsqrt(var + RWKV_GN_EPS) * lng_ref[...] + lnb_ref[...]
        y_ref[b] = (yn + bonus_ref[b]) * g_ref[b]


def _rwkv_scan(r, k, v, kk, a, lw, g, bonus, ln_g, ln_b, n_batch, seq_len, chunks_per_step=4):
    L = SCAN_CHUNK * chunks_per_step
    shp = (n_batch, seq_len, A_WIDTH)
    seq_spec = pl.BlockSpec((n_batch, L, A_WIDTH), lambda i: (0, i, 0))
    args = [t.reshape(shp) for t in (r, k, v, kk, a, lw, g, bonus)]
    consts = [ln_g.reshape(1, -1), ln_b.reshape(1, -1), _head_block_diag(A_WIDTH, A_HEAD_DIM)]
    y = pl.pallas_call(
        _rwkv_scan_kernel,
        out_shape=jax.ShapeDtypeStruct(shp, F32),
        grid=(seq_len // L,),
        in_specs=[seq_spec] * 8 + [_const_spec(c.shape) for c in consts],
        out_specs=seq_spec,
        scratch_shapes=[pltpu.VMEM((n_batch, N_PAIRS, PAIR, PAIR), F32)],
        compiler_params=_cparams(("arbitrary",)),
        name="rwkv_scan",
    )(*args, *consts)
    return y.reshape(n_batch * seq_len, A_WIDTH)


def _exchange_rope_halves(t):
    half = B_ROPE // 2
    nope = jnp.zeros_like(t[..., :B_NOPE])
    return jnp.concatenate([nope, t[..., B_NOPE + half:], t[..., B_NOPE:B_NOPE + half]], axis=-1)


def _mla_pre_kernel(zb_ref, cos_ref, sin_ref, gq_ref, wuq_ref, gkv_ref, wk_ref, wv_ref,
                    gqh_ref, gkh_ref, q_o, k_o, v_o):
    cq = zb_ref[:, 0:B_Q_LORA]
    ckv = zb_ref[:, B_Q_LORA:B_Q_LORA + B_KV_LORA]
    kr_raw = zb_ref[:, B_Q_LORA + B_KV_LORA:]
    half = B_ROPE // 2
    lane = lax.broadcasted_iota(jnp.int32, kr_raw.shape, 1)
    kr = pltpu.roll(kr_raw, B_NOPE, 1)
    kr_sw = jnp.where(lane < B_NOPE + half, pltpu.roll(kr_raw, B_NOPE - half, 1),
                      pltpu.roll(kr_raw, B_NOPE + half, 1))
    qn = cq * lax.rsqrt(jnp.mean(cq * cq, axis=-1, keepdims=True) + EPS) * gq_ref[...]
    kvn = ckv * lax.rsqrt(jnp.mean(ckv * ckv, axis=-1, keepdims=True) + EPS) * gkv_ref[...]
    width = B_HEADS * B_HEAD_PAD
    q2 = _dot(qn, wuq_ref[...])
    k = _dot(kvn, wk_ref[...])
    v_o[...] = _dot(kvn, wv_ref[...]).astype(BF16)
    q_scale = LOG2_E / math.sqrt(B_QK)
    cos_t = cos_ref[...]
    sin_t = sin_ref[...]
    gq_cos = cos_t * (gqh_ref[0:1, :] * q_scale)
    gq_sin = sin_t * (gqh_ref[1:2, :] * q_scale)
    gk_cos = cos_t * gkh_ref[0:1, :]
    gk_sin = kr_sw * sin_t * gkh_ref[1:2, :]
    for h in range(B_HEADS):
        sl = slice(h * B_HEAD_PAD, (h + 1) * B_HEAD_PAD)
        qh = q2[:, sl]
        q_sw = q2[:, width + h * B_HEAD_PAD:width + (h + 1) * B_HEAD_PAD]
        kh = k[:, sl] + kr
        rs_q = lax.rsqrt(jnp.sum(qh * qh, axis=-1, keepdims=True) * (1.0 / B_QK) + EPS)
        rs_k = lax.rsqrt(jnp.sum(kh * kh, axis=-1, keepdims=True) * (1.0 / B_QK) + EPS)
        q_o[:, sl] = ((qh * gq_cos + q_sw * gq_sin) * rs_q).astype(BF16)
        k_o[:, sl] = ((kh * gk_cos + gk_sin) * rs_k).astype(BF16)


def _rope_tables(positions):
    half = B_ROPE // 2
    inv_freq = ROPE_THETA ** (-jnp.arange(0, B_ROPE, 2, dtype=F32) / B_ROPE)
    ang = positions.astype(F32).reshape(-1, 1) * inv_freq
    cos, sin = jnp.cos(ang), jnp.sin(ang)
    n = ang.shape[0]
    ones = jnp.ones((n, B_NOPE), F32)
    zeros = jnp.zeros((n, B_NOPE), F32)
    tail = jnp.zeros((n, B_HEAD_PAD - B_QK), F32)
    cos_t = jnp.concatenate([ones, cos, cos, tail], axis=1)
    sin_t = jnp.concatenate([zeros, -sin, sin, tail], axis=1)
    assert cos_t.shape[1] == B_HEAD_PAD and half * 2 == B_ROPE
    return cos_t, sin_t


def _pad_heads(w, heads, dim):
    kdim = w.shape[0]
    w = w.reshape(kdim, heads, dim)
    return jnp.pad(w, ((0, 0), (0, 0), (0, B_HEAD_PAD - dim))).reshape(kdim, heads * B_HEAD_PAD)


def _mla_pre(zb, cos_t, sin_t, q_norm_g, w_uq, kv_norm_g, w_ukv, q_head_g, k_head_g, tm=256):
    n = zb.shape[0]
    w_uq_x = _exchange_rope_halves(w_uq.reshape(B_Q_LORA, B_HEADS, B_QK)).reshape(B_Q_LORA, -1)
    wuq = jnp.concatenate([_pad_heads(w_uq, B_HEADS, B_QK), _pad_heads(w_uq_x, B_HEADS, B_QK)],
                          axis=1).astype(BF16)
    w_ukv = w_ukv.reshape(B_KV_LORA, B_HEADS, B_NOPE + B_V)
    wk = _pad_heads(w_ukv[:, :, :B_NOPE].reshape(B_KV_LORA, -1), B_HEADS, B_NOPE).astype(BF16)
    wv = w_ukv[:, :, B_NOPE:].reshape(B_KV_LORA, B_WIDTH).astype(BF16)
    pad_g = lambda gg: jnp.pad(jnp.stack([gg, _exchange_rope_halves(gg)]), ((0, 0), (0, B_HEAD_PAD - B_QK)))
    consts = [q_norm_g.reshape(1, -1), wuq, kv_norm_g.reshape(1, -1), wk, wv,
              pad_g(q_head_g), pad_g(k_head_g)]
    width = B_HEADS * B_HEAD_PAD
    return pl.pallas_call(
        _mla_pre_kernel,
        out_shape=(jax.ShapeDtypeStruct((n, width), BF16), jax.ShapeDtypeStruct((n, width), BF16),
                   jax.ShapeDtypeStruct((n, B_WIDTH), BF16)),
        grid=(n // tm,),
        in_specs=[_row_spec(tm, 4 * LANES), _row_spec(tm, LANES), _row_spec(tm, LANES)]
                 + [_const_spec(c.shape) for c in consts],
        out_specs=(_row_spec(tm, width), _row_spec(tm, width), _row_spec(tm, B_WIDTH)),
        compiler_params=_cparams(("parallel",)),
        name="mla_pre",
    )(zb, cos_t, sin_t, *consts)


def _attn_kernel(qt_ref, kt_ref, q_ref, k_ref, v_ref, o_ref, m_ref, acc_ref):
    t = pl.program_id(2)
    i = qt_ref[t]
    j = kt_ref[t]
    tk = k_ref.shape[0]

    @pl.when(j == 0)
    def _():
        m_ref[...] = jnp.full_like(m_ref, NEG_BIG)
        acc_ref[...] = jnp.zeros_like(acc_ref)

    def update(rows, cols, masked):
        v = v_ref[cols, :]
        n_cols = v.shape[0]
        lane_v = lax.broadcasted_iota(jnp.int32, v.shape, 1)
        one = jnp.ones_like(v)
        for h in range(2):
            sl = slice(h * B_HEAD_PAD, (h + 1) * B_HEAD_PAD)
            s = lax.dot_general(q_ref[rows, sl], k_ref[cols, sl], (((1,), (1,)), ((), ())),
                                preferred_element_type=F32)
            if masked:
                qc = lax.broadcasted_iota(jnp.int32, s.shape, 0) // CHUNK
                kc = lax.broadcasted_iota(jnp.int32, s.shape, 1) // CHUNK
                s = jnp.where(kc <= qc, s, NEG_BIG)
            m_prev = m_ref[h, rows, :]
            m_new = jnp.maximum(m_prev, jnp.max(s, axis=-1, keepdims=True))
            alpha = jnp.exp2(m_prev - m_new)
            p = jnp.exp2((s - jnp.concatenate([m_new] * (n_cols // LANES), axis=1)).astype(BF16))
            vh = jnp.where((lane_v < B_V) == (h == 0), v, one)
            acc_ref[h, rows, :] = alpha * acc_ref[h, rows, :] + jnp.dot(p, vh, preferred_element_type=F32)
            m_ref[h, rows, :] = m_new

    @pl.when(j < i)
    def _():
        update(slice(0, tk), slice(0, tk), False)

    @pl.when(j == i)
    def _():
        update(slice(0, tk), slice(0, tk), True)
        lane = lax.broadcasted_iota(jnp.int32, o_ref.shape, 1)
        a0 = acc_ref[0]
        a1 = acc_ref[1]
        o_ref[...] = jnp.where(lane < B_V, a0 / pltpu.roll(a0, B_V, 1), a1 / pltpu.roll(a1, B_V, 1))


def _attention(q, k, v, n_batch, seq_len, tile=1024):
    tile = min(tile, seq_len)
    nt = seq_len // tile
    width = B_HEADS * B_HEAD_PAD
    q = q.reshape(n_batch, seq_len, width)
    k = k.reshape(n_batch, seq_len, width)
    v = v.reshape(n_batch, seq_len, B_WIDTH)
    q_tiles = jnp.asarray([i for i in range(nt) for _ in range(i + 1)], jnp.int32)
    k_tiles = jnp.asarray([j for i in range(nt) for j in range(i + 1)], jnp.int32)
    o = pl.pallas_call(
        _attn_kernel,
        out_shape=jax.ShapeDtypeStruct((n_batch, seq_len, B_WIDTH), F32),
        grid_spec=pltpu.PrefetchScalarGridSpec(
            num_scalar_prefetch=2,
            grid=(n_batch, B_HEADS // 2, nt * (nt + 1) // 2),
            in_specs=[pl.BlockSpec((None, tile, 2 * B_HEAD_PAD), lambda b, p, t, qt, kt: (b, qt[t], p)),
                      pl.BlockSpec((None, tile, 2 * B_HEAD_PAD), lambda b, p, t, qt, kt: (b, kt[t], p)),
                      pl.BlockSpec((None, tile, 2 * B_V), lambda b, p, t, qt, kt: (b, kt[t], p))],
            out_specs=pl.BlockSpec((None, tile, 2 * B_V), lambda b, p, t, qt, kt: (b, qt[t], p)),
            scratch_shapes=[pltpu.VMEM((2, tile, LANES), F32), pltpu.VMEM((2, tile, 2 * B_V), F32)]),
        compiler_params=_cparams(("parallel", "parallel", "arbitrary")),
        name="mla_attention",
    )(q_tiles, k_tiles, q, k, v)
    return o.reshape(n_batch * seq_len, B_WIDTH)


def _gmlp_kernel(zc_ref, lng_ref, lnb_ref, ws_ref, bias_ref, og_ref, y_ref):
    z = zc_ref[...]
    z = 0.5 * z * (1.0 + lax.erf(z * (1.0 / math.sqrt(2.0))))
    u = z[:, :C_WIDTH]
    v = z[:, C_WIDTH:]
    mu = jnp.mean(v, axis=-1, keepdims=True)
    d = v - mu
    var = jnp.mean(d * d, axis=-1, keepdims=True)
    v = d * lax.rsqrt(var + EPS) * lng_ref[...] + lnb_ref[...]
    ti = lax.broadcasted_iota(jnp.int32, (C_BLOCK, C_BLOCK), 0)
    si = lax.broadcasted_iota(jnp.int32, (C_BLOCK, C_BLOCK), 1)
    group = lax.broadcasted_iota(jnp.int32, (1, C_WIDTH), 1) // C_GROUP_DIM
    ws = [jnp.where(ti >= si, ws_ref[gi], 0.0).astype(BF16) for gi in range(C_GROUPS)]
    blocks = []
    for blk in range(v.shape[0] // C_BLOCK):
        vb = v[blk * C_BLOCK:(blk + 1) * C_BLOCK]
        sv = bias_ref[...]
        for gi in range(C_GROUPS):
            sv = sv + jnp.dot(ws[gi], jnp.where(group == gi, vb, 0.0).astype(BF16),
                              preferred_element_type=F32)
        blocks.append(sv)
    sv = jnp.concatenate(blocks, axis=0) if len(blocks) > 1 else blocks[0]
    y = u * sv
    y_ref[...] = y * lax.rsqrt(jnp.mean(y * y, axis=-1, keepdims=True) + EPS) * og_ref[...]


def _gmlp(zc, ln_g, ln_b, w_s, b_s, out_g, tm=256):
    n = zc.shape[0]
    bias = jnp.repeat(jnp.transpose(b_s), C_GROUP_DIM, axis=1)
    consts = [ln_g.reshape(1, -1), ln_b.reshape(1, -1), w_s, bias, out_g.reshape(1, -1)]
    return pl.pallas_call(
        _gmlp_kernel,
        out_shape=jax.ShapeDtypeStruct((n, C_WIDTH), F32),
        grid=(n // tm,),
        in_specs=[_row_spec(tm, C_COLS)] + [_const_spec(c.shape) for c in consts],
        out_specs=_row_spec(tm, C_WIDTH),
        compiler_params=_cparams(("parallel",)),
        name="gmlp",
    )(zc, *consts)


def _out_proj_kernel(x_ref, ya_ref, o_ref, yc_ref, bg_ref, wa_ref, wb_ref, wc_ref, fg_ref,
                     xo_ref, h_ref):
    o = o_ref[...]
    yb = o * lax.rsqrt(jnp.mean(o * o, axis=-1, keepdims=True) + EPS) * bg_ref[...]
    x = (x_ref[...] + _dot(ya_ref[...], wa_ref[...]) + _dot(yb, wb_ref[...])
         + _dot(yc_ref[...], wc_ref[...]))
    xo_ref[...] = x
    h_ref[...] = (x * lax.rsqrt(jnp.mean(x * x, axis=-1, keepdims=True) + EPS) * fg_ref[...]).astype(BF16)


def _out_proj(x, ya, o, yc, b_out_g, w_out, ffn_g, tm=512):
    n = x.shape[0]
    wa = w_out[:A_WIDTH].astype(BF16)
    wb = w_out[A_WIDTH:A_WIDTH + B_WIDTH].astype(BF16)
    wc = w_out[A_WIDTH + B_WIDTH:].astype(BF16)
    consts = [b_out_g.reshape(1, -1), wa, wb, wc, ffn_g.reshape(1, -1)]
    return pl.pallas_call(
        _out_proj_kernel,
        out_shape=(jax.ShapeDtypeStruct((n, D_MODEL), F32), jax.ShapeDtypeStruct((n, D_MODEL), BF16)),
        grid=(n // tm,),
        in_specs=[_row_spec(tm, D_MODEL), _row_spec(tm, A_WIDTH), _row_spec(tm, B_WIDTH),
                  _row_spec(tm, C_WIDTH)] + [_const_spec(c.shape) for c in consts],
        out_specs=(_row_spec(tm, D_MODEL), _row_spec(tm, D_MODEL)),
        compiler_params=_cparams(("parallel",)),
        name="out_proj",
    )(x, ya, o, yc, *consts)


def _load_weight_bf16(w_hbm, dst_ref, stage_ref, sem):
    rows = dst_ref.shape[0]
    step = stage_ref.shape[1]
    starts = list(range(0, rows, step))

    def copy(i):
        n = min(step, rows - starts[i])
        return pltpu.make_async_copy(w_hbm.at[pl.ds(starts[i], n)], stage_ref.at[i % 2, pl.ds(0, n)],
                                     sem.at[i % 2])

    copy(0).start()
    for i, r0 in enumerate(starts):
        if i + 1 < len(starts):
            copy(i + 1).start()
        copy(i).wait()
        n = min(step, rows - r0)
        dst_ref[r0:r0 + n, :] = stage_ref[i % 2, 0:n, :].astype(BF16)


def _ff_chunks(d_ff, chunk):
    return [(c0, min(chunk, d_ff - c0)) for c0 in range(0, d_ff, chunk)]


def _swiglu_resident(h, acc, wg_ref, wu_ref, wd_ref, chunk, between=None):
    for ci, (c0, cn) in enumerate(_ff_chunks(wg_ref.shape[1], chunk)):
        if between is not None:
            between(ci)
        gate = jnp.dot(h, wg_ref[:, c0:c0 + cn], preferred_element_type=F32)
        up = jnp.dot(h, wu_ref[:, c0:c0 + cn], preferred_element_type=F32)
        act = (gate * _sigmoid(gate) * up).astype(BF16)
        acc = acc + jnp.dot(act, wd_ref[c0:c0 + cn, :], preferred_element_type=F32)
    return acc


def _ffn_kernel(h_ref, x_ref, wg_hbm, wu_hbm, wd_hbm, o_ref, wg_ref, wu_ref, wd_ref, stage_in, stage_out,
                sem, *, ff_chunk):
    @pl.when(pl.program_id(0) == 0)
    def _():
        _load_weight_bf16(wg_hbm, wg_ref, stage_in, sem)
        _load_weight_bf16(wu_hbm, wu_ref, stage_in, sem)
        _load_weight_bf16(wd_hbm, wd_ref, stage_out, sem)

    o_ref[...] = _swiglu_resident(h_ref[...], x_ref[...], wg_ref, wu_ref, wd_ref, ff_chunk)


STAGE_ROWS_IN = 128
STAGE_ROWS_OUT = 512


def _weight_scratch(d_ff):
    return [pltpu.VMEM((D_MODEL, d_ff), BF16), pltpu.VMEM((D_MODEL, d_ff), BF16),
            pltpu.VMEM((d_ff, D_MODEL), BF16),
            pltpu.VMEM((2, STAGE_ROWS_IN, d_ff), F32), pltpu.VMEM((2, STAGE_ROWS_OUT, D_MODEL), F32)]


def _dense_ffn(x, h, wg, wu, wd, tm=512, ff_chunk=512):
    n = x.shape[0]
    tm = min(tm, n)
    d_ff = wg.shape[1]
    any_spec = pl.BlockSpec(memory_space=pl.ANY)
    return pl.pallas_call(
        functools.partial(_ffn_kernel, ff_chunk=ff_chunk),
        out_shape=jax.ShapeDtypeStruct((n, D_MODEL), F32),
        grid=(n // tm,),
        in_specs=[_row_spec(tm, D_MODEL), _row_spec(tm, D_MODEL), any_spec, any_spec, any_spec],
        out_specs=_row_spec(tm, D_MODEL),
        scratch_shapes=_weight_scratch(d_ff) + [pltpu.SemaphoreType.DMA((2,))],
        compiler_params=_cparams(("arbitrary",)),
        name="dense_ffn",
    )(h, x, wg, wu, wd)


def _router_kernel(x_ref, fg_ref, wr_ref, route_ref):
    x = x_ref[...]
    h = x * lax.rsqrt(jnp.mean(x * x, axis=-1, keepdims=True) + EPS) * fg_ref[...]
    logits = _dot3(h, wr_ref[...])
    lane = lax.broadcasted_iota(jnp.int32, logits.shape, 1)
    logits = jnp.where(lane < N_EXPERTS, logits, -jnp.inf)
    v1 = jnp.max(logits, axis=-1, keepdims=True)
    i1 = jnp.min(jnp.where(logits == v1, lane, LANES), axis=-1, keepdims=True)
    rest = jnp.where(lane == i1, -jnp.inf, logits)
    v2 = jnp.max(rest, axis=-1, keepdims=True)
    i2 = jnp.min(jnp.where(rest == v2, lane, LANES), axis=-1, keepdims=True)
    e2 = jnp.exp(v2 - v1)
    g1 = 1.0 / (1.0 + e2)
    g2 = e2 / (1.0 + e2)
    route_ref[...] = jnp.where(lane == 0, i1.astype(F32),
                               jnp.where(lane == 1, i2.astype(F32),
                                         jnp.where(lane == 2, g1, jnp.where(lane == 3, g2, 0.0))))


def _router(x, ffn_g, w_router, tm=512):
    n = x.shape[0]
    tm = min(tm, n)
    wr = jnp.pad(w_router, ((0, 0), (0, LANES - N_EXPERTS)))
    return pl.pallas_call(
        _router_kernel,
        out_shape=jax.ShapeDtypeStruct((n, LANES), F32),
        grid=(n // tm,),
        in_specs=[_row_spec(tm, D_MODEL), _const_spec((1, D_MODEL)), _const_spec(wr.shape)],
        out_specs=_row_spec(tm, LANES),
        compiler_params=_cparams(("parallel",)),
        name="moe_router",
    )(x, ffn_g.reshape(1, -1), wr)


def _moe_plan(route, tm):
    n = route.shape[0]
    experts = jnp.concatenate([route[:, 0], route[:, 1]]).astype(jnp.int32)
    onehot = (experts[:, None] == jnp.arange(N_EXPERTS, dtype=jnp.int32)[None, :]).astype(jnp.int32)
    csum = jnp.cumsum(onehot, axis=0)
    rank = jnp.sum(csum * onehot, axis=1) - 1
    counts = csum[-1]
    padded = (counts + tm - 1) // tm * tm
    ends = jnp.cumsum(padded)
    pos = (ends - padded)[experts] + rank
    n_rows = 2 * n + N_EXPERTS * tm
    tokens = jnp.concatenate([jnp.arange(n, dtype=jnp.int32)] * 2)
    src = jnp.zeros((n_rows,), jnp.int32).at[pos].set(tokens)
    tile_start = jnp.arange(n_rows // tm, dtype=jnp.int32) * tm
    tile_expert = jnp.minimum(jnp.sum(tile_start[:, None] >= ends[None, :], axis=1), N_EXPERTS - 1)
    n_used = (ends[-1] // tm).reshape(1)
    return pos.astype(jnp.int32), src, tile_expert.astype(jnp.int32), n_used.astype(jnp.int32)


def _moe_expert_kernel(te_ref, src_ref, nu_ref, x_hbm, fg_ref, wg_hbm, wu_hbm, wd_hbm, y_ref,
                       rows_ref, wg_ref, wu_ref, wd_ref, stage_in, stage_out, row_sem, w_sem, *, ff_chunk):
    t = pl.program_id(0)
    tm = y_ref.shape[0]
    n_used = nu_ref[0]
    slot = t % 2
    e = te_ref[t]

    def row_copy(tile, s, r):
        return pltpu.make_async_copy(x_hbm.at[pl.ds(src_ref[tile * tm + r], 1)],
                                     rows_ref.at[s, pl.ds(r, 1)], row_sem.at[s])

    def wait_rows(s):
        pltpu.make_async_copy(x_hbm.at[pl.ds(0, tm)], rows_ref.at[s], row_sem.at[s]).wait()

    new_expert = (t == 0) | (e != te_ref[jnp.maximum(t - 1, 0)])

    @pl.when(t == 0)
    def _():
        def body(r, c):
            row_copy(0, 0, r).start()
            return c
        lax.fori_loop(0, tm, body, 0, unroll=8)

    @pl.when((t < n_used) & new_expert)
    def _():
        _load_weight_bf16(wg_hbm.at[e], wg_ref, stage_in, w_sem)
        _load_weight_bf16(wu_hbm.at[e], wu_ref, stage_in, w_sem)
        _load_weight_bf16(wd_hbm.at[e], wd_ref, stage_out, w_sem)

    @pl.when(t < n_used)
    def _():
        wait_rows(slot)
        x = rows_ref[slot]
        h = (x * lax.rsqrt(jnp.mean(x * x, axis=-1, keepdims=True) + EPS) * fg_ref[...]).astype(BF16)
        rows_per_chunk = -(-tm // len(_ff_chunks(wg_ref.shape[1], ff_chunk)))

        def prefetch_rows(ci):
            for r in range(ci * rows_per_chunk, min((ci + 1) * rows_per_chunk, tm)):
                row_copy(t + 1, 1 - slot, r).start()

        y_ref[...] = _swiglu_resident(h, jnp.zeros((tm, D_MODEL), F32), wg_ref, wu_ref, wd_ref, ff_chunk,
                                      between=prefetch_rows)

    @pl.when(t == n_used)
    def _():
        wait_rows(slot)

    @pl.when((t == pl.num_programs(0) - 1) & (t < n_used))
    def _():
        wait_rows(1 - slot)

    @pl.when(t >= n_used)
    def _():
        y_ref[...] = jnp.zeros_like(y_ref)


def _moe_experts(x, src, tile_expert, n_used, ffn_g, wg, wu, wd, tm, ff_chunk=512):
    n_rows = src.shape[0]
    src = jnp.pad(src, (0, tm))
    d_ff = wg.shape[2]
    any_spec = pl.BlockSpec(memory_space=pl.ANY)
    return pl.pallas_call(
        functools.partial(_moe_expert_kernel, ff_chunk=min(ff_chunk, d_ff)),
        out_shape=jax.ShapeDtypeStruct((n_rows, D_MODEL), F32),
        grid_spec=pltpu.PrefetchScalarGridSpec(
            num_scalar_prefetch=3,
            grid=(n_rows // tm,),
            in_specs=[any_spec, pl.BlockSpec((1, D_MODEL), lambda t, *_: (0, 0)),
                      any_spec, any_spec, any_spec],
            out_specs=pl.BlockSpec((tm, D_MODEL), lambda t, *_: (t, 0)),
            scratch_shapes=[pltpu.VMEM((2, tm, D_MODEL), F32)] + _weight_scratch(d_ff)
                           + [pltpu.SemaphoreType.DMA((2,)), pltpu.SemaphoreType.DMA((2,))]),
        compiler_params=_cparams(("arbitrary",)),
        name="moe_experts",
    )(tile_expert, src, n_used, x, ffn_g.reshape(1, -1), wg, wu, wd)


def _moe_combine_kernel(pos_ref, x_ref, route_ref, y_hbm, o_ref, buf_ref, sem, *, n_tokens):
    i = pl.program_id(0)
    tm = x_ref.shape[0]
    slot = i % 2

    def start_rows(tile, s):
        def body(r, c):
            tok = tile * tm + r
            for choice in range(2):
                pltpu.make_async_copy(y_hbm.at[pl.ds(pos_ref[choice * n_tokens + tok], 1)],
                                      buf_ref.at[s, choice, pl.ds(r, 1)], sem.at[s]).start()
            return c
        lax.fori_loop(0, tm, body, 0, unroll=4)

    @pl.when(i == 0)
    def _():
        start_rows(0, 0)

    @pl.when(i + 1 < pl.num_programs(0))
    def _():
        start_rows(i + 1, 1 - slot)

    for choice in range(2):
        pltpu.make_async_copy(y_hbm.at[pl.ds(0, tm)], buf_ref.at[slot, choice], sem.at[slot]).wait()
    route = route_ref[...]
    lane = lax.broadcasted_iota(jnp.int32, route.shape, 1)
    g1 = jnp.sum(jnp.where(lane == 2, route, 0.0), axis=-1, keepdims=True)
    g2 = jnp.sum(jnp.where(lane == 3, route, 0.0), axis=-1, keepdims=True)
    o_ref[...] = x_ref[...] + g1 * buf_ref[slot, 0] + g2 * buf_ref[slot, 1]


def _moe_combine(x, route, pos, y, tm=256):
    n = x.shape[0]
    tm = min(tm, n)
    return pl.pallas_call(
        functools.partial(_moe_combine_kernel, n_tokens=n),
        out_shape=jax.ShapeDtypeStruct((n, D_MODEL), F32),
        grid_spec=pltpu.PrefetchScalarGridSpec(
            num_scalar_prefetch=1,
            grid=(n // tm,),
            in_specs=[pl.BlockSpec((tm, D_MODEL), lambda i, *_: (i, 0)),
                      pl.BlockSpec((tm, LANES), lambda i, *_: (i, 0)),
                      pl.BlockSpec(memory_space=pl.ANY)],
            out_specs=pl.BlockSpec((tm, D_MODEL), lambda i, *_: (i, 0)),
            scratch_shapes=[pltpu.VMEM((2, 2, tm, D_MODEL), F32), pltpu.SemaphoreType.DMA((2,))]),
        compiler_params=_cparams(("arbitrary",)),
        name="moe_combine",
    )(pos, x, route, y)


def _moe(x, ffn_g, w_router, wg, wu, wd, tm=512):
    tm = min(tm, x.shape[0])
    route = _router(x, ffn_g, w_router)
    pos, src, tile_expert, n_used = _moe_plan(route, tm)
    y = _moe_experts(x, src, tile_expert, n_used, ffn_g, wg, wu, wd, tm)
    return _moe_combine(x, route, pos, y)


def kernel(x, positions, mix_norm_g, w_in, shift_mu, a_w0, a_w_up, a_a0, a_a_up, a_g_up, a_k_k, a_k_a,
           a_r_k, a_ln_g, a_ln_b, a_v0, a_v_down, a_v_up, b_q_norm_g, b_w_uq, b_kv_norm_g, b_w_ukv,
           b_q_head_g, b_k_head_g, b_out_g, c_ln_g, c_ln_b, c_w_s, c_b_s, c_out_g, w_out, ffn_norm_g,
           dense_w_gate, dense_w_up, dense_w_down, moe_router, moe_w_gate, moe_w_up, moe_w_down):
    n_batch, seq_len, d_model = x.shape
    depth = w_in.shape[0]
    xf = x.reshape(n_batch * seq_len, d_model)
    cos_t, sin_t = _rope_tables(positions)
    v_first = None
    for l in range(depth):
        za, zb, zc = _in_proj(xf, mix_norm_g[l], w_in[l], shift_mu[l], seq_len)
        vres = (a_v0[l - 1], a_v_down[l - 1], a_v_up[l - 1]) if l > 0 else (None, None, None)
        r, k, v, kk, a, lw, g, bonus = _rwkv_pre(
            za, a_w0[l], a_w_up[l], a_a0[l], a_a_up[l], a_g_up[l], a_k_k[l], a_k_a[l],
            a_r_k[l].reshape(-1), v_first if l > 0 else None, *vres)
        if l == 0:
            v_first = v
        ya = _rwkv_scan(r, k, v, kk, a, lw, g, bonus, a_ln_g[l], a_ln_b[l], n_batch, seq_len)
        q, kq, vq = _mla_pre(zb, cos_t, sin_t, b_q_norm_g[l], b_w_uq[l], b_kv_norm_g[l], b_w_ukv[l],
                             b_q_head_g[l], b_k_head_g[l])
        o = _attention(q, kq, vq, n_batch, seq_len)
        yc = _gmlp(zc, c_ln_g[l], c_ln_b[l], c_w_s[l], c_b_s[l], c_out_g[l])
        xf, h = _out_proj(xf, ya, o, yc, b_out_g[l], w_out[l], ffn_norm_g[l])
        i = l // 2
        if l % 2 == 0:
            xf = _dense_ffn(xf, h, dense_w_gate[i], dense_w_up[i], dense_w_down[i])
        else:
            xf = _moe(xf, ffn_norm_g[l], moe_router[i], moe_w_gate[i], moe_w_up[i], moe_w_down[i])
    return xf.reshape(n_batch, seq_len, d_model)
```

```python
import functools
import math

import jax
import jax.numpy as jnp
from jax import lax
from jax.experimental import pallas as pl
from jax.experimental.pallas import tpu as pltpu

F32 = jnp.float32
BF16 = jnp.bfloat16

D_MODEL = 1024
CHUNK = 64
A_HEADS = 6
A_HEAD_DIM = 64
A_WIDTH = A_HEADS * A_HEAD_DIM
A_W_LORA = 32
A_A_LORA = 32
A_G_LORA = 64
A_COLS = 3 * A_WIDTH + A_W_LORA + A_A_LORA + A_G_LORA
RWKV_GN_EPS = 64e-5
B_HEADS = 6
B_NOPE = 64
B_ROPE = 32
B_QK = B_NOPE + B_ROPE
B_V = 64
B_WIDTH = B_HEADS * B_V
B_Q_LORA = 256
B_KV_LORA = 128
B_COLS = B_Q_LORA + B_KV_LORA + B_ROPE
ROPE_THETA = 10000.0
C_GROUPS = 4
C_GROUP_DIM = 64
C_WIDTH = C_GROUPS * C_GROUP_DIM
C_BLOCK = 128
C_COLS = 2 * C_WIDTH
N_EXPERTS = 8
EPS = 1e-6

LANES = 128
SUBLANES = 8
VMEM_LIMIT_BYTES = 56 * 1024 * 1024

SCAN_CHUNK = 64
PAIR = 2 * A_HEAD_DIM
N_PAIRS = A_WIDTH // PAIR
B_HEAD_PAD = LANES
NEG_BIG = -1e30
LOG2_E = math.log2(math.e)


def _cparams(sem):
    return pltpu.CompilerParams(dimension_semantics=sem, vmem_limit_bytes=VMEM_LIMIT_BYTES)


def _dot(a, b):
    return jnp.dot(a.astype(BF16), b.astype(BF16), preferred_element_type=F32)


def _dot_nt(a, b):
    return lax.dot_general(a.astype(BF16), b.astype(BF16), (((1,), (1,)), ((), ())),
                           preferred_element_type=F32)


def _dot_split(a, b, split_lhs=True):
    x = a if split_lhs else b
    hi = x.astype(BF16)
    lo = (x - hi.astype(F32)).astype(BF16)
    if split_lhs:
        e = b.astype(BF16)
        return jnp.dot(hi, e, preferred_element_type=F32) + jnp.dot(lo, e, preferred_element_type=F32)
    e = a.astype(BF16)
    return jnp.dot(e, hi, preferred_element_type=F32) + jnp.dot(e, lo, preferred_element_type=F32)


def _dot3(a, b):
    a_hi = a.astype(BF16)
    a_lo = (a - a_hi.astype(F32)).astype(BF16)
    b_hi = b.astype(BF16)
    b_lo = (b - b_hi.astype(F32)).astype(BF16)
    return (jnp.dot(a_hi, b_hi, preferred_element_type=F32) + jnp.dot(a_lo, b_hi, preferred_element_type=F32)
            + jnp.dot(a_hi, b_lo, preferred_element_type=F32))


def _dot_state(m, h):
    return _dot(m, h)


def _sigmoid(x):
    return 1.0 / (1.0 + jnp.exp(-x))


def _const_spec(shape):
    nd = len(shape)
    return pl.BlockSpec(shape, lambda *_: (0,) * nd)


def _row_spec(tm, width):
    return pl.BlockSpec((tm, width), lambda i: (i, 0))


def _mixer_in_kernel(*refs, tiles_per_seq):
    (x_ref, g_ref, wa_ref, wb_ref, wc_ref, mu_ref, cos_ref, sin_ref) = refs[:8]
    mla_consts = refs[8:15]
    gmlp_consts = refs[15:20]
    za_ref, q_o, k_o, v_o, yc_ref, carry_ref = refs[20:]
    x = x_ref[...]
    ms = jnp.mean(x * x, axis=-1, keepdims=True)
    h = (x * lax.rsqrt(ms + EPS) * g_ref[...]).astype(BF16)
    zb = jnp.dot(h, wb_ref[...], preferred_element_type=F32)
    zc = jnp.dot(h, wc_ref[...], preferred_element_type=F32)
    za = jnp.dot(h, wa_ref[...], preferred_element_type=F32)
    _mla_pre_compute(zb, cos_ref[...], sin_ref[...], *mla_consts, q_o, k_o, v_o)
    _gmlp_compute(zc, *gmlp_consts, yc_ref)
    tm = za.shape[0]
    first = jnp.where(pl.program_id(0) % tiles_per_seq == 0, 0.0, carry_ref[0:1, :])
    row = lax.broadcasted_iota(jnp.int32, za.shape, 0)
    prev = jnp.where(row == 0, first, pltpu.roll(za, 1, 0))
    carry_ref[0:1, :] = za[tm - 1:tm, :]
    za_ref[...] = za + (prev - za) * mu_ref[...]


def _mixer_in(x, g, w_in, mu, seq_len, cos_t, sin_t, mla_consts, gmlp_consts, tm=256):
    n = x.shape[0]
    wa = w_in[:, :A_COLS].astype(BF16)
    wb = jnp.pad(w_in[:, A_COLS:A_COLS + B_COLS], ((0, 0), (0, 4 * LANES - B_COLS))).astype(BF16)
    wc = w_in[:, A_COLS + B_COLS:].astype(BF16)
    consts = [g.reshape(1, -1), wa, wb, wc, mu.reshape(1, -1)]
    width = B_HEADS * B_HEAD_PAD
    return pl.pallas_call(
        functools.partial(_mixer_in_kernel, tiles_per_seq=seq_len // tm),
        out_shape=(jax.ShapeDtypeStruct((n, A_COLS), F32),
                   jax.ShapeDtypeStruct((n, width), BF16), jax.ShapeDtypeStruct((n, width), BF16),
                   jax.ShapeDtypeStruct((n, B_WIDTH), BF16), jax.ShapeDtypeStruct((n, C_WIDTH), F32)),
        grid=(n // tm,),
        in_specs=[_row_spec(tm, D_MODEL)] + [_const_spec(c.shape) for c in consts]
                 + [_row_spec(tm, LANES), _row_spec(tm, LANES)]
                 + [_const_spec(c.shape) for c in mla_consts + gmlp_consts],
        out_specs=(_row_spec(tm, A_COLS), _row_spec(tm, width), _row_spec(tm, width),
                   _row_spec(tm, B_WIDTH), _row_spec(tm, C_WIDTH)),
        scratch_shapes=[pltpu.VMEM((SUBLANES, A_COLS), F32)],
        compiler_params=_cparams(("arbitrary",)),
        name="mixer_in",
    )(x, *consts, cos_t, sin_t, *mla_consts, *gmlp_consts)


def _rwkv_pre_kernel(*refs, has_vres):
    if has_vres:
        (za_ref, w0_ref, wup_ref, a0_ref, aup_ref, gup_ref, kk_ref, ka_ref, rk_ref, bd_ref,
         vf_ref, v0_ref, vdn_ref, vup_ref,
         r_o, k_o, v_o, kk_o, a_o, lw_o, g_o, bonus_o) = refs
    else:
        (za_ref, w0_ref, wup_ref, a0_ref, aup_ref, gup_ref, kk_ref, ka_ref, rk_ref, bd_ref,
         r_o, k_o, v_o, kk_o, a_o, lw_o, g_o, bonus_o) = refs
    r = za_ref[:, 0:A_WIDTH]
    k = za_ref[:, A_WIDTH:2 * A_WIDTH]
    v = za_ref[:, 2 * A_WIDTH:3 * A_WIDTH]
    lora = za_ref[:, 3 * A_WIDTH:A_COLS]
    w = w0_ref[...] + _dot3(jnp.tanh(lora), wup_ref[...])
    w_log = -(jnp.maximum(-w, 0.0) + jnp.log(1.0 + jnp.exp(-jnp.abs(w)))) - 0.5
    lw_o[...] = -jnp.exp(w_log)
    a = _sigmoid(a0_ref[...] + _dot3(lora, aup_ref[...]))
    g_o[...] = _dot(_sigmoid(lora), gup_ref[...])
    if has_vres:
        mix = _sigmoid(v0_ref[...] + _dot3(_dot3(v, vdn_ref[...]), vup_ref[...]))
        v = v + (vf_ref[...] - v) * mix
    kk = k * kk_ref[...]
    ss = _dot_split(kk * kk, bd_ref[...])
    kk = kk / jnp.maximum(jnp.sqrt(ss), 1e-12)
    k = k * (1.0 + (a - 1.0) * ka_ref[...])
    r_o[...] = r
    k_o[...] = k
    v_o[...] = v
    kk_o[...] = kk
    a_o[...] = a
    bonus_o[...] = _dot_split(r * k * rk_ref[...], bd_ref[...]) * v


def _head_block_diag(width, head_dim):
    idx = jnp.arange(width) // head_dim
    return (idx[:, None] == idx[None, :]).astype(F32)


def _pad_rows(w, row_off, rows=LANES):
    return jnp.pad(w, ((row_off, rows - row_off - w.shape[0]), (0, 0)))


def _rwkv_pre(za, w0, w_up, a0, a_up, g_up, k_k, k_a, r_k, v_first, v0, v_down, v_up, tm=256):
    n = za.shape[0]
    has_vres = v_first is not None
    row = lambda t: t.reshape(1, -1)
    args = [za, row(w0), _pad_rows(w_up, 0), row(a0), _pad_rows(a_up, A_W_LORA),
            _pad_rows(g_up, A_W_LORA + A_A_LORA), row(k_k), row(k_a), row(r_k),
            _head_block_diag(A_WIDTH, A_HEAD_DIM)]
    specs = [_row_spec(tm, A_COLS)] + [_const_spec(a.shape) for a in args[1:]]
    if has_vres:
        vdn = jnp.pad(v_down, ((0, 0), (0, LANES - v_down.shape[1])))
        extra = [v_first, row(v0), vdn, _pad_rows(v_up, 0)]
        args += extra
        specs += [_row_spec(tm, A_WIDTH)] + [_const_spec(a.shape) for a in extra[1:]]
    out = jax.ShapeDtypeStruct((n, A_WIDTH), F32)
    return pl.pallas_call(
        functools.partial(_rwkv_pre_kernel, has_vres=has_vres),
        out_shape=(out,) * 8,
        grid=(n // tm,),
        in_specs=specs,
        out_specs=(_row_spec(tm, A_WIDTH),) * 8,
        compiler_params=_cparams(("parallel",)),
        name="rwkv_pre",
    )(*args)


def _stack_heads(x, lane_lo):
    return jnp.concatenate([jnp.where(lane_lo, x, 0.0), jnp.where(lane_lo, 0.0, x)], axis=0)


def _rwkv_scan_kernel(r_ref, k_ref, v_ref, kk_ref, a_ref, lw_ref, g_ref, bonus_ref, lng_ref, lnb_ref,
                      bd_ref, y_ref, h_ref):
    n_batch, rows_per_step, _ = r_ref.shape
    L = SCAN_CHUNK
    n_chunks = rows_per_step // L
    S2 = 2 * L

    @pl.when(pl.program_id(0) == 0)
    def _():
        h_ref[...] = jnp.zeros_like(h_ref)

    ti = lax.broadcasted_iota(jnp.int32, (L, L), 0)
    si = lax.broadcasted_iota(jnp.int32, (L, L), 1)
    tril_incl = (ti >= si).astype(F32)
    row_w = lax.broadcasted_iota(jnp.int32, (L, S2), 0)
    col_w = lax.broadcasted_iota(jnp.int32, (L, S2), 1) % L
    strict = row_w > col_w
    incl = row_w >= col_w
    eye_w = row_w == col_w
    eye = (lax.broadcasted_iota(jnp.int32, (PAIR, PAIR), 0)
           == lax.broadcasted_iota(jnp.int32, (PAIR, PAIR), 1))
    lane_lo = lax.broadcasted_iota(jnp.int32, (1, PAIR), 1) < A_HEAD_DIM
    zeros_s = jnp.zeros((S2, PAIR), F32)
    mid = L // 2 - 1

    chains = []
    for b, ck in [(b, ck) for b in range(n_batch) for ck in range(n_chunks)]:
        rows = slice(ck * L, (ck + 1) * L)
        lw = lw_ref[b, rows, :]
        cum = _dot_split(tril_incl, lw, split_lhs=False)
        c_mid = cum[mid:mid + 1, :]
        c_end = cum[L - 1:L, :]
        kk = kk_ref[b, rows, :]
        kb = kk * a_ref[b, rows, :]
        kmod = k_ref[b, rows, :]
        e_dn = jnp.exp(c_mid - cum)
        e_hat = jnp.exp(c_end - cum)
        at_all = -kk * jnp.exp(cum - lw - c_mid)
        bt_all = kb * e_dn
        kt_all = kmod * e_dn
        rt_all = r_ref[b, rows, :] * jnp.exp(cum - c_mid)
        bh_all = kb * e_hat
        kh_all = kmod * e_hat
        p_end = jnp.exp(c_end)
        p_mid = jnp.exp(c_mid)
        at_true = at_all * p_mid
        rt_true = rt_all * p_mid
        v_all = v_ref[b, rows, :]
        for p in range(N_PAIRS):
            sl = slice(p * PAIR, (p + 1) * PAIR)
            stack = lambda t: _stack_heads(t[:, sl], lane_lo)
            chains.append(dict(
                b=b, p=p, ck=ck, ar=jnp.concatenate([at_all[:, sl], rt_all[:, sl]], axis=0),
                bk_s=jnp.concatenate([stack(bt_all), stack(kt_all)], axis=0),
                bkh_t=jnp.concatenate([stack(bh_all).T, stack(kh_all).T], axis=1),
                v_s=stack(v_all), a_in_s=stack(at_true), r_in=rt_true[:, sl], p_end=p_end[:, sl]))
    stack = lambda t: _stack_heads(t, lane_lo)
    for c in chains:
        gram = _dot_nt(c["ar"], c["bk_s"])
        a_ab = jnp.where(strict, gram[:L, :S2], 0.0)
        c["a_ak"] = jnp.where(strict, gram[:L, S2:], 0.0)
        c["a_r"] = jnp.where(jnp.concatenate([incl, incl], axis=1), gram[L:], 0.0)
        c["x"] = a_ab
        c["t"] = jnp.where(eye_w, 1.0, a_ab)
    n_doublings = int(math.log2(L)) - 1
    for c in chains:
        c["x"] = _dot(c["x"], stack(c["x"]))
    for it in range(n_doublings):
        for c in chains:
            if it + 1 < n_doublings:
                both = _dot(c["x"], jnp.concatenate([stack(c["t"]), stack(c["x"])], axis=1))
                c["t"] = c["t"] + both[:, :S2]
                c["x"] = both[:, S2:]
            else:
                c["t"] = c["t"] + _dot(c["x"], stack(c["t"]))
    for c in chains:
        c["av"] = _dot(c["a_ak"], c["v_s"])
    for c in chains:
        tw = _dot(c["t"], jnp.concatenate([c["a_in_s"], stack(c["av"])], axis=1))
        c["rhs"] = jnp.concatenate(
            [jnp.concatenate([stack(tw[:, :PAIR]), stack(tw[:, PAIR:])], axis=1),
             jnp.concatenate([zeros_s, c["v_s"]], axis=1)], axis=0)
    for c in chains:
        out = _dot(jnp.concatenate([c["a_r"], c["bkh_t"]], axis=0), c["rhs"])
        c["q"] = c["r_in"] + out[:L, :PAIR]
        c["y0"] = out[:L, PAIR:]
        c["m"] = jnp.where(eye, c["p_end"], 0.0) + out[L:, :PAIR]
        c["c"] = out[L:, PAIR:]
    state = {(b, p): h_ref[b, p] for b in range(n_batch) for p in range(N_PAIRS)}
    for ck in range(n_chunks):
        for c in [c for c in chains if c["ck"] == ck]:
            out = _dot_state(jnp.concatenate([c["q"], c["m"]], axis=0), state[c["b"], c["p"]])
            c["y"] = out[:L] + c["y0"]
            state[c["b"], c["p"]] = out[L:] + c["c"]
    for (b, p), h in state.items():
        h_ref[b, p] = h
    for b in range(n_batch):
        y = jnp.concatenate(
            [jnp.concatenate([c["y"] for c in chains if c["b"] == b and c["ck"] == ck], axis=1)
             for ck in range(n_chunks)], axis=0)
        mu = _dot_split(y, bd_ref[...]) * (1.0 / A_HEAD_DIM)
        d = y - mu
        var = _dot_split(d * d, bd_ref[...]) * (1.0 / A_HEAD_DIM)
        yn = d * lax.rsqrt(var + RWKV_GN_EPS) * lng_ref[...] + lnb_ref[...]
        y_ref[b] = (yn + bonus_ref[b]) * g_ref[b]


def _rwkv_scan(r, k, v, kk, a, lw, g, bonus, ln_g, ln_b, n_batch, seq_len, chunks_per_step=4):
    L = SCAN_CHUNK * chunks_per_step
    shp = (n_batch, seq_len, A_WIDTH)
    seq_spec = pl.BlockSpec((n_batch, L, A_WIDTH), lambda i: (0, i, 0))
    args = [t.reshape(shp) for t in (r, k, v, kk, a, lw, g, bonus)]
    consts = [ln_g.reshape(1, -1), ln_b.reshape(1, -1), _head_block_diag(A_WIDTH, A_HEAD_DIM)]
    y = pl.pallas_call(
        _rwkv_scan_kernel,
        out_shape=jax.ShapeDtypeStruct(shp, F32),
        grid=(seq_len // L,),
        in_specs=[seq_spec] * 8 + [_const_spec(c.shape) for c in consts],
        out_specs=seq_spec,
        scratch_shapes=[pltpu.VMEM((n_batch, N_PAIRS, PAIR, PAIR), F32)],
        compiler_params=_cparams(("arbitrary",)),
        name="rwkv_scan",
    )(*args, *consts)
    return y.reshape(n_batch * seq_len, A_WIDTH)


def _exchange_rope_halves(t):
    half = B_ROPE // 2
    nope = jnp.zeros_like(t[..., :B_NOPE])
    return jnp.concatenate([nope, t[..., B_NOPE + half:], t[..., B_NOPE:B_NOPE + half]], axis=-1)


def _mla_pre_compute(zb, cos_t, sin_t, gq_ref, wuq_ref, gkv_ref, wk_ref, wv_ref, gqh_ref, gkh_ref,
                     q_o, k_o, v_o):
    cq = zb[:, 0:B_Q_LORA]
    ckv = zb[:, B_Q_LORA:B_Q_LORA + B_KV_LORA]
    kr_raw = zb[:, B_Q_LORA + B_KV_LORA:]
    half = B_ROPE // 2
    lane = lax.broadcasted_iota(jnp.int32, kr_raw.shape, 1)
    kr = pltpu.roll(kr_raw, B_NOPE, 1)
    kr_sw = jnp.where(lane < B_NOPE + half, pltpu.roll(kr_raw, B_NOPE - half, 1),
                      pltpu.roll(kr_raw, B_NOPE + half, 1))
    qn = cq * lax.rsqrt(jnp.mean(cq * cq, axis=-1, keepdims=True) + EPS) * gq_ref[...]
    kvn = ckv * lax.rsqrt(jnp.mean(ckv * ckv, axis=-1, keepdims=True) + EPS) * gkv_ref[...]
    width = B_HEADS * B_HEAD_PAD
    q2 = _dot(qn, wuq_ref[...])
    k = _dot(kvn, wk_ref[...])
    v_o[...] = _dot(kvn, wv_ref[...]).astype(BF16)
    q_scale = LOG2_E / math.sqrt(B_QK)
    gq_cos = cos_t * (gqh_ref[0:1, :] * q_scale)
    gq_sin = sin_t * (gqh_ref[1:2, :] * q_scale)
    gk_cos = cos_t * gkh_ref[0:1, :]
    gk_sin = kr_sw * sin_t * gkh_ref[1:2, :]
    for h in range(B_HEADS):
        sl = slice(h * B_HEAD_PAD, (h + 1) * B_HEAD_PAD)
        qh = q2[:, sl]
        q_sw = q2[:, width + h * B_HEAD_PAD:width + (h + 1) * B_HEAD_PAD]
        kh = k[:, sl] + kr
        rs_q = lax.rsqrt(jnp.sum(qh * qh, axis=-1, keepdims=True) * (1.0 / B_QK) + EPS)
        rs_k = lax.rsqrt(jnp.sum(kh * kh, axis=-1, keepdims=True) * (1.0 / B_QK) + EPS)
        q_o[:, sl] = ((qh * gq_cos + q_sw * gq_sin) * rs_q).astype(BF16)
        k_o[:, sl] = ((kh * gk_cos + gk_sin) * rs_k).astype(BF16)


def _rope_tables(positions):
    half = B_ROPE // 2
    inv_freq = ROPE_THETA ** (-jnp.arange(0, B_ROPE, 2, dtype=F32) / B_ROPE)
    ang = positions.astype(F32).reshape(-1, 1) * inv_freq
    cos, sin = jnp.cos(ang), jnp.sin(ang)
    n = ang.shape[0]
    ones = jnp.ones((n, B_NOPE), F32)
    zeros = jnp.zeros((n, B_NOPE), F32)
    tail = jnp.zeros((n, B_HEAD_PAD - B_QK), F32)
    cos_t = jnp.concatenate([ones, cos, cos, tail], axis=1)
    sin_t = jnp.concatenate([zeros, -sin, sin, tail], axis=1)
    assert cos_t.shape[1] == B_HEAD_PAD and half * 2 == B_ROPE
    return cos_t, sin_t


def _pad_heads(w, heads, dim):
    kdim = w.shape[0]
    w = w.reshape(kdim, heads, dim)
    return jnp.pad(w, ((0, 0), (0, 0), (0, B_HEAD_PAD - dim))).reshape(kdim, heads * B_HEAD_PAD)


def _mla_pre_consts(q_norm_g, w_uq, kv_norm_g, w_ukv, q_head_g, k_head_g):
    w_uq_x = _exchange_rope_halves(w_uq.reshape(B_Q_LORA, B_HEADS, B_QK)).reshape(B_Q_LORA, -1)
    wuq = jnp.concatenate([_pad_heads(w_uq, B_HEADS, B_QK), _pad_heads(w_uq_x, B_HEADS, B_QK)],
                          axis=1).astype(BF16)
    w_ukv = w_ukv.reshape(B_KV_LORA, B_HEADS, B_NOPE + B_V)
    wk = _pad_heads(w_ukv[:, :, :B_NOPE].reshape(B_KV_LORA, -1), B_HEADS, B_NOPE).astype(BF16)
    wv = w_ukv[:, :, B_NOPE:].reshape(B_KV_LORA, B_WIDTH).astype(BF16)
    pad_g = lambda gg: jnp.pad(jnp.stack([gg, _exchange_rope_halves(gg)]), ((0, 0), (0, B_HEAD_PAD - B_QK)))
    return [q_norm_g.reshape(1, -1), wuq, kv_norm_g.reshape(1, -1), wk, wv,
            pad_g(q_head_g), pad_g(k_head_g)]


def _attn_kernel(qt_ref, kt_ref, q_ref, k_ref, v_ref, o_ref, m_ref, acc_ref):
    t = pl.program_id(2)
    i = qt_ref[t]
    j = kt_ref[t]
    tk = k_ref.shape[0]

    @pl.when(j == 0)
    def _():
        m_ref[...] = jnp.full_like(m_ref, NEG_BIG)
        acc_ref[...] = jnp.zeros_like(acc_ref)

    def update(rows, cols, masked):
        v = v_ref[cols, :]
        n_cols = v.shape[0]
        lane_v = lax.broadcasted_iota(jnp.int32, v.shape, 1)
        one = jnp.ones_like(v)
        for h in range(2):
            sl = slice(h * B_HEAD_PAD, (h + 1) * B_HEAD_PAD)
            s = lax.dot_general(q_ref[rows, sl], k_ref[cols, sl], (((1,), (1,)), ((), ())),
                                preferred_element_type=F32)
            if masked:
                qc = lax.broadcasted_iota(jnp.int32, s.shape, 0) // CHUNK
                kc = lax.broadcasted_iota(jnp.int32, s.shape, 1) // CHUNK
                s = jnp.where(kc <= qc, s, NEG_BIG)
            m_prev = m_ref[h, rows, :]
            m_new = jnp.maximum(m_prev, jnp.max(s, axis=-1, keepdims=True))
            alpha = jnp.exp2(m_prev - m_new)
            p = jnp.exp2((s - jnp.concatenate([m_new] * (n_cols // LANES), axis=1)).astype(BF16))
            vh = jnp.where((lane_v < B_V) == (h == 0), v, one)
            acc_ref[h, rows, :] = alpha * acc_ref[h, rows, :] + jnp.dot(p, vh, preferred_element_type=F32)
            m_ref[h, rows, :] = m_new

    @pl.when(j < i)
    def _():
        update(slice(0, tk), slice(0, tk), False)

    @pl.when(j == i)
    def _():
        update(slice(0, tk), slice(0, tk), True)
        lane = lax.broadcasted_iota(jnp.int32, o_ref.shape, 1)
        a0 = acc_ref[0]
        a1 = acc_ref[1]
        o_ref[...] = jnp.where(lane < B_V, a0 / pltpu.roll(a0, B_V, 1), a1 / pltpu.roll(a1, B_V, 1))


def _attention(q, k, v, n_batch, seq_len, tile=1024):
    tile = min(tile, seq_len)
    nt = seq_len // tile
    width = B_HEADS * B_HEAD_PAD
    q = q.reshape(n_batch, seq_len, width)
    k = k.reshape(n_batch, seq_len, width)
    v = v.reshape(n_batch, seq_len, B_WIDTH)
    q_tiles = jnp.asarray([i for i in range(nt) for _ in range(i + 1)], jnp.int32)
    k_tiles = jnp.asarray([j for i in range(nt) for j in range(i + 1)], jnp.int32)
    o = pl.pallas_call(
        _attn_kernel,
        out_shape=jax.ShapeDtypeStruct((n_batch, seq_len, B_WIDTH), F32),
        grid_spec=pltpu.PrefetchScalarGridSpec(
            num_scalar_prefetch=2,
            grid=(n_batch, B_HEADS // 2, nt * (nt + 1) // 2),
            in_specs=[pl.BlockSpec((None, tile, 2 * B_HEAD_PAD), lambda b, p, t, qt, kt: (b, qt[t], p)),
                      pl.BlockSpec((None, tile, 2 * B_HEAD_PAD), lambda b, p, t, qt, kt: (b, kt[t], p)),
                      pl.BlockSpec((None, tile, 2 * B_V), lambda b, p, t, qt, kt: (b, kt[t], p))],
            out_specs=pl.BlockSpec((None, tile, 2 * B_V), lambda b, p, t, qt, kt: (b, qt[t], p)),
            scratch_shapes=[pltpu.VMEM((2, tile, LANES), F32), pltpu.VMEM((2, tile, 2 * B_V), F32)]),
        compiler_params=_cparams(("parallel", "parallel", "arbitrary")),
        name="mla_attention",
    )(q_tiles, k_tiles, q, k, v)
    return o.reshape(n_batch * seq_len, B_WIDTH)


def _gmlp_compute(z, lng_ref, lnb_ref, ws_ref, bias_ref, og_ref, y_ref):
    z = 0.5 * z * (1.0 + lax.erf(z * (1.0 / math.sqrt(2.0))))
    u = z[:, :C_WIDTH]
    v = z[:, C_WIDTH:]
    mu = jnp.mean(v, axis=-1, keepdims=True)
    d = v - mu
    var = jnp.mean(d * d, axis=-1, keepdims=True)
    v = d * lax.rsqrt(var + EPS) * lng_ref[...] + lnb_ref[...]
    ti = lax.broadcasted_iota(jnp.int32, (C_BLOCK, C_BLOCK), 0)
    si = lax.broadcasted_iota(jnp.int32, (C_BLOCK, C_BLOCK), 1)
    group = lax.broadcasted_iota(jnp.int32, (1, C_WIDTH), 1) // C_GROUP_DIM
    ws = [jnp.where(ti >= si, ws_ref[gi], 0.0).astype(BF16) for gi in range(C_GROUPS)]
    blocks = []
    for blk in range(v.shape[0] // C_BLOCK):
        vb = v[blk * C_BLOCK:(blk + 1) * C_BLOCK]
        sv = bias_ref[...]
        for gi in range(C_GROUPS):
            sv = sv + jnp.dot(ws[gi], jnp.where(group == gi, vb, 0.0).astype(BF16),
                              preferred_element_type=F32)
        blocks.append(sv)
    sv = jnp.concatenate(blocks, axis=0) if len(blocks) > 1 else blocks[0]
    y = u * sv
    y_ref[...] = y * lax.rsqrt(jnp.mean(y * y, axis=-1, keepdims=True) + EPS) * og_ref[...]


def _gmlp_consts(ln_g, ln_b, w_s, b_s, out_g):
    bias = jnp.repeat(jnp.transpose(b_s), C_GROUP_DIM, axis=1)
    return [ln_g.reshape(1, -1), ln_b.reshape(1, -1), w_s, bias, out_g.reshape(1, -1)]


def _out_proj_kernel(x_ref, ya_ref, o_ref, yc_ref, bg_ref, wa_ref, wb_ref, wc_ref, fg_ref,
                     xo_ref, h_ref):
    o = o_ref[...]
    yb = o * lax.rsqrt(jnp.mean(o * o, axis=-1, keepdims=True) + EPS) * bg_ref[...]
    x = (x_ref[...] + _dot(ya_ref[...], wa_ref[...]) + _dot(yb, wb_ref[...])
         + _dot(yc_ref[...], wc_ref[...]))
    xo_ref[...] = x
    h_ref[...] = (x * lax.rsqrt(jnp.mean(x * x, axis=-1, keepdims=True) + EPS) * fg_ref[...]).astype(BF16)


def _out_proj(x, ya, o, yc, b_out_g, w_out, ffn_g, tm=512):
    n = x.shape[0]
    wa = w_out[:A_WIDTH].astype(BF16)
    wb = w_out[A_WIDTH:A_WIDTH + B_WIDTH].astype(BF16)
    wc = w_out[A_WIDTH + B_WIDTH:].astype(BF16)
    consts = [b_out_g.reshape(1, -1), wa, wb, wc, ffn_g.reshape(1, -1)]
    return pl.pallas_call(
        _out_proj_kernel,
        out_shape=(jax.ShapeDtypeStruct((n, D_MODEL), F32), jax.ShapeDtypeStruct((n, D_MODEL), BF16)),
        grid=(n // tm,),
        in_specs=[_row_spec(tm, D_MODEL), _row_spec(tm, A_WIDTH), _row_spec(tm, B_WIDTH),
                  _row_spec(tm, C_WIDTH)] + [_const_spec(c.shape) for c in consts],
        out_specs=(_row_spec(tm, D_MODEL), _row_spec(tm, D_MODEL)),
        compiler_params=_cparams(("parallel",)),
        name="out_proj",
    )(x, ya, o, yc, *consts)


def _load_weight_bf16(w_hbm, dst_ref, stage_ref, sem):
    rows = dst_ref.shape[0]
    step = stage_ref.shape[1]
    starts = list(range(0, rows, step))

    def copy(i):
        n = min(step, rows - starts[i])
        return pltpu.make_async_copy(w_hbm.at[pl.ds(starts[i], n)], stage_ref.at[i % 2, pl.ds(0, n)],
                                     sem.at[i % 2])

    copy(0).start()
    for i, r0 in enumerate(starts):
        if i + 1 < len(starts):
            copy(i + 1).start()
        copy(i).wait()
        n = min(step, rows - r0)
        dst_ref[r0:r0 + n, :] = stage_ref[i % 2, 0:n, :].astype(BF16)


def _ff_chunks(d_ff, chunk):
    return [(c0, min(chunk, d_ff - c0)) for c0 in range(0, d_ff, chunk)]


def _swiglu_resident(h, acc, wg_ref, wu_ref, wd_ref, chunk, between=None):
    for ci, (c0, cn) in enumerate(_ff_chunks(wg_ref.shape[1], chunk)):
        if between is not None:
            between(ci)
        gate = jnp.dot(h, wg_ref[:, c0:c0 + cn], preferred_element_type=F32)
        up = jnp.dot(h, wu_ref[:, c0:c0 + cn], preferred_element_type=F32)
        act = (gate * _sigmoid(gate) * up).astype(BF16)
        acc = acc + jnp.dot(act, wd_ref[c0:c0 + cn, :], preferred_element_type=F32)
    return acc


def _ffn_kernel(h_ref, x_ref, wg_hbm, wu_hbm, wd_hbm, o_ref, wg_ref, wu_ref, wd_ref, stage_in, stage_out,
                sem, *, ff_chunk):
    @pl.when(pl.program_id(0) == 0)
    def _():
        _load_weight_bf16(wg_hbm, wg_ref, stage_in, sem)
        _load_weight_bf16(wu_hbm, wu_ref, stage_in, sem)
        _load_weight_bf16(wd_hbm, wd_ref, stage_out, sem)

    o_ref[...] = _swiglu_resident(h_ref[...], x_ref[...], wg_ref, wu_ref, wd_ref, ff_chunk)


STAGE_ROWS_IN = 128
STAGE_ROWS_OUT = 512


def _weight_scratch(d_ff):
    return [pltpu.VMEM((D_MODEL, d_ff), BF16), pltpu.VMEM((D_MODEL, d_ff), BF16),
            pltpu.VMEM((d_ff, D_MODEL), BF16),
            pltpu.VMEM((2, STAGE_ROWS_IN, d_ff), F32), pltpu.VMEM((2, STAGE_ROWS_OUT, D_MODEL), F32)]


def _dense_ffn(x, h, wg, wu, wd, tm=512, ff_chunk=512):
    n = x.shape[0]
    tm = min(tm, n)
    d_ff = wg.shape[1]
    any_spec = pl.BlockSpec(memory_space=pl.ANY)
    return pl.pallas_call(
        functools.partial(_ffn_kernel, ff_chunk=ff_chunk),
        out_shape=jax.ShapeDtypeStruct((n, D_MODEL), F32),
        grid=(n // tm,),
        in_specs=[_row_spec(tm, D_MODEL), _row_spec(tm, D_MODEL), any_spec, any_spec, any_spec],
        out_specs=_row_spec(tm, D_MODEL),
        scratch_shapes=_weight_scratch(d_ff) + [pltpu.SemaphoreType.DMA((2,))],
        compiler_params=_cparams(("arbitrary",)),
        name="dense_ffn",
    )(h, x, wg, wu, wd)


def _router_kernel(x_ref, fg_ref, wr_ref, route_ref):
    x = x_ref[...]
    h = x * lax.rsqrt(jnp.mean(x * x, axis=-1, keepdims=True) + EPS) * fg_ref[...]
    logits = _dot3(h, wr_ref[...])
    lane = lax.broadcasted_iota(jnp.int32, logits.shape, 1)
    logits = jnp.where(lane < N_EXPERTS, logits, -jnp.inf)
    v1 = jnp.max(logits, axis=-1, keepdims=True)
    i1 = jnp.min(jnp.where(logits == v1, lane, LANES), axis=-1, keepdims=True)
    rest = jnp.where(lane == i1, -jnp.inf, logits)
    v2 = jnp.max(rest, axis=-1, keepdims=True)
    i2 = jnp.min(jnp.where(rest == v2, lane, LANES), axis=-1, keepdims=True)
    e2 = jnp.exp(v2 - v1)
    g1 = 1.0 / (1.0 + e2)
    g2 = e2 / (1.0 + e2)
    route_ref[...] = jnp.where(lane == 0, i1.astype(F32),
                               jnp.where(lane == 1, i2.astype(F32),
                                         jnp.where(lane == 2, g1, jnp.where(lane == 3, g2, 0.0))))


def _router(x, ffn_g, w_router, tm=512):
    n = x.shape[0]
    tm = min(tm, n)
    wr = jnp.pad(w_router, ((0, 0), (0, LANES - N_EXPERTS)))
    return pl.pallas_call(
        _router_kernel,
        out_shape=jax.ShapeDtypeStruct((n, LANES), F32),
        grid=(n // tm,),
        in_specs=[_row_spec(tm, D_MODEL), _const_spec((1, D_MODEL)), _const_spec(wr.shape)],
        out_specs=_row_spec(tm, LANES),
        compiler_params=_cparams(("parallel",)),
        name="moe_router",
    )(x, ffn_g.reshape(1, -1), wr)


def _moe_plan(route, tm):
    n = route.shape[0]
    experts = jnp.concatenate([route[:, 0], route[:, 1]]).astype(jnp.int32)
    onehot = (experts[:, None] == jnp.arange(N_EXPERTS, dtype=jnp.int32)[None, :]).astype(jnp.int32)
    csum = jnp.cumsum(onehot, axis=0)
    rank = jnp.sum(csum * onehot, axis=1) - 1
    counts = csum[-1]
    padded = (counts + tm - 1) // tm * tm
    ends = jnp.cumsum(padded)
    pos = (ends - padded)[experts] + rank
    n_rows = 2 * n + N_EXPERTS * tm
    tokens = jnp.concatenate([jnp.arange(n, dtype=jnp.int32)] * 2)
    src = jnp.zeros((n_rows,), jnp.int32).at[pos].set(tokens)
    tile_start = jnp.arange(n_rows // tm, dtype=jnp.int32) * tm
    tile_expert = jnp.minimum(jnp.sum(tile_start[:, None] >= ends[None, :], axis=1), N_EXPERTS - 1)
    n_used = (ends[-1] // tm).reshape(1)
    return pos.astype(jnp.int32), src, tile_expert.astype(jnp.int32), n_used.astype(jnp.int32)


def _moe_expert_kernel(te_ref, src_ref, nu_ref, x_hbm, fg_ref, wg_hbm, wu_hbm, wd_hbm, y_ref,
                       rows_ref, wg_ref, wu_ref, wd_ref, stage_in, stage_out, row_sem, w_sem, *, ff_chunk):
    t = pl.program_id(0)
    tm = y_ref.shape[0]
    n_used = nu_ref[0]
    slot = t % 2
    e = te_ref[t]

    def row_copy(tile, s, r):
        return pltpu.make_async_copy(x_hbm.at[pl.ds(src_ref[tile * tm + r], 1)],
                                     rows_ref.at[s, pl.ds(r, 1)], row_sem.at[s])

    def wait_rows(s):
        pltpu.make_async_copy(x_hbm.at[pl.ds(0, tm)], rows_ref.at[s], row_sem.at[s]).wait()

    new_expert = (t == 0) | (e != te_ref[jnp.maximum(t - 1, 0)])

    @pl.when(t == 0)
    def _():
        def body(r, c):
            row_copy(0, 0, r).start()
            return c
        lax.fori_loop(0, tm, body, 0, unroll=8)

    @pl.when((t < n_used) & new_expert)
    def _():
        _load_weight_bf16(wg_hbm.at[e], wg_ref, stage_in, w_sem)
        _load_weight_bf16(wu_hbm.at[e], wu_ref, stage_in, w_sem)
        _load_weight_bf16(wd_hbm.at[e], wd_ref, stage_out, w_sem)

    @pl.when(t < n_used)
    def _():
        wait_rows(slot)
        x = rows_ref[slot]
        h = (x * lax.rsqrt(jnp.mean(x * x, axis=-1, keepdims=True) + EPS) * fg_ref[...]).astype(BF16)
        rows_per_chunk = -(-tm // len(_ff_chunks(wg_ref.shape[1], ff_chunk)))

        def prefetch_rows(ci):
            for r in range(ci * rows_per_chunk, min((ci + 1) * rows_per_chunk, tm)):
                row_copy(t + 1, 1 - slot, r).start()

        y_ref[...] = _swiglu_resident(h, jnp.zeros((tm, D_MODEL), F32), wg_ref, wu_ref, wd_ref, ff_chunk,
                                      between=prefetch_rows)

    @pl.when(t == n_used)
    def _():
        wait_rows(slot)

    @pl.when((t == pl.num_programs(0) - 1) & (t < n_used))
    def _():
        wait_rows(1 - slot)

    @pl.when(t >= n_used)
    def _():
        y_ref[...] = jnp.zeros_like(y_ref)


def _moe_experts(x, src, tile_expert, n_used, ffn_g, wg, wu, wd, tm, ff_chunk=512):
    n_rows = src.shape[0]
    src = jnp.pad(src, (0, tm))
    d_ff = wg.shape[2]
    any_spec = pl.BlockSpec(memory_space=pl.ANY)
    return pl.pallas_call(
        functools.partial(_moe_expert_kernel, ff_chunk=min(ff_chunk, d_ff)),
        out_shape=jax.ShapeDtypeStruct((n_rows, D_MODEL), F32),
        grid_spec=pltpu.PrefetchScalarGridSpec(
            num_scalar_prefetch=3,
            grid=(n_rows // tm,),
            in_specs=[any_spec, pl.BlockSpec((1, D_MODEL), lambda t, *_: (0, 0)),
                      any_spec, any_spec, any_spec],
            out_specs=pl.BlockSpec((tm, D_MODEL), lambda t, *_: (t, 0)),
            scratch_shapes=[pltpu.VMEM((2, tm, D_MODEL), F32)] + _weight_scratch(d_ff)
                           + [pltpu.SemaphoreType.DMA((2,)), pltpu.SemaphoreType.DMA((2,))]),
        compiler_params=_cparams(("arbitrary",)),
        name="moe_experts",
    )(tile_expert, src, n_used, x, ffn_g.reshape(1, -1), wg, wu, wd)


def _moe_combine_kernel(pos_ref, x_ref, route_ref, y_hbm, o_ref, buf_ref, sem, *, n_tokens):
    i = pl.program_id(0)
    tm = x_ref.shape[0]
    slot = i % 2

    def start_rows(tile, s):
        def body(r, c):
            tok = tile * tm + r
            for choice in range(2):
                pltpu.make_async_copy(y_hbm.at[pl.ds(pos_ref[choice * n_tokens + tok], 1)],
                                      buf_ref.at[s, choice, pl.ds(r, 1)], sem.at[s]).start()
            return c
        lax.fori_loop(0, tm, body, 0, unroll=4)

    @pl.when(i == 0)
    def _():
        start_rows(0, 0)

    @pl.when(i + 1 < pl.num_programs(0))
    def _():
        start_rows(i + 1, 1 - slot)

    for choice in range(2):
        pltpu.make_async_copy(y_hbm.at[pl.ds(0, tm)], buf_ref.at[slot, choice], sem.at[slot]).wait()
    route = route_ref[...]
    lane = lax.broadcasted_iota(jnp.int32, route.shape, 1)
    g1 = jnp.sum(jnp.where(lane == 2, route, 0.0), axis=-1, keepdims=True)
    g2 = jnp.sum(jnp.where(lane == 3, route, 0.0), axis=-1, keepdims=True)
    o_ref[...] = x_ref[...] + g1 * buf_ref[slot, 0] + g2 * buf_ref[slot, 1]


def _moe_combine(x, route, pos, y, tm=256):
    n = x.shape[0]
    tm = min(tm, n)
    return pl.pallas_call(
        functools.partial(_moe_combine_kernel, n_tokens=n),
        out_shape=jax.ShapeDtypeStruct((n, D_MODEL), F32),
        grid_spec=pltpu.PrefetchScalarGridSpec(
            num_scalar_prefetch=1,
            grid=(n // tm,),
            in_specs=[pl.BlockSpec((tm, D_MODEL), lambda i, *_: (i, 0)),
                      pl.BlockSpec((tm, LANES), lambda i, *_: (i, 0)),
                      pl.BlockSpec(memory_space=pl.ANY)],
            out_specs=pl.BlockSpec((tm, D_MODEL), lambda i, *_: (i, 0)),
            scratch_shapes=[pltpu.VMEM((2, 2, tm, D_MODEL), F32), pltpu.SemaphoreType.DMA((2,))]),
        compiler_params=_cparams(("arbitrary",)),
        name="moe_combine",
    )(pos, x, route, y)


def _moe(x, ffn_g, w_router, wg, wu, wd, tm=512):
    tm = min(tm, x.shape[0])
    route = _router(x, ffn_g, w_router)
    pos, src, tile_expert, n_used = _moe_plan(route, tm)
    y = _moe_experts(x, src, tile_expert, n_used, ffn_g, wg, wu, wd, tm)
    return _moe_combine(x, route, pos, y)


def kernel(x, positions, mix_norm_g, w_in, shift_mu, a_w0, a_w_up, a_a0, a_a_up, a_g_up, a_k_k, a_k_a,
           a_r_k, a_ln_g, a_ln_b, a_v0, a_v_down, a_v_up, b_q_norm_g, b_w_uq, b_kv_norm_g, b_w_ukv,
           b_q_head_g, b_k_head_g, b_out_g, c_ln_g, c_ln_b, c_w_s, c_b_s, c_out_g, w_out, ffn_norm_g,
           dense_w_gate, dense_w_up, dense_w_down, moe_router, moe_w_gate, moe_w_up, moe_w_down):
    n_batch, seq_len, d_model = x.shape
    depth = w_in.shape[0]
    xf = x.reshape(n_batch * seq_len, d_model)
    cos_t, sin_t = _rope_tables(positions)
    v_first = None
    for l in range(depth):
        za, q, kq, vq, yc = _mixer_in(
            xf, mix_norm_g[l], w_in[l], shift_mu[l], seq_len, cos_t, sin_t,
            _mla_pre_consts(b_q_norm_g[l], b_w_uq[l], b_kv_norm_g[l], b_w_ukv[l], b_q_head_g[l], b_k_head_g[l]),
            _gmlp_consts(c_ln_g[l], c_ln_b[l], c_w_s[l], c_b_s[l], c_out_g[l]))
        vres = (a_v0[l - 1], a_v_down[l - 1], a_v_up[l - 1]) if l > 0 else (None, None, None)
        r, k, v, kk, a, lw, g, bonus = _rwkv_pre(
            za, a_w0[l], a_w_up[l], a_a0[l], a_a_up[l], a_g_up[l], a_k_k[l], a_k_a[l],
            a_r_k[l].reshape(-1), v_first if l > 0 else None, *vres)
        if l == 0:
            v_first = v
        ya = _rwkv_scan(r, k, v, kk, a, lw, g, bonus, a_ln_g[l], a_ln_b[l], n_batch, seq_len)
        o = _attention(q, kq, vq, n_batch, seq_len)
        xf, h = _out_proj(xf, ya, o, yc, b_out_g[l], w_out[l], ffn_norm_g[l])
        i = l // 2
        if l % 2 == 0:
            xf = _dense_ffn(xf, h, dense_w_gate[i], dense_w_up[i], dense_w_down[i])
        else:
            xf = _moe(xf, ffn_norm_g[l], moe_router[i], moe_w_gate[i], moe_w_up[i], moe_w_down[i])
    return xf.reshape(n_batch, seq_len, d_model)
```

```python
import functools
import math

import jax
import jax.numpy as jnp
from jax import lax
from jax.experimental import pallas as pl
from jax.experimental.pallas import tpu as pltpu

F32 = jnp.float32
BF16 = jnp.bfloat16

D_MODEL = 1024
CHUNK = 64
A_HEADS = 6
A_HEAD_DIM = 64
A_WIDTH = A_HEADS * A_HEAD_DIM
A_W_LORA = 32
A_A_LORA = 32
A_G_LORA = 64
A_COLS = 3 * A_WIDTH + A_W_LORA + A_A_LORA + A_G_LORA
RWKV_GN_EPS = 64e-5
B_HEADS = 6
B_NOPE = 64
B_ROPE = 32
B_QK = B_NOPE + B_ROPE
B_V = 64
B_WIDTH = B_HEADS * B_V
B_Q_LORA = 256
B_KV_LORA = 128
B_COLS = B_Q_LORA + B_KV_LORA + B_ROPE
ROPE_THETA = 10000.0
C_GROUPS = 4
C_GROUP_DIM = 64
C_WIDTH = C_GROUPS * C_GROUP_DIM
C_BLOCK = 128
C_COLS = 2 * C_WIDTH
N_EXPERTS = 8
EPS = 1e-6

LANES = 128
SUBLANES = 8
VMEM_LIMIT_BYTES = 56 * 1024 * 1024

SCAN_CHUNK = 64
PAIR = 2 * A_HEAD_DIM
N_PAIRS = A_WIDTH // PAIR
B_HEAD_PAD = LANES
NEG_BIG = -1e30
LOG2_E = math.log2(math.e)


def _cparams(sem):
    return pltpu.CompilerParams(dimension_semantics=sem, vmem_limit_bytes=VMEM_LIMIT_BYTES)


def _dot(a, b):
    return jnp.dot(a.astype(BF16), b.astype(BF16), preferred_element_type=F32)


def _dot_nt(a, b):
    return lax.dot_general(a.astype(BF16), b.astype(BF16), (((1,), (1,)), ((), ())),
                           preferred_element_type=F32)


def _dot_split(a, b, split_lhs=True):
    x = a if split_lhs else b
    hi = x.astype(BF16)
    lo = (x - hi.astype(F32)).astype(BF16)
    if split_lhs:
        e = b.astype(BF16)
        return jnp.dot(hi, e, preferred_element_type=F32) + jnp.dot(lo, e, preferred_element_type=F32)
    e = a.astype(BF16)
    return jnp.dot(e, hi, preferred_element_type=F32) + jnp.dot(e, lo, preferred_element_type=F32)


def _dot3(a, b):
    a_hi = a.astype(BF16)
    a_lo = (a - a_hi.astype(F32)).astype(BF16)
    b_hi = b.astype(BF16)
    b_lo = (b - b_hi.astype(F32)).astype(BF16)
    return (jnp.dot(a_hi, b_hi, preferred_element_type=F32) + jnp.dot(a_lo, b_hi, preferred_element_type=F32)
            + jnp.dot(a_hi, b_lo, preferred_element_type=F32))


def _dot_state(m, h):
    return _dot(m, h)


def _sigmoid(x):
    return 1.0 / (1.0 + jnp.exp(-x))


def _const_spec(shape):
    nd = len(shape)
    return pl.BlockSpec(shape, lambda *_: (0,) * nd)


def _row_spec(tm, width):
    return pl.BlockSpec((tm, width), lambda i: (i, 0))


def _mixer_in_kernel(*refs, tiles_per_seq):
    (x_ref, g_ref, wa_ref, wb_ref, wc_ref, mu_ref, cos_ref, sin_ref) = refs[:8]
    mla_consts = refs[8:15]
    gmlp_consts = refs[15:20]
    za_ref, q_o, k_o, v_o, yc_ref, carry_ref = refs[20:]
    x = x_ref[...]
    ms = jnp.mean(x * x, axis=-1, keepdims=True)
    h = (x * lax.rsqrt(ms + EPS) * g_ref[...]).astype(BF16)
    zb = jnp.dot(h, wb_ref[...], preferred_element_type=F32)
    zc = jnp.dot(h, wc_ref[...], preferred_element_type=F32)
    za = jnp.dot(h, wa_ref[...], preferred_element_type=F32)
    _mla_pre_compute(zb, cos_ref[...], sin_ref[...], *mla_consts, q_o, k_o, v_o)
    _gmlp_compute(zc, *gmlp_consts, yc_ref)
    tm = za.shape[0]
    first = jnp.where(pl.program_id(0) % tiles_per_seq == 0, 0.0, carry_ref[0:1, :])
    row = lax.broadcasted_iota(jnp.int32, za.shape, 0)
    prev = jnp.where(row == 0, first, pltpu.roll(za, 1, 0))
    carry_ref[0:1, :] = za[tm - 1:tm, :]
    za_ref[...] = za + (prev - za) * mu_ref[...]


def _mixer_in(x, g, w_in, mu, seq_len, cos_t, sin_t, mla_consts, gmlp_consts, tm=256):
    n = x.shape[0]
    wa = w_in[:, :A_COLS].astype(BF16)
    wb = jnp.pad(w_in[:, A_COLS:A_COLS + B_COLS], ((0, 0), (0, 4 * LANES - B_COLS))).astype(BF16)
    wc = w_in[:, A_COLS + B_COLS:].astype(BF16)
    consts = [g.reshape(1, -1), wa, wb, wc, mu.reshape(1, -1)]
    width = B_HEADS * B_HEAD_PAD
    return pl.pallas_call(
        functools.partial(_mixer_in_kernel, tiles_per_seq=seq_len // tm),
        out_shape=(jax.ShapeDtypeStruct((n, A_COLS), F32),
                   jax.ShapeDtypeStruct((n, width), BF16), jax.ShapeDtypeStruct((n, width), BF16),
                   jax.ShapeDtypeStruct((n, B_WIDTH), BF16), jax.ShapeDtypeStruct((n, C_WIDTH), F32)),
        grid=(n // tm,),
        in_specs=[_row_spec(tm, D_MODEL)] + [_const_spec(c.shape) for c in consts]
                 + [_row_spec(tm, LANES), _row_spec(tm, LANES)]
                 + [_const_spec(c.shape) for c in mla_consts + gmlp_consts],
        out_specs=(_row_spec(tm, A_COLS), _row_spec(tm, width), _row_spec(tm, width),
                   _row_spec(tm, B_WIDTH), _row_spec(tm, C_WIDTH)),
        scratch_shapes=[pltpu.VMEM((SUBLANES, A_COLS), F32)],
        compiler_params=_cparams(("arbitrary",)),
        name="mixer_in",
    )(x, *consts, cos_t, sin_t, *mla_consts, *gmlp_consts)


def _rwkv_pre_compute(za, w0_ref, wup_ref, a0_ref, aup_ref, gup_ref, kk_ref, ka_ref, rk_ref, bd_ref, vres):
    r = za[:, 0:A_WIDTH]
    k = za[:, A_WIDTH:2 * A_WIDTH]
    v = za[:, 2 * A_WIDTH:3 * A_WIDTH]
    lora = za[:, 3 * A_WIDTH:A_COLS]
    w = w0_ref[...] + _dot3(jnp.tanh(lora), wup_ref[...])
    w_log = -(jnp.maximum(-w, 0.0) + jnp.log(1.0 + jnp.exp(-jnp.abs(w)))) - 0.5
    lw = -jnp.exp(w_log)
    a = _sigmoid(a0_ref[...] + _dot3(lora, aup_ref[...]))
    g = _dot(_sigmoid(lora), gup_ref[...])
    if vres is not None:
        vf, v0_ref, vdn_ref, vup_ref = vres
        mix = _sigmoid(v0_ref[...] + _dot3(_dot3(v, vdn_ref[...]), vup_ref[...]))
        v = v + (vf - v) * mix
    kk = k * kk_ref[...]
    ss = _dot_split(kk * kk, bd_ref[...])
    kk = kk / jnp.maximum(jnp.sqrt(ss), 1e-12)
    k = k * (1.0 + (a - 1.0) * ka_ref[...])
    bonus = _dot_split(r * k * rk_ref[...], bd_ref[...]) * v
    return r, k, v, kk, a, lw, g, bonus


def _head_block_diag(width, head_dim):
    idx = jnp.arange(width) // head_dim
    return (idx[:, None] == idx[None, :]).astype(F32)


def _pad_rows(w, row_off, rows=LANES):
    return jnp.pad(w, ((row_off, rows - row_off - w.shape[0]), (0, 0)))


def _rwkv_consts(w0, w_up, a0, a_up, g_up, k_k, k_a, r_k):
    row = lambda t: t.reshape(1, -1)
    return [row(w0), _pad_rows(w_up, 0), row(a0), _pad_rows(a_up, A_W_LORA),
            _pad_rows(g_up, A_W_LORA + A_A_LORA), row(k_k), row(k_a), row(r_k),
            _head_block_diag(A_WIDTH, A_HEAD_DIM)]


def _rwkv_vres_consts(v0, v_down, v_up):
    return [v0.reshape(1, -1), jnp.pad(v_down, ((0, 0), (0, LANES - v_down.shape[1]))), _pad_rows(v_up, 0)]


def _stack_heads(x, lane_lo):
    return jnp.concatenate([jnp.where(lane_lo, x, 0.0), jnp.where(lane_lo, 0.0, x)], axis=0)


def _rwkv_mix_kernel(*refs, has_vres):
    za_ref = refs[0]
    pre_consts = refs[1:10]
    bd_ref = pre_consts[-1]
    if has_vres:
        vf_ref, vres_consts = refs[10], refs[11:14]
        lng_ref, lnb_ref, y_ref, v_out, h_ref = refs[14:]
    else:
        lng_ref, lnb_ref, y_ref, v_out, h_ref = refs[10:]
    n_batch, rows_per_step, _ = za_ref.shape
    L = SCAN_CHUNK
    n_chunks = rows_per_step // L
    S2 = 2 * L

    @pl.when(pl.program_id(0) == 0)
    def _():
        h_ref[...] = jnp.zeros_like(h_ref)

    ti = lax.broadcasted_iota(jnp.int32, (L, L), 0)
    si = lax.broadcasted_iota(jnp.int32, (L, L), 1)
    tril_incl = (ti >= si).astype(F32)
    row_w = lax.broadcasted_iota(jnp.int32, (L, S2), 0)
    col_w = lax.broadcasted_iota(jnp.int32, (L, S2), 1) % L
    strict = row_w > col_w
    incl = row_w >= col_w
    eye_w = row_w == col_w
    eye = (lax.broadcasted_iota(jnp.int32, (PAIR, PAIR), 0)
           == lax.broadcasted_iota(jnp.int32, (PAIR, PAIR), 1))
    lane_lo = lax.broadcasted_iota(jnp.int32, (1, PAIR), 1) < A_HEAD_DIM
    zeros_s = jnp.zeros((S2, PAIR), F32)
    mid = L // 2 - 1

    chains = []
    pre = []
    for b in range(n_batch):
        vres = (vf_ref[b], *vres_consts) if has_vres else None
        pre.append(_rwkv_pre_compute(za_ref[b], *pre_consts, vres))
        v_out[b] = pre[b][2]
    for b, ck in [(b, ck) for b in range(n_batch) for ck in range(n_chunks)]:
        rows = slice(ck * L, (ck + 1) * L)
        r_b, k_b, v_b, kk_b, a_b, lw_b = (t[rows] for t in pre[b][:6])
        lw = lw_b
        cum = _dot_split(tril_incl, lw, split_lhs=False)
        c_mid = cum[mid:mid + 1, :]
        c_end = cum[L - 1:L, :]
        kk = kk_b
        kb = kk * a_b
        kmod = k_b
        e_dn = jnp.exp(c_mid - cum)
        e_hat = jnp.exp(c_end - cum)
        at_all = -kk * jnp.exp(cum - lw - c_mid)
        bt_all = kb * e_dn
        kt_all = kmod * e_dn
        rt_all = r_b * jnp.exp(cum - c_mid)
        bh_all = kb * e_hat
        kh_all = kmod * e_hat
        p_end = jnp.exp(c_end)
        p_mid = jnp.exp(c_mid)
        at_true = at_all * p_mid
        rt_true = rt_all * p_mid
        v_all = v_b
        for p in range(N_PAIRS):
            sl = slice(p * PAIR, (p + 1) * PAIR)
            stack = lambda t: _stack_heads(t[:, sl], lane_lo)
            chains.append(dict(
                b=b, p=p, ck=ck, ar=jnp.concatenate([at_all[:, sl], rt_all[:, sl]], axis=0),
                bk_s=jnp.concatenate([stack(bt_all), stack(kt_all)], axis=0),
                bkh_t=jnp.concatenate([stack(bh_all).T, stack(kh_all).T], axis=1),
                v_s=stack(v_all), a_in_s=stack(at_true), r_in=rt_true[:, sl], p_end=p_end[:, sl]))
    stack = lambda t: _stack_heads(t, lane_lo)
    for c in chains:
        gram = _dot_nt(c["ar"], c["bk_s"])
        a_ab = jnp.where(strict, gram[:L, :S2], 0.0)
        c["a_ak"] = jnp.where(strict, gram[:L, S2:], 0.0)
        c["a_r"] = jnp.where(jnp.concatenate([incl, incl], axis=1), gram[L:], 0.0)
        c["x"] = a_ab
        c["t"] = jnp.where(eye_w, 1.0, a_ab)
    n_doublings = int(math.log2(L)) - 1
    for c in chains:
        c["x"] = _dot(c["x"], stack(c["x"]))
    for it in range(n_doublings):
        for c in chains:
            if it + 1 < n_doublings:
                both = _dot(c["x"], jnp.concatenate([stack(c["t"]), stack(c["x"])], axis=1))
                c["t"] = c["t"] + both[:, :S2]
                c["x"] = both[:, S2:]
            else:
                c["t"] = c["t"] + _dot(c["x"], stack(c["t"]))
    for c in chains:
        c["av"] = _dot(c["a_ak"], c["v_s"])
    for c in chains:
        tw = _dot(c["t"], jnp.concatenate([c["a_in_s"], stack(c["av"])], axis=1))
        c["rhs"] = jnp.concatenate(
            [jnp.concatenate([stack(tw[:, :PAIR]), stack(tw[:, PAIR:])], axis=1),
             jnp.concatenate([zeros_s, c["v_s"]], axis=1)], axis=0)
    for c in chains:
        out = _dot(jnp.concatenate([c["a_r"], c["bkh_t"]], axis=0), c["rhs"])
        c["q"] = c["r_in"] + out[:L, :PAIR]
        c["y0"] = out[:L, PAIR:]
        c["m"] = jnp.where(eye, c["p_end"], 0.0) + out[L:, :PAIR]
        c["c"] = out[L:, PAIR:]
    state = {(b, p): h_ref[b, p] for b in range(n_batch) for p in range(N_PAIRS)}
    for ck in range(n_chunks):
        for c in [c for c in chains if c["ck"] == ck]:
            out = _dot_state(jnp.concatenate([c["q"], c["m"]], axis=0), state[c["b"], c["p"]])
            c["y"] = out[:L] + c["y0"]
            state[c["b"], c["p"]] = out[L:] + c["c"]
    for (b, p), h in state.items():
        h_ref[b, p] = h
    for b in range(n_batch):
        y = jnp.concatenate(
            [jnp.concatenate([c["y"] for c in chains if c["b"] == b and c["ck"] == ck], axis=1)
             for ck in range(n_chunks)], axis=0)
        mu = _dot_split(y, bd_ref[...]) * (1.0 / A_HEAD_DIM)
        d = y - mu
        var = _dot_split(d * d, bd_ref[...]) * (1.0 / A_HEAD_DIM)
        yn = d * lax.rsqrt(var + RWKV_GN_EPS) * lng_ref[...] + lnb_ref[...]
        y_ref[b] = (yn + pre[b][7]) * pre[b][6]


def _rwkv_mix(za, pre_consts, v_first, vres_consts, ln_g, ln_b, n_batch, seq_len, chunks_per_step=4):
    rows = SCAN_CHUNK * chunks_per_step
    has_vres = v_first is not None
    shp = (n_batch, seq_len, A_WIDTH)
    seq_spec = pl.BlockSpec((n_batch, rows, A_WIDTH), lambda i: (0, i, 0))
    args = [za.reshape(n_batch, seq_len, A_COLS)] + list(pre_consts)
    specs = [pl.BlockSpec((n_batch, rows, A_COLS), lambda i: (0, i, 0))] + [_const_spec(c.shape) for c in pre_consts]
    if has_vres:
        args += [v_first.reshape(shp)] + list(vres_consts)
        specs += [seq_spec] + [_const_spec(c.shape) for c in vres_consts]
    norm = [ln_g.reshape(1, -1), ln_b.reshape(1, -1)]
    y, v = pl.pallas_call(
        functools.partial(_rwkv_mix_kernel, has_vres=has_vres),
        out_shape=(jax.ShapeDtypeStruct(shp, F32), jax.ShapeDtypeStruct(shp, F32)),
        grid=(seq_len // rows,),
        in_specs=specs + [_const_spec(c.shape) for c in norm],
        out_specs=(seq_spec, seq_spec),
        scratch_shapes=[pltpu.VMEM((n_batch, N_PAIRS, PAIR, PAIR), F32)],
        compiler_params=_cparams(("arbitrary",)),
        name="rwkv_mix",
    )(*args, *norm)
    return y.reshape(n_batch * seq_len, A_WIDTH), v.reshape(n_batch * seq_len, A_WIDTH)


def _exchange_rope_halves(t):
    half = B_ROPE // 2
    nope = jnp.zeros_like(t[..., :B_NOPE])
    return jnp.concatenate([nope, t[..., B_NOPE + half:], t[..., B_NOPE:B_NOPE + half]], axis=-1)


def _mla_pre_compute(zb, cos_t, sin_t, gq_ref, wuq_ref, gkv_ref, wk_ref, wv_ref, gqh_ref, gkh_ref,
                     q_o, k_o, v_o):
    cq = zb[:, 0:B_Q_LORA]
    ckv = zb[:, B_Q_LORA:B_Q_LORA + B_KV_LORA]
    kr_raw = zb[:, B_Q_LORA + B_KV_LORA:]
    half = B_ROPE // 2
    lane = lax.broadcasted_iota(jnp.int32, kr_raw.shape, 1)
    kr = pltpu.roll(kr_raw, B_NOPE, 1)
    kr_sw = jnp.where(lane < B_NOPE + half, pltpu.roll(kr_raw, B_NOPE - half, 1),
                      pltpu.roll(kr_raw, B_NOPE + half, 1))
    qn = cq * lax.rsqrt(jnp.mean(cq * cq, axis=-1, keepdims=True) + EPS) * gq_ref[...]
    kvn = ckv * lax.rsqrt(jnp.mean(ckv * ckv, axis=-1, keepdims=True) + EPS) * gkv_ref[...]
    width = B_HEADS * B_HEAD_PAD
    q2 = _dot(qn, wuq_ref[...])
    k = _dot(kvn, wk_ref[...])
    v_o[...] = _dot(kvn, wv_ref[...]).astype(BF16)
    q_scale = LOG2_E / math.sqrt(B_QK)
    gq_cos = cos_t * (gqh_ref[0:1, :] * q_scale)
    gq_sin = sin_t * (gqh_ref[1:2, :] * q_scale)
    gk_cos = cos_t * gkh_ref[0:1, :]
    gk_sin = kr_sw * sin_t * gkh_ref[1:2, :]
    for h in range(B_HEADS):
        sl = slice(h * B_HEAD_PAD, (h + 1) * B_HEAD_PAD)
        qh = q2[:, sl]
        q_sw = q2[:, width + h * B_HEAD_PAD:width + (h + 1) * B_HEAD_PAD]
        kh = k[:, sl] + kr
        rs_q = lax.rsqrt(jnp.sum(qh * qh, axis=-1, keepdims=True) * (1.0 / B_QK) + EPS)
        rs_k = lax.rsqrt(jnp.sum(kh * kh, axis=-1, keepdims=True) * (1.0 / B_QK) + EPS)
        q_o[:, sl] = ((qh * gq_cos + q_sw * gq_sin) * rs_q).astype(BF16)
        k_o[:, sl] = ((kh * gk_cos + gk_sin) * rs_k).astype(BF16)


def _rope_tables(positions):
    half = B_ROPE // 2
    inv_freq = ROPE_THETA ** (-jnp.arange(0, B_ROPE, 2, dtype=F32) / B_ROPE)
    ang = positions.astype(F32).reshape(-1, 1) * inv_freq
    cos, sin = jnp.cos(ang), jnp.sin(ang)
    n = ang.shape[0]
    ones = jnp.ones((n, B_NOPE), F32)
    zeros = jnp.zeros((n, B_NOPE), F32)
    tail = jnp.zeros((n, B_HEAD_PAD - B_QK), F32)
    cos_t = jnp.concatenate([ones, cos, cos, tail], axis=1)
    sin_t = jnp.concatenate([zeros, -sin, sin, tail], axis=1)
    assert cos_t.shape[1] == B_HEAD_PAD and half * 2 == B_ROPE
    return cos_t, sin_t


def _pad_heads(w, heads, dim):
    kdim = w.shape[0]
    w = w.reshape(kdim, heads, dim)
    return jnp.pad(w, ((0, 0), (0, 0), (0, B_HEAD_PAD - dim))).reshape(kdim, heads * B_HEAD_PAD)


def _mla_pre_consts(q_norm_g, w_uq, kv_norm_g, w_ukv, q_head_g, k_head_g):
    w_uq_x = _exchange_rope_halves(w_uq.reshape(B_Q_LORA, B_HEADS, B_QK)).reshape(B_Q_LORA, -1)
    wuq = jnp.concatenate([_pad_heads(w_uq, B_HEADS, B_QK), _pad_heads(w_uq_x, B_HEADS, B_QK)],
                          axis=1).astype(BF16)
    w_ukv = w_ukv.reshape(B_KV_LORA, B_HEADS, B_NOPE + B_V)
    wk = _pad_heads(w_ukv[:, :, :B_NOPE].reshape(B_KV_LORA, -1), B_HEADS, B_NOPE).astype(BF16)
    wv = w_ukv[:, :, B_NOPE:].reshape(B_KV_LORA, B_WIDTH).astype(BF16)
    pad_g = lambda gg: jnp.pad(jnp.stack([gg, _exchange_rope_halves(gg)]), ((0, 0), (0, B_HEAD_PAD - B_QK)))
    return [q_norm_g.reshape(1, -1), wuq, kv_norm_g.reshape(1, -1), wk, wv,
            pad_g(q_head_g), pad_g(k_head_g)]


def _attn_kernel(qt_ref, kt_ref, q_ref, k_ref, v_ref, o_ref, m_ref, acc_ref):
    t = pl.program_id(2)
    i = qt_ref[t]
    j = kt_ref[t]
    tk = k_ref.shape[0]

    @pl.when(j == 0)
    def _():
        m_ref[...] = jnp.full_like(m_ref, NEG_BIG)
        acc_ref[...] = jnp.zeros_like(acc_ref)

    def update(rows, cols, masked):
        v = v_ref[cols, :]
        n_cols = v.shape[0]
        lane_v = lax.broadcasted_iota(jnp.int32, v.shape, 1)
        one = jnp.ones_like(v)
        for h in range(2):
            sl = slice(h * B_HEAD_PAD, (h + 1) * B_HEAD_PAD)
            s = lax.dot_general(q_ref[rows, sl], k_ref[cols, sl], (((1,), (1,)), ((), ())),
                                preferred_element_type=F32)
            if masked:
                qc = lax.broadcasted_iota(jnp.int32, s.shape, 0) // CHUNK
                kc = lax.broadcasted_iota(jnp.int32, s.shape, 1) // CHUNK
                s = jnp.where(kc <= qc, s, NEG_BIG)
            m_prev = m_ref[h, rows, :]
            m_new = jnp.maximum(m_prev, jnp.max(s, axis=-1, keepdims=True))
            alpha = jnp.exp2(m_prev - m_new)
            p = jnp.exp2((s - jnp.concatenate([m_new] * (n_cols // LANES), axis=1)).astype(BF16))
            vh = jnp.where((lane_v < B_V) == (h == 0), v, one)
            acc_ref[h, rows, :] = alpha * acc_ref[h, rows, :] + jnp.dot(p, vh, preferred_element_type=F32)
            m_ref[h, rows, :] = m_new

    @pl.when(j < i)
    def _():
        update(slice(0, tk), slice(0, tk), False)

    @pl.when(j == i)
    def _():
        update(slice(0, tk), slice(0, tk), True)
        lane = lax.broadcasted_iota(jnp.int32, o_ref.shape, 1)
        a0 = acc_ref[0]
        a1 = acc_ref[1]
        o_ref[...] = jnp.where(lane < B_V, a0 / pltpu.roll(a0, B_V, 1), a1 / pltpu.roll(a1, B_V, 1))


def _attention(q, k, v, n_batch, seq_len, tile=1024):
    tile = min(tile, seq_len)
    nt = seq_len // tile
    width = B_HEADS * B_HEAD_PAD
    q = q.reshape(n_batch, seq_len, width)
    k = k.reshape(n_batch, seq_len, width)
    v = v.reshape(n_batch, seq_len, B_WIDTH)
    q_tiles = jnp.asarray([i for i in range(nt) for _ in range(i + 1)], jnp.int32)
    k_tiles = jnp.asarray([j for i in range(nt) for j in range(i + 1)], jnp.int32)
    o = pl.pallas_call(
        _attn_kernel,
        out_shape=jax.ShapeDtypeStruct((n_batch, seq_len, B_WIDTH), F32),
        grid_spec=pltpu.PrefetchScalarGridSpec(
            num_scalar_prefetch=2,
            grid=(n_batch, B_HEADS // 2, nt * (nt + 1) // 2),
            in_specs=[pl.BlockSpec((None, tile, 2 * B_HEAD_PAD), lambda b, p, t, qt, kt: (b, qt[t], p)),
                      pl.BlockSpec((None, tile, 2 * B_HEAD_PAD), lambda b, p, t, qt, kt: (b, kt[t], p)),
                      pl.BlockSpec((None, tile, 2 * B_V), lambda b, p, t, qt, kt: (b, kt[t], p))],
            out_specs=pl.BlockSpec((None, tile, 2 * B_V), lambda b, p, t, qt, kt: (b, qt[t], p)),
            scratch_shapes=[pltpu.VMEM((2, tile, LANES), F32), pltpu.VMEM((2, tile, 2 * B_V), F32)]),
        compiler_params=_cparams(("parallel", "parallel", "arbitrary")),
        name="mla_attention",
    )(q_tiles, k_tiles, q, k, v)
    return o.reshape(n_batch * seq_len, B_WIDTH)


def _gmlp_compute(z, lng_ref, lnb_ref, ws_ref, bias_ref, og_ref, y_ref):
    z = 0.5 * z * (1.0 + lax.erf(z * (1.0 / math.sqrt(2.0))))
    u = z[:, :C_WIDTH]
    v = z[:, C_WIDTH:]
    mu = jnp.mean(v, axis=-1, keepdims=True)
    d = v - mu
    var = jnp.mean(d * d, axis=-1, keepdims=True)
    v = d * lax.rsqrt(var + EPS) * lng_ref[...] + lnb_ref[...]
    ti = lax.broadcasted_iota(jnp.int32, (C_BLOCK, C_BLOCK), 0)
    si = lax.broadcasted_iota(jnp.int32, (C_BLOCK, C_BLOCK), 1)
    group = lax.broadcasted_iota(jnp.int32, (1, C_WIDTH), 1) // C_GROUP_DIM
    ws = [jnp.where(ti >= si, ws_ref[gi], 0.0).astype(BF16) for gi in range(C_GROUPS)]
    blocks = []
    for blk in range(v.shape[0] // C_BLOCK):
        vb = v[blk * C_BLOCK:(blk + 1) * C_BLOCK]
        sv = bias_ref[...]
        for gi in range(C_GROUPS):
            sv = sv + jnp.dot(ws[gi], jnp.where(group == gi, vb, 0.0).astype(BF16),
                              preferred_element_type=F32)
        blocks.append(sv)
    sv = jnp.concatenate(blocks, axis=0) if len(blocks) > 1 else blocks[0]
    y = u * sv
    y_ref[...] = y * lax.rsqrt(jnp.mean(y * y, axis=-1, keepdims=True) + EPS) * og_ref[...]


def _gmlp_consts(ln_g, ln_b, w_s, b_s, out_g):
    bias = jnp.repeat(jnp.transpose(b_s), C_GROUP_DIM, axis=1)
    return [ln_g.reshape(1, -1), ln_b.reshape(1, -1), w_s, bias, out_g.reshape(1, -1)]


def _mix_out_compute(mix_refs, rows=slice(None)):
    x_ref, ya_ref, o_ref, yc_ref, bg_ref, wa_ref, wb_ref, wc_ref, fg_ref = mix_refs
    o = o_ref[rows, :]
    yb = o * lax.rsqrt(jnp.mean(o * o, axis=-1, keepdims=True) + EPS) * bg_ref[...]
    x = (x_ref[rows, :] + _dot(ya_ref[rows, :], wa_ref[...]) + _dot(yb, wb_ref[...])
         + _dot(yc_ref[rows, :], wc_ref[...]))
    h = x * lax.rsqrt(jnp.mean(x * x, axis=-1, keepdims=True) + EPS) * fg_ref[...]
    return x, h


def _mix_out_args(x, ya, o, yc, b_out_g, w_out, ffn_g, tm):
    wa = w_out[:A_WIDTH].astype(BF16)
    wb = w_out[A_WIDTH:A_WIDTH + B_WIDTH].astype(BF16)
    wc = w_out[A_WIDTH + B_WIDTH:].astype(BF16)
    consts = [b_out_g.reshape(1, -1), wa, wb, wc, ffn_g.reshape(1, -1)]
    specs = ([_row_spec(tm, D_MODEL), _row_spec(tm, A_WIDTH), _row_spec(tm, B_WIDTH), _row_spec(tm, C_WIDTH)]
             + [_const_spec(c.shape) for c in consts])
    return [x, ya, o, yc] + consts, specs


def _load_weight_bf16(w_hbm, dst_ref, stage_ref, sem):
    rows = dst_ref.shape[0]
    step = stage_ref.shape[1]
    starts = list(range(0, rows, step))

    def copy(i):
        n = min(step, rows - starts[i])
        return pltpu.make_async_copy(w_hbm.at[pl.ds(starts[i], n)], stage_ref.at[i % 2, pl.ds(0, n)],
                                     sem.at[i % 2])

    copy(0).start()
    for i, r0 in enumerate(starts):
        if i + 1 < len(starts):
            copy(i + 1).start()
        copy(i).wait()
        n = min(step, rows - r0)
        dst_ref[r0:r0 + n, :] = stage_ref[i % 2, 0:n, :].astype(BF16)


def _ff_chunks(d_ff, chunk):
    return [(c0, min(chunk, d_ff - c0)) for c0 in range(0, d_ff, chunk)]


def _swiglu_resident(h, acc, wg_ref, wu_ref, wd_ref, chunk, between=None):
    for ci, (c0, cn) in enumerate(_ff_chunks(wg_ref.shape[1], chunk)):
        if between is not None:
            between(ci)
        gate = jnp.dot(h, wg_ref[:, c0:c0 + cn], preferred_element_type=F32)
        up = jnp.dot(h, wu_ref[:, c0:c0 + cn], preferred_element_type=F32)
        act = (gate * _sigmoid(gate) * up).astype(BF16)
        acc = acc + jnp.dot(act, wd_ref[c0:c0 + cn, :], preferred_element_type=F32)
    return acc


def _ffn_kernel(h_ref, x_ref, wg_hbm, wu_hbm, wd_hbm, o_ref, wg_ref, wu_ref, wd_ref, stage_in, stage_out,
                sem, *, ff_chunk):
    @pl.when(pl.program_id(0) == 0)
    def _():
        _load_weight_bf16(wg_hbm, wg_ref, stage_in, sem)
        _load_weight_bf16(wu_hbm, wu_ref, stage_in, sem)
        _load_weight_bf16(wd_hbm, wd_ref, stage_out, sem)

    o_ref[...] = _swiglu_resident(h_ref[...], x_ref[...], wg_ref, wu_ref, wd_ref, ff_chunk)


STAGE_ROWS_IN = 128
STAGE_ROWS_OUT = 512


def _weight_scratch(d_ff):
    return [pltpu.VMEM((D_MODEL, d_ff), BF16), pltpu.VMEM((D_MODEL, d_ff), BF16),
            pltpu.VMEM((d_ff, D_MODEL), BF16),
            pltpu.VMEM((2, STAGE_ROWS_IN, d_ff), F32), pltpu.VMEM((2, STAGE_ROWS_OUT, D_MODEL), F32)]


def _dense_ffn(x, h, wg, wu, wd, tm=512, ff_chunk=512):
    n = x.shape[0]
    tm = min(tm, n)
    d_ff = wg.shape[1]
    any_spec = pl.BlockSpec(memory_space=pl.ANY)
    return pl.pallas_call(
        functools.partial(_ffn_kernel, ff_chunk=ff_chunk),
        out_shape=jax.ShapeDtypeStruct((n, D_MODEL), F32),
        grid=(n // tm,),
        in_specs=[_row_spec(tm, D_MODEL), _row_spec(tm, D_MODEL), any_spec, any_spec, any_spec],
        out_specs=_row_spec(tm, D_MODEL),
        scratch_shapes=_weight_scratch(d_ff) + [pltpu.SemaphoreType.DMA((2,))],
        compiler_params=_cparams(("arbitrary",)),
        name="dense_ffn",
    )(h, x, wg, wu, wd)


def _mix_out_kernel(*refs):
    x, h = _mix_out_compute(refs[:9])
    refs[9][...] = x
    if len(refs) > 10:
        refs[10][...] = h.astype(BF16)


def _mix_out(x, ya, o, yc, b_out_g, w_out, ffn_g, with_normed, tm=512):
    n = x.shape[0]
    tm = min(tm, n)
    args, specs = _mix_out_args(x, ya, o, yc, b_out_g, w_out, ffn_g, tm)
    shapes = [jax.ShapeDtypeStruct((n, D_MODEL), F32)] + [jax.ShapeDtypeStruct((n, D_MODEL), BF16)] * with_normed
    return pl.pallas_call(
        _mix_out_kernel,
        out_shape=tuple(shapes),
        grid=(n // tm,),
        in_specs=specs,
        out_specs=tuple(_row_spec(tm, D_MODEL) for _ in shapes),
        compiler_params=_cparams(("parallel",)),
        name="mix_out",
    )(*args)


def _router_kernel(x_ref, fg_ref, wr_ref, route_ref):
    x = x_ref[...]
    h = x * lax.rsqrt(jnp.mean(x * x, axis=-1, keepdims=True) + EPS) * fg_ref[...]
    logits = _dot3(h, wr_ref[...])
    lane = lax.broadcasted_iota(jnp.int32, logits.shape, 1)
    logits = jnp.where(lane < N_EXPERTS, logits, -jnp.inf)
    v1 = jnp.max(logits, axis=-1, keepdims=True)
    i1 = jnp.min(jnp.where(logits == v1, lane, LANES), axis=-1, keepdims=True)
    rest = jnp.where(lane == i1, -jnp.inf, logits)
    v2 = jnp.max(rest, axis=-1, keepdims=True)
    i2 = jnp.min(jnp.where(rest == v2, lane, LANES), axis=-1, keepdims=True)
    e2 = jnp.exp(v2 - v1)
    g1 = 1.0 / (1.0 + e2)
    g2 = e2 / (1.0 + e2)
    route_ref[...] = jnp.where(lane == 0, i1.astype(F32),
                               jnp.where(lane == 1, i2.astype(F32),
                                         jnp.where(lane == 2, g1, jnp.where(lane == 3, g2, 0.0))))


def _router(x, ffn_g, w_router, tm=512):
    n = x.shape[0]
    tm = min(tm, n)
    wr = jnp.pad(w_router, ((0, 0), (0, LANES - N_EXPERTS)))
    return pl.pallas_call(
        _router_kernel,
        out_shape=jax.ShapeDtypeStruct((n, LANES), F32),
        grid=(n // tm,),
        in_specs=[_row_spec(tm, D_MODEL), _const_spec((1, D_MODEL)), _const_spec(wr.shape)],
        out_specs=_row_spec(tm, LANES),
        compiler_params=_cparams(("parallel",)),
        name="moe_router",
    )(x, ffn_g.reshape(1, -1), wr)


def _moe_plan(route, tm):
    n = route.shape[0]
    experts = jnp.concatenate([route[:, 0], route[:, 1]]).astype(jnp.int32)
    onehot = (experts[:, None] == jnp.arange(N_EXPERTS, dtype=jnp.int32)[None, :]).astype(jnp.int32)
    csum = jnp.cumsum(onehot, axis=0)
    rank = jnp.sum(csum * onehot, axis=1) - 1
    counts = csum[-1]
    padded = (counts + tm - 1) // tm * tm
    ends = jnp.cumsum(padded)
    pos = (ends - padded)[experts] + rank
    n_rows = 2 * n + N_EXPERTS * tm
    tokens = jnp.concatenate([jnp.arange(n, dtype=jnp.int32)] * 2)
    src = jnp.zeros((n_rows,), jnp.int32).at[pos].set(tokens)
    tile_start = jnp.arange(n_rows // tm, dtype=jnp.int32) * tm
    tile_expert = jnp.minimum(jnp.sum(tile_start[:, None] >= ends[None, :], axis=1), N_EXPERTS - 1)
    n_used = (ends[-1] // tm).reshape(1)
    return pos.astype(jnp.int32), src, tile_expert.astype(jnp.int32), n_used.astype(jnp.int32)


def _moe_expert_kernel(te_ref, src_ref, nu_ref, x_hbm, fg_ref, wg_hbm, wu_hbm, wd_hbm, y_ref,
                       rows_ref, wg_ref, wu_ref, wd_ref, stage_in, stage_out, row_sem, w_sem, *, ff_chunk):
    t = pl.program_id(0)
    tm = y_ref.shape[0]
    n_used = nu_ref[0]
    slot = t % 2
    e = te_ref[t]

    def row_copy(tile, s, r):
        return pltpu.make_async_copy(x_hbm.at[pl.ds(src_ref[tile * tm + r], 1)],
                                     rows_ref.at[s, pl.ds(r, 1)], row_sem.at[s])

    def wait_rows(s):
        pltpu.make_async_copy(x_hbm.at[pl.ds(0, tm)], rows_ref.at[s], row_sem.at[s]).wait()

    new_expert = (t == 0) | (e != te_ref[jnp.maximum(t - 1, 0)])

    @pl.when(t == 0)
    def _():
        def body(r, c):
            row_copy(0, 0, r).start()
            return c
        lax.fori_loop(0, tm, body, 0, unroll=8)

    @pl.when((t < n_used) & new_expert)
    def _():
        _load_weight_bf16(wg_hbm.at[e], wg_ref, stage_in, w_sem)
        _load_weight_bf16(wu_hbm.at[e], wu_ref, stage_in, w_sem)
        _load_weight_bf16(wd_hbm.at[e], wd_ref, stage_out, w_sem)

    @pl.when(t < n_used)
    def _():
        wait_rows(slot)
        x = rows_ref[slot]
        h = (x * lax.rsqrt(jnp.mean(x * x, axis=-1, keepdims=True) + EPS) * fg_ref[...]).astype(BF16)
        rows_per_chunk = -(-tm // len(_ff_chunks(wg_ref.shape[1], ff_chunk)))

        def prefetch_rows(ci):
            for r in range(ci * rows_per_chunk, min((ci + 1) * rows_per_chunk, tm)):
                row_copy(t + 1, 1 - slot, r).start()

        y_ref[...] = _swiglu_resident(h, jnp.zeros((tm, D_MODEL), F32), wg_ref, wu_ref, wd_ref, ff_chunk,
                                      between=prefetch_rows)

    @pl.when(t == n_used)
    def _():
        wait_rows(slot)

    @pl.when((t == pl.num_programs(0) - 1) & (t < n_used))
    def _():
        wait_rows(1 - slot)

    @pl.when(t >= n_used)
    def _():
        y_ref[...] = jnp.zeros_like(y_ref)


def _moe_experts(x, src, tile_expert, n_used, ffn_g, wg, wu, wd, tm, ff_chunk=512):
    n_rows = src.shape[0]
    src = jnp.pad(src, (0, tm))
    d_ff = wg.shape[2]
    any_spec = pl.BlockSpec(memory_space=pl.ANY)
    return pl.pallas_call(
        functools.partial(_moe_expert_kernel, ff_chunk=min(ff_chunk, d_ff)),
        out_shape=jax.ShapeDtypeStruct((n_rows, D_MODEL), F32),
        grid_spec=pltpu.PrefetchScalarGridSpec(
            num_scalar_prefetch=3,
            grid=(n_rows // tm,),
            in_specs=[any_spec, pl.BlockSpec((1, D_MODEL), lambda t, *_: (0, 0)),
                      any_spec, any_spec, any_spec],
            out_specs=pl.BlockSpec((tm, D_MODEL), lambda t, *_: (t, 0)),
            scratch_shapes=[pltpu.VMEM((2, tm, D_MODEL), F32)] + _weight_scratch(d_ff)
                           + [pltpu.SemaphoreType.DMA((2,)), pltpu.SemaphoreType.DMA((2,))]),
        compiler_params=_cparams(("arbitrary",)),
        name="moe_experts",
    )(tile_expert, src, n_used, x, ffn_g.reshape(1, -1), wg, wu, wd)


def _moe_combine_kernel(pos_ref, x_ref, route_ref, y_hbm, o_ref, buf_ref, sem, *, n_tokens):
    i = pl.program_id(0)
    tm = x_ref.shape[0]
    slot = i % 2

    def start_rows(tile, s):
        def body(r, c):
            tok = tile * tm + r
            for choice in range(2):
                pltpu.make_async_copy(y_hbm.at[pl.ds(pos_ref[choice * n_tokens + tok], 1)],
                                      buf_ref.at[s, choice, pl.ds(r, 1)], sem.at[s]).start()
            return c
        lax.fori_loop(0, tm, body, 0, unroll=4)

    @pl.when(i == 0)
    def _():
        start_rows(0, 0)

    @pl.when(i + 1 < pl.num_programs(0))
    def _():
        start_rows(i + 1, 1 - slot)

    for choice in range(2):
        pltpu.make_async_copy(y_hbm.at[pl.ds(0, tm)], buf_ref.at[slot, choice], sem.at[slot]).wait()
    route = route_ref[...]
    lane = lax.broadcasted_iota(jnp.int32, route.shape, 1)
    g1 = jnp.sum(jnp.where(lane == 2, route, 0.0), axis=-1, keepdims=True)
    g2 = jnp.sum(jnp.where(lane == 3, route, 0.0), axis=-1, keepdims=True)
    o_ref[...] = x_ref[...] + g1 * buf_ref[slot, 0] + g2 * buf_ref[slot, 1]


def _moe_combine(x, route, pos, y, tm=256):
    n = x.shape[0]
    tm = min(tm, n)
    return pl.pallas_call(
        functools.partial(_moe_combine_kernel, n_tokens=n),
        out_shape=jax.ShapeDtypeStruct((n, D_MODEL), F32),
        grid_spec=pltpu.PrefetchScalarGridSpec(
            num_scalar_prefetch=1,
            grid=(n // tm,),
            in_specs=[pl.BlockSpec((tm, D_MODEL), lambda i, *_: (i, 0)),
                      pl.BlockSpec((tm, LANES), lambda i, *_: (i, 0)),
                      pl.BlockSpec(memory_space=pl.ANY)],
            out_specs=pl.BlockSpec((tm, D_MODEL), lambda i, *_: (i, 0)),
            scratch_shapes=[pltpu.VMEM((2, 2, tm, D_MODEL), F32), pltpu.SemaphoreType.DMA((2,))]),
        compiler_params=_cparams(("arbitrary",)),
        name="moe_combine",
    )(pos, x, route, y)


def _moe(x, route, ffn_g, wg, wu, wd, tm=512):
    tm = min(tm, x.shape[0])
    pos, src, tile_expert, n_used = _moe_plan(route, tm)
    y = _moe_experts(x, src, tile_expert, n_used, ffn_g, wg, wu, wd, tm)
    return _moe_combine(x, route, pos, y)


def kernel(x, positions, mix_norm_g, w_in, shift_mu, a_w0, a_w_up, a_a0, a_a_up, a_g_up, a_k_k, a_k_a,
           a_r_k, a_ln_g, a_ln_b, a_v0, a_v_down, a_v_up, b_q_norm_g, b_w_uq, b_kv_norm_g, b_w_ukv,
           b_q_head_g, b_k_head_g, b_out_g, c_ln_g, c_ln_b, c_w_s, c_b_s, c_out_g, w_out, ffn_norm_g,
           dense_w_gate, dense_w_up, dense_w_down, moe_router, moe_w_gate, moe_w_up, moe_w_down):
    n_batch, seq_len, d_model = x.shape
    depth = w_in.shape[0]
    xf = x.reshape(n_batch * seq_len, d_model)
    cos_t, sin_t = _rope_tables(positions)
    v_first = None
    for l in range(depth):
        za, q, kq, vq, yc = _mixer_in(
            xf, mix_norm_g[l], w_in[l], shift_mu[l], seq_len, cos_t, sin_t,
            _mla_pre_consts(b_q_norm_g[l], b_w_uq[l], b_kv_norm_g[l], b_w_ukv[l], b_q_head_g[l], b_k_head_g[l]),
            _gmlp_consts(c_ln_g[l], c_ln_b[l], c_w_s[l], c_b_s[l], c_out_g[l]))
        vres = _rwkv_vres_consts(a_v0[l - 1], a_v_down[l - 1], a_v_up[l - 1]) if l > 0 else None
        ya, v = _rwkv_mix(
            za, _rwkv_consts(a_w0[l], a_w_up[l], a_a0[l], a_a_up[l], a_g_up[l], a_k_k[l], a_k_a[l],
                             a_r_k[l].reshape(-1)),
            v_first if l > 0 else None, vres, a_ln_g[l], a_ln_b[l], n_batch, seq_len)
        if l == 0:
            v_first = v
        o = _attention(q, kq, vq, n_batch, seq_len)
        i = l // 2
        if l % 2 == 0:
            xf, h = _mix_out(xf, ya, o, yc, b_out_g[l], w_out[l], ffn_norm_g[l], with_normed=True)
            xf = _dense_ffn(xf, h, dense_w_gate[i], dense_w_up[i], dense_w_down[i])
        else:
            xf, = _mix_out(xf, ya, o, yc, b_out_g[l], w_out[l], ffn_norm_g[l], with_normed=False)
            route = _router(xf, ffn_norm_g[l], moe_router[i])
            xf = _moe(xf, route, ffn_norm_g[l], moe_w_gate[i], moe_w_up[i], moe_w_down[i])
    return xf.reshape(n_batch, seq_len, d_model)
```

```python
import functools
import math

import jax
import jax.numpy as jnp
from jax import lax
from jax.experimental import pallas as pl
from jax.experimental.pallas import tpu as pltpu

F32 = jnp.float32
BF16 = jnp.bfloat16

D_MODEL = 1024
CHUNK = 64
A_HEADS = 6
A_HEAD_DIM = 64
A_WIDTH = A_HEADS * A_HEAD_DIM
A_W_LORA = 32
A_A_LORA = 32
A_G_LORA = 64
A_COLS = 3 * A_WIDTH + A_W_LORA + A_A_LORA + A_G_LORA
RWKV_GN_EPS = 64e-5
B_HEADS = 6
B_NOPE = 64
B_ROPE = 32
B_QK = B_NOPE + B_ROPE
B_V = 64
B_WIDTH = B_HEADS * B_V
B_Q_LORA = 256
B_KV_LORA = 128
B_COLS = B_Q_LORA + B_KV_LORA + B_ROPE
ROPE_THETA = 10000.0
C_GROUPS = 4
C_GROUP_DIM = 64
C_WIDTH = C_GROUPS * C_GROUP_DIM
C_BLOCK = 128
C_COLS = 2 * C_WIDTH
N_EXPERTS = 8
EPS = 1e-6

LANES = 128
SUBLANES = 8
VMEM_LIMIT_BYTES = 56 * 1024 * 1024

SCAN_CHUNK = 64
PAIR = 2 * A_HEAD_DIM
N_PAIRS = A_WIDTH // PAIR
B_HEAD_PAD = LANES
NEG_BIG = -1e30
LOG2_E = math.log2(math.e)


def _cparams(sem):
    return pltpu.CompilerParams(dimension_semantics=sem, vmem_limit_bytes=VMEM_LIMIT_BYTES)


def _dot(a, b):
    return jnp.dot(a.astype(BF16), b.astype(BF16), preferred_element_type=F32)


def _dot_nt(a, b):
    return lax.dot_general(a.astype(BF16), b.astype(BF16), (((1,), (1,)), ((), ())),
                           preferred_element_type=F32)


def _dot_split(a, b, split_lhs=True):
    x = a if split_lhs else b
    hi = x.astype(BF16)
    lo = (x - hi.astype(F32)).astype(BF16)
    if split_lhs:
        e = b.astype(BF16)
        return jnp.dot(hi, e, preferred_element_type=F32) + jnp.dot(lo, e, preferred_element_type=F32)
    e = a.astype(BF16)
    return jnp.dot(e, hi, preferred_element_type=F32) + jnp.dot(e, lo, preferred_element_type=F32)


def _dot3(a, b):
    a_hi = a.astype(BF16)
    a_lo = (a - a_hi.astype(F32)).astype(BF16)
    b_hi = b.astype(BF16)
    b_lo = (b - b_hi.astype(F32)).astype(BF16)
    return (jnp.dot(a_hi, b_hi, preferred_element_type=F32) + jnp.dot(a_lo, b_hi, preferred_element_type=F32)
            + jnp.dot(a_hi, b_lo, preferred_element_type=F32))


def _dot_state(m, h):
    return _dot(m, h)


def _sigmoid(x):
    return 1.0 / (1.0 + jnp.exp(-x))


def _const_spec(shape):
    nd = len(shape)
    return pl.BlockSpec(shape, lambda *_: (0,) * nd)


def _row_spec(tm, width):
    return pl.BlockSpec((tm, width), lambda i: (i, 0))


def _mixer_in_kernel(*refs, tiles_per_seq):
    (x_ref, g_ref, wa_ref, wb_ref, wc_ref, mu_ref, cos_ref, sin_ref) = refs[:8]
    mla_consts = refs[8:15]
    gmlp_consts = refs[15:20]
    za_ref, q_o, k_o, v_o, yc_ref, carry_ref = refs[20:]
    x = x_ref[...]
    ms = jnp.mean(x * x, axis=-1, keepdims=True)
    h = (x * lax.rsqrt(ms + EPS) * g_ref[...]).astype(BF16)
    zb = jnp.dot(h, wb_ref[...], preferred_element_type=F32)
    zc = jnp.dot(h, wc_ref[...], preferred_element_type=F32)
    za = jnp.dot(h, wa_ref[...], preferred_element_type=F32)
    _mla_pre_compute(zb, cos_ref[...], sin_ref[...], *mla_consts, q_o, k_o, v_o)
    _gmlp_compute(zc, *gmlp_consts, yc_ref)
    tm = za.shape[0]
    first = jnp.where(pl.program_id(0) % tiles_per_seq == 0, 0.0, carry_ref[0:1, :])
    row = lax.broadcasted_iota(jnp.int32, za.shape, 0)
    prev = jnp.where(row == 0, first, pltpu.roll(za, 1, 0))
    carry_ref[0:1, :] = za[tm - 1:tm, :]
    za_ref[...] = za + (prev - za) * mu_ref[...]


def _mixer_in(x, g, w_in, mu, seq_len, cos_t, sin_t, mla_consts, gmlp_consts, tm=256):
    n = x.shape[0]
    wa = w_in[:, :A_COLS].astype(BF16)
    wb = jnp.pad(w_in[:, A_COLS:A_COLS + B_COLS], ((0, 0), (0, 4 * LANES - B_COLS))).astype(BF16)
    wc = w_in[:, A_COLS + B_COLS:].astype(BF16)
    consts = [g.reshape(1, -1), wa, wb, wc, mu.reshape(1, -1)]
    width = B_HEADS * B_HEAD_PAD
    return pl.pallas_call(
        functools.partial(_mixer_in_kernel, tiles_per_seq=seq_len // tm),
        out_shape=(jax.ShapeDtypeStruct((n, A_COLS), F32),
                   jax.ShapeDtypeStruct((n, width), BF16), jax.ShapeDtypeStruct((n, width), BF16),
                   jax.ShapeDtypeStruct((n, B_WIDTH), BF16), jax.ShapeDtypeStruct((n, C_WIDTH), F32)),
        grid=(n // tm,),
        in_specs=[_row_spec(tm, D_MODEL)] + [_const_spec(c.shape) for c in consts]
                 + [_row_spec(tm, LANES), _row_spec(tm, LANES)]
                 + [_const_spec(c.shape) for c in mla_consts + gmlp_consts],
        out_specs=(_row_spec(tm, A_COLS), _row_spec(tm, width), _row_spec(tm, width),
                   _row_spec(tm, B_WIDTH), _row_spec(tm, C_WIDTH)),
        scratch_shapes=[pltpu.VMEM((SUBLANES, A_COLS), F32)],
        compiler_params=_cparams(("arbitrary",)),
        name="mixer_in",
    )(x, *consts, cos_t, sin_t, *mla_consts, *gmlp_consts)


def _rwkv_pre_compute(za, w0_ref, wup_ref, a0_ref, aup_ref, gup_ref, kk_ref, ka_ref, rk_ref, bd_ref, vres):
    r = za[:, 0:A_WIDTH]
    k = za[:, A_WIDTH:2 * A_WIDTH]
    v = za[:, 2 * A_WIDTH:3 * A_WIDTH]
    lora = za[:, 3 * A_WIDTH:A_COLS]
    w = w0_ref[...] + _dot3(jnp.tanh(lora), wup_ref[...])
    w_log = -(jnp.maximum(-w, 0.0) + jnp.log(1.0 + jnp.exp(-jnp.abs(w)))) - 0.5
    lw = -jnp.exp(w_log)
    a = _sigmoid(a0_ref[...] + _dot3(lora, aup_ref[...]))
    g = _dot(_sigmoid(lora), gup_ref[...])
    if vres is not None:
        vf, v0_ref, vdn_ref, vup_ref = vres
        mix = _sigmoid(v0_ref[...] + _dot3(_dot3(v, vdn_ref[...]), vup_ref[...]))
        v = v + (vf - v) * mix
    kk = k * kk_ref[...]
    ss = _dot_split(kk * kk, bd_ref[...])
    kk = kk / jnp.maximum(jnp.sqrt(ss), 1e-12)
    k = k * (1.0 + (a - 1.0) * ka_ref[...])
    bonus = _dot_split(r * k * rk_ref[...], bd_ref[...]) * v
    return r, k, v, kk, a, lw, g, bonus


def _head_block_diag(width, head_dim):
    idx = jnp.arange(width) // head_dim
    return (idx[:, None] == idx[None, :]).astype(F32)


def _pad_rows(w, row_off, rows=LANES):
    return jnp.pad(w, ((row_off, rows - row_off - w.shape[0]), (0, 0)))


def _rwkv_consts(w0, w_up, a0, a_up, g_up, k_k, k_a, r_k):
    row = lambda t: t.reshape(1, -1)
    return [row(w0), _pad_rows(w_up, 0), row(a0), _pad_rows(a_up, A_W_LORA),
            _pad_rows(g_up, A_W_LORA + A_A_LORA), row(k_k), row(k_a), row(r_k),
            _head_block_diag(A_WIDTH, A_HEAD_DIM)]


def _rwkv_vres_consts(v0, v_down, v_up):
    return [v0.reshape(1, -1), jnp.pad(v_down, ((0, 0), (0, LANES - v_down.shape[1]))), _pad_rows(v_up, 0)]


def _stack_heads(x, lane_lo):
    return jnp.concatenate([jnp.where(lane_lo, x, 0.0), jnp.where(lane_lo, 0.0, x)], axis=0)


def _rwkv_mix_kernel(*refs, has_vres):
    za_ref = refs[0]
    pre_consts = refs[1:10]
    bd_ref = pre_consts[-1]
    if has_vres:
        vf_ref, vres_consts = refs[10], refs[11:14]
        lng_ref, lnb_ref, y_ref, v_out, h_ref = refs[14:]
    else:
        lng_ref, lnb_ref, y_ref, v_out, h_ref = refs[10:]
    n_batch, rows_per_step, _ = za_ref.shape
    L = SCAN_CHUNK
    n_chunks = rows_per_step // L
    S2 = 2 * L

    @pl.when(pl.program_id(0) == 0)
    def _():
        h_ref[...] = jnp.zeros_like(h_ref)

    ti = lax.broadcasted_iota(jnp.int32, (L, L), 0)
    si = lax.broadcasted_iota(jnp.int32, (L, L), 1)
    tril_incl = (ti >= si).astype(F32)
    row_w = lax.broadcasted_iota(jnp.int32, (L, S2), 0)
    col_w = lax.broadcasted_iota(jnp.int32, (L, S2), 1) % L
    strict = row_w > col_w
    incl = row_w >= col_w
    eye_w = row_w == col_w
    eye = (lax.broadcasted_iota(jnp.int32, (PAIR, PAIR), 0)
           == lax.broadcasted_iota(jnp.int32, (PAIR, PAIR), 1))
    lane_lo = lax.broadcasted_iota(jnp.int32, (1, PAIR), 1) < A_HEAD_DIM
    zeros_s = jnp.zeros((S2, PAIR), F32)
    mid = L // 2 - 1

    chains = []
    pre = []
    for b in range(n_batch):
        vres = (vf_ref[b], *vres_consts) if has_vres else None
        pre.append(_rwkv_pre_compute(za_ref[b], *pre_consts, vres))
        v_out[b] = pre[b][2]
    for b, ck in [(b, ck) for b in range(n_batch) for ck in range(n_chunks)]:
        rows = slice(ck * L, (ck + 1) * L)
        r_b, k_b, v_b, kk_b, a_b, lw_b = (t[rows] for t in pre[b][:6])
        lw = lw_b
        cum = _dot_split(tril_incl, lw, split_lhs=False)
        c_mid = cum[mid:mid + 1, :]
        c_end = cum[L - 1:L, :]
        kk = kk_b
        kb = kk * a_b
        kmod = k_b
        e_dn = jnp.exp(c_mid - cum)
        e_hat = jnp.exp(c_end - cum)
        at_all = -kk * jnp.exp(cum - lw - c_mid)
        bt_all = kb * e_dn
        kt_all = kmod * e_dn
        rt_all = r_b * jnp.exp(cum - c_mid)
        bh_all = kb * e_hat
        kh_all = kmod * e_hat
        p_end = jnp.exp(c_end)
        p_mid = jnp.exp(c_mid)
        at_true = at_all * p_mid
        rt_true = rt_all * p_mid
        v_all = v_b
        for p in range(N_PAIRS):
            sl = slice(p * PAIR, (p + 1) * PAIR)
            stack = lambda t: _stack_heads(t[:, sl], lane_lo)
            chains.append(dict(
                b=b, p=p, ck=ck, ar=jnp.concatenate([at_all[:, sl], rt_all[:, sl]], axis=0),
                bk_s=jnp.concatenate([stack(bt_all), stack(kt_all)], axis=0),
                bkh_t=jnp.concatenate([stack(bh_all).T, stack(kh_all).T], axis=1),
                v_s=stack(v_all), a_in_s=stack(at_true), r_in=rt_true[:, sl], p_end=p_end[:, sl]))
    stack = lambda t: _stack_heads(t, lane_lo)
    for c in chains:
        gram = _dot_nt(c["ar"], c["bk_s"])
        a_ab = jnp.where(strict, gram[:L, :S2], 0.0)
        c["a_ak"] = jnp.where(strict, gram[:L, S2:], 0.0)
        c["a_r"] = jnp.where(jnp.concatenate([incl, incl], axis=1), gram[L:], 0.0)
        c["x"] = a_ab
        c["t"] = jnp.where(eye_w, 1.0, a_ab)
    n_doublings = int(math.log2(L)) - 1
    for c in chains:
        c["x"] = _dot(c["x"], stack(c["x"]))
    for it in range(n_doublings):
        for c in chains:
            if it + 1 < n_doublings:
                both = _dot(c["x"], jnp.concatenate([stack(c["t"]), stack(c["x"])], axis=1))
                c["t"] = c["t"] + both[:, :S2]
                c["x"] = both[:, S2:]
            else:
                c["t"] = c["t"] + _dot(c["x"], stack(c["t"]))
    for c in chains:
        c["av"] = _dot(c["a_ak"], c["v_s"])
    for c in chains:
        tw = _dot(c["t"], jnp.concatenate([c["a_in_s"], stack(c["av"])], axis=1))
        c["rhs"] = jnp.concatenate(
            [jnp.concatenate([stack(tw[:, :PAIR]), stack(tw[:, PAIR:])], axis=1),
             jnp.concatenate([zeros_s, c["v_s"]], axis=1)], axis=0)
    for c in chains:
        out = _dot(jnp.concatenate([c["a_r"], c["bkh_t"]], axis=0), c["rhs"])
        c["q"] = c["r_in"] + out[:L, :PAIR]
        c["y0"] = out[:L, PAIR:]
        c["m"] = jnp.where(eye, c["p_end"], 0.0) + out[L:, :PAIR]
        c["c"] = out[L:, PAIR:]
    state = {(b, p): h_ref[b, p] for b in range(n_batch) for p in range(N_PAIRS)}
    for ck in range(n_chunks):
        for c in [c for c in chains if c["ck"] == ck]:
            out = _dot_state(jnp.concatenate([c["q"], c["m"]], axis=0), state[c["b"], c["p"]])
            c["y"] = out[:L] + c["y0"]
            state[c["b"], c["p"]] = out[L:] + c["c"]
    for (b, p), h in state.items():
        h_ref[b, p] = h
    for b in range(n_batch):
        y = jnp.concatenate(
            [jnp.concatenate([c["y"] for c in chains if c["b"] == b and c["ck"] == ck], axis=1)
             for ck in range(n_chunks)], axis=0)
        mu = _dot_split(y, bd_ref[...]) * (1.0 / A_HEAD_DIM)
        d = y - mu
        var = _dot_split(d * d, bd_ref[...]) * (1.0 / A_HEAD_DIM)
        yn = d * lax.rsqrt(var + RWKV_GN_EPS) * lng_ref[...] + lnb_ref[...]
        y_ref[b] = (yn + pre[b][7]) * pre[b][6]


def _rwkv_mix(za, pre_consts, v_first, vres_consts, ln_g, ln_b, n_batch, seq_len, chunks_per_step=4):
    rows = SCAN_CHUNK * chunks_per_step
    has_vres = v_first is not None
    shp = (n_batch, seq_len, A_WIDTH)
    seq_spec = pl.BlockSpec((n_batch, rows, A_WIDTH), lambda i: (0, i, 0))
    args = [za.reshape(n_batch, seq_len, A_COLS)] + list(pre_consts)
    specs = [pl.BlockSpec((n_batch, rows, A_COLS), lambda i: (0, i, 0))] + [_const_spec(c.shape) for c in pre_consts]
    if has_vres:
        args += [v_first.reshape(shp)] + list(vres_consts)
        specs += [seq_spec] + [_const_spec(c.shape) for c in vres_consts]
    norm = [ln_g.reshape(1, -1), ln_b.reshape(1, -1)]
    y, v = pl.pallas_call(
        functools.partial(_rwkv_mix_kernel, has_vres=has_vres),
        out_shape=(jax.ShapeDtypeStruct(shp, F32), jax.ShapeDtypeStruct(shp, F32)),
        grid=(seq_len // rows,),
        in_specs=specs + [_const_spec(c.shape) for c in norm],
        out_specs=(seq_spec, seq_spec),
        scratch_shapes=[pltpu.VMEM((n_batch, N_PAIRS, PAIR, PAIR), F32)],
        compiler_params=_cparams(("arbitrary",)),
        name="rwkv_mix",
    )(*args, *norm)
    return y.reshape(n_batch * seq_len, A_WIDTH), v.reshape(n_batch * seq_len, A_WIDTH)


def _exchange_rope_halves(t):
    half = B_ROPE // 2
    nope = jnp.zeros_like(t[..., :B_NOPE])
    return jnp.concatenate([nope, t[..., B_NOPE + half:], t[..., B_NOPE:B_NOPE + half]], axis=-1)


def _mla_pre_compute(zb, cos_t, sin_t, gq_ref, wuq_ref, gkv_ref, wk_ref, wv_ref, gqh_ref, gkh_ref,
                     q_o, k_o, v_o):
    cq = zb[:, 0:B_Q_LORA]
    ckv = zb[:, B_Q_LORA:B_Q_LORA + B_KV_LORA]
    kr_raw = zb[:, B_Q_LORA + B_KV_LORA:]
    half = B_ROPE // 2
    lane = lax.broadcasted_iota(jnp.int32, kr_raw.shape, 1)
    kr = pltpu.roll(kr_raw, B_NOPE, 1)
    kr_sw = jnp.where(lane < B_NOPE + half, pltpu.roll(kr_raw, B_NOPE - half, 1),
                      pltpu.roll(kr_raw, B_NOPE + half, 1))
    qn = cq * lax.rsqrt(jnp.mean(cq * cq, axis=-1, keepdims=True) + EPS) * gq_ref[...]
    kvn = ckv * lax.rsqrt(jnp.mean(ckv * ckv, axis=-1, keepdims=True) + EPS) * gkv_ref[...]
    width = B_HEADS * B_HEAD_PAD
    q2 = _dot(qn, wuq_ref[...])
    k = _dot(kvn, wk_ref[...])
    v_o[...] = _dot(kvn, wv_ref[...]).astype(BF16)
    q_scale = LOG2_E / math.sqrt(B_QK)
    gq_cos = cos_t * (gqh_ref[0:1, :] * q_scale)
    gq_sin = sin_t * (gqh_ref[1:2, :] * q_scale)
    gk_cos = cos_t * gkh_ref[0:1, :]
    gk_sin = kr_sw * sin_t * gkh_ref[1:2, :]
    for h in range(B_HEADS):
        sl = slice(h * B_HEAD_PAD, (h + 1) * B_HEAD_PAD)
        qh = q2[:, sl]
        q_sw = q2[:, width + h * B_HEAD_PAD:width + (h + 1) * B_HEAD_PAD]
        kh = k[:, sl] + kr
        rs_q = lax.rsqrt(jnp.sum(qh * qh, axis=-1, keepdims=True) * (1.0 / B_QK) + EPS)
        rs_k = lax.rsqrt(jnp.sum(kh * kh, axis=-1, keepdims=True) * (1.0 / B_QK) + EPS)
        q_o[:, sl] = ((qh * gq_cos + q_sw * gq_sin) * rs_q).astype(BF16)
        k_o[:, sl] = ((kh * gk_cos + gk_sin) * rs_k).astype(BF16)


def _rope_tables(positions):
    half = B_ROPE // 2
    inv_freq = ROPE_THETA ** (-jnp.arange(0, B_ROPE, 2, dtype=F32) / B_ROPE)
    ang = positions.astype(F32).reshape(-1, 1) * inv_freq
    cos, sin = jnp.cos(ang), jnp.sin(ang)
    n = ang.shape[0]
    ones = jnp.ones((n, B_NOPE), F32)
    zeros = jnp.zeros((n, B_NOPE), F32)
    tail = jnp.zeros((n, B_HEAD_PAD - B_QK), F32)
    cos_t = jnp.concatenate([ones, cos, cos, tail], axis=1)
    sin_t = jnp.concatenate([zeros, -sin, sin, tail], axis=1)
    assert cos_t.shape[1] == B_HEAD_PAD and half * 2 == B_ROPE
    return cos_t, sin_t


def _pad_heads(w, heads, dim):
    kdim = w.shape[0]
    w = w.reshape(kdim, heads, dim)
    return jnp.pad(w, ((0, 0), (0, 0), (0, B_HEAD_PAD - dim))).reshape(kdim, heads * B_HEAD_PAD)


def _mla_pre_consts(q_norm_g, w_uq, kv_norm_g, w_ukv, q_head_g, k_head_g):
    w_uq_x = _exchange_rope_halves(w_uq.reshape(B_Q_LORA, B_HEADS, B_QK)).reshape(B_Q_LORA, -1)
    wuq = jnp.concatenate([_pad_heads(w_uq, B_HEADS, B_QK), _pad_heads(w_uq_x, B_HEADS, B_QK)],
                          axis=1).astype(BF16)
    w_ukv = w_ukv.reshape(B_KV_LORA, B_HEADS, B_NOPE + B_V)
    wk = _pad_heads(w_ukv[:, :, :B_NOPE].reshape(B_KV_LORA, -1), B_HEADS, B_NOPE).astype(BF16)
    wv = w_ukv[:, :, B_NOPE:].reshape(B_KV_LORA, B_WIDTH).astype(BF16)
    pad_g = lambda gg: jnp.pad(jnp.stack([gg, _exchange_rope_halves(gg)]), ((0, 0), (0, B_HEAD_PAD - B_QK)))
    return [q_norm_g.reshape(1, -1), wuq, kv_norm_g.reshape(1, -1), wk, wv,
            pad_g(q_head_g), pad_g(k_head_g)]


def _attn_kernel(qt_ref, kt_ref, q_ref, k_ref, v_ref, o_ref, m_ref, acc_ref):
    t = pl.program_id(2)
    i = qt_ref[t]
    j = kt_ref[t]
    tk = k_ref.shape[0]

    @pl.when(j == 0)
    def _():
        m_ref[...] = jnp.full_like(m_ref, NEG_BIG)
        acc_ref[...] = jnp.zeros_like(acc_ref)

    def update(rows, cols, masked):
        v = v_ref[cols, :]
        n_cols = v.shape[0]
        lane_v = lax.broadcasted_iota(jnp.int32, v.shape, 1)
        one = jnp.ones_like(v)
        for h in range(2):
            sl = slice(h * B_HEAD_PAD, (h + 1) * B_HEAD_PAD)
            s = lax.dot_general(q_ref[rows, sl], k_ref[cols, sl], (((1,), (1,)), ((), ())),
                                preferred_element_type=F32)
            if masked:
                qc = lax.broadcasted_iota(jnp.int32, s.shape, 0) // CHUNK
                kc = lax.broadcasted_iota(jnp.int32, s.shape, 1) // CHUNK
                s = jnp.where(kc <= qc, s, NEG_BIG)
            m_prev = m_ref[h, rows, :]
            m_new = jnp.maximum(m_prev, jnp.max(s, axis=-1, keepdims=True))
            alpha = jnp.exp2(m_prev - m_new)
            p = jnp.exp2((s - jnp.concatenate([m_new] * (n_cols // LANES), axis=1)).astype(BF16))
            vh = jnp.where((lane_v < B_V) == (h == 0), v, one)
            acc_ref[h, rows, :] = alpha * acc_ref[h, rows, :] + jnp.dot(p, vh, preferred_element_type=F32)
            m_ref[h, rows, :] = m_new

    @pl.when(j < i)
    def _():
        update(slice(0, tk), slice(0, tk), False)

    @pl.when(j == i)
    def _():
        update(slice(0, tk), slice(0, tk), True)
        lane = lax.broadcasted_iota(jnp.int32, o_ref.shape, 1)
        a0 = acc_ref[0]
        a1 = acc_ref[1]
        o_ref[...] = jnp.where(lane < B_V, a0 / pltpu.roll(a0, B_V, 1), a1 / pltpu.roll(a1, B_V, 1))


def _attention(q, k, v, n_batch, seq_len, tile=1024):
    tile = min(tile, seq_len)
    nt = seq_len // tile
    width = B_HEADS * B_HEAD_PAD
    q = q.reshape(n_batch, seq_len, width)
    k = k.reshape(n_batch, seq_len, width)
    v = v.reshape(n_batch, seq_len, B_WIDTH)
    q_tiles = jnp.asarray([i for i in range(nt) for _ in range(i + 1)], jnp.int32)
    k_tiles = jnp.asarray([j for i in range(nt) for j in range(i + 1)], jnp.int32)
    o = pl.pallas_call(
        _attn_kernel,
        out_shape=jax.ShapeDtypeStruct((n_batch, seq_len, B_WIDTH), F32),
        grid_spec=pltpu.PrefetchScalarGridSpec(
            num_scalar_prefetch=2,
            grid=(n_batch, B_HEADS // 2, nt * (nt + 1) // 2),
            in_specs=[pl.BlockSpec((None, tile, 2 * B_HEAD_PAD), lambda b, p, t, qt, kt: (b, qt[t], p)),
                      pl.BlockSpec((None, tile, 2 * B_HEAD_PAD), lambda b, p, t, qt, kt: (b, kt[t], p)),
                      pl.BlockSpec((None, tile, 2 * B_V), lambda b, p, t, qt, kt: (b, kt[t], p))],
            out_specs=pl.BlockSpec((None, tile, 2 * B_V), lambda b, p, t, qt, kt: (b, qt[t], p)),
            scratch_shapes=[pltpu.VMEM((2, tile, LANES), F32), pltpu.VMEM((2, tile, 2 * B_V), F32)]),
        compiler_params=_cparams(("parallel", "parallel", "arbitrary")),
        name="mla_attention",
    )(q_tiles, k_tiles, q, k, v)
    return o.reshape(n_batch * seq_len, B_WIDTH)


def _gmlp_compute(z, lng_ref, lnb_ref, ws_ref, bias_ref, og_ref, y_ref):
    z = 0.5 * z * (1.0 + lax.erf(z * (1.0 / math.sqrt(2.0))))
    u = z[:, :C_WIDTH]
    v = z[:, C_WIDTH:]
    mu = jnp.mean(v, axis=-1, keepdims=True)
    d = v - mu
    var = jnp.mean(d * d, axis=-1, keepdims=True)
    v = d * lax.rsqrt(var + EPS) * lng_ref[...] + lnb_ref[...]
    ti = lax.broadcasted_iota(jnp.int32, (C_BLOCK, C_BLOCK), 0)
    si = lax.broadcasted_iota(jnp.int32, (C_BLOCK, C_BLOCK), 1)
    group = lax.broadcasted_iota(jnp.int32, (1, C_WIDTH), 1) // C_GROUP_DIM
    ws = [jnp.where(ti >= si, ws_ref[gi], 0.0).astype(BF16) for gi in range(C_GROUPS)]
    blocks = []
    for blk in range(v.shape[0] // C_BLOCK):
        vb = v[blk * C_BLOCK:(blk + 1) * C_BLOCK]
        sv = bias_ref[...]
        for gi in range(C_GROUPS):
            sv = sv + jnp.dot(ws[gi], jnp.where(group == gi, vb, 0.0).astype(BF16),
                              preferred_element_type=F32)
        blocks.append(sv)
    sv = jnp.concatenate(blocks, axis=0) if len(blocks) > 1 else blocks[0]
    y = u * sv
    y_ref[...] = y * lax.rsqrt(jnp.mean(y * y, axis=-1, keepdims=True) + EPS) * og_ref[...]


def _gmlp_consts(ln_g, ln_b, w_s, b_s, out_g):
    bias = jnp.repeat(jnp.transpose(b_s), C_GROUP_DIM, axis=1)
    return [ln_g.reshape(1, -1), ln_b.reshape(1, -1), w_s, bias, out_g.reshape(1, -1)]


def _mix_out_compute(mix_refs, rows=slice(None)):
    x_ref, ya_ref, o_ref, yc_ref, bg_ref, wa_ref, wb_ref, wc_ref, fg_ref = mix_refs
    o = o_ref[rows, :]
    yb = o * lax.rsqrt(jnp.mean(o * o, axis=-1, keepdims=True) + EPS) * bg_ref[...]
    x = (x_ref[rows, :] + _dot(ya_ref[rows, :], wa_ref[...]) + _dot(yb, wb_ref[...])
         + _dot(yc_ref[rows, :], wc_ref[...]))
    h = x * lax.rsqrt(jnp.mean(x * x, axis=-1, keepdims=True) + EPS) * fg_ref[...]
    return x, h


def _mix_out_args(x, ya, o, yc, b_out_g, w_out, ffn_g, tm):
    wa = w_out[:A_WIDTH].astype(BF16)
    wb = w_out[A_WIDTH:A_WIDTH + B_WIDTH].astype(BF16)
    wc = w_out[A_WIDTH + B_WIDTH:].astype(BF16)
    consts = [b_out_g.reshape(1, -1), wa, wb, wc, ffn_g.reshape(1, -1)]
    specs = ([_row_spec(tm, D_MODEL), _row_spec(tm, A_WIDTH), _row_spec(tm, B_WIDTH), _row_spec(tm, C_WIDTH)]
             + [_const_spec(c.shape) for c in consts])
    return [x, ya, o, yc] + consts, specs


def _ff_chunks(d_ff, chunk):
    return [(c0, min(chunk, d_ff - c0)) for c0 in range(0, d_ff, chunk)]


class _WeightStream:
    def __init__(self, hbm, resident, stage_cols, stage_rows, sem, chunk):
        self.hbm, self.resident = hbm, resident
        self.stage_cols, self.stage_rows, self.sem = stage_cols, stage_rows, sem
        self.chunks = _ff_chunks(resident[0].shape[1], chunk)

    def _copies(self, ci):
        (c0, cn), slot = self.chunks[ci], ci % 2
        wg, wu, wd = self.hbm
        return (pltpu.make_async_copy(wg.at[:, pl.ds(c0, cn)], self.stage_cols.at[slot, 0, :, pl.ds(0, cn)],
                                      self.sem.at[slot]),
                pltpu.make_async_copy(wu.at[:, pl.ds(c0, cn)], self.stage_cols.at[slot, 1, :, pl.ds(0, cn)],
                                      self.sem.at[slot]),
                pltpu.make_async_copy(wd.at[pl.ds(c0, cn)], self.stage_rows.at[slot, pl.ds(0, cn)],
                                      self.sem.at[slot]))

    def start(self, ci):
        for c in self._copies(ci):
            c.start()

    def finish(self, ci):
        for c in self._copies(ci):
            c.wait()
        (c0, cn), slot = self.chunks[ci], ci % 2
        wg_ref, wu_ref, wd_ref = self.resident
        wg_ref[:, c0:c0 + cn] = self.stage_cols[slot, 0, :, 0:cn].astype(BF16)
        wu_ref[:, c0:c0 + cn] = self.stage_cols[slot, 1, :, 0:cn].astype(BF16)
        wd_ref[c0:c0 + cn, :] = self.stage_rows[slot, 0:cn, :].astype(BF16)

    def load_all(self):
        self.start(0)
        for ci in range(len(self.chunks)):
            if ci + 1 < len(self.chunks):
                self.start(ci + 1)
            self.finish(ci)


def _swiglu_resident(h, acc, wg_ref, wu_ref, wd_ref, chunk, between=None, after=None):
    for ci, (c0, cn) in enumerate(_ff_chunks(wg_ref.shape[1], chunk)):
        if between is not None:
            between(ci)
        gate = jnp.dot(h, wg_ref[:, c0:c0 + cn], preferred_element_type=F32)
        up = jnp.dot(h, wu_ref[:, c0:c0 + cn], preferred_element_type=F32)
        act = (gate * _sigmoid(gate) * up).astype(BF16)
        acc = acc + jnp.dot(act, wd_ref[c0:c0 + cn, :], preferred_element_type=F32)
        if after is not None:
            after(ci)
    return acc


def _ffn_kernel(h_ref, x_ref, wg_hbm, wu_hbm, wd_hbm, o_ref, wg_ref, wu_ref, wd_ref, stage_cols, stage_rows,
                sem, *, ff_chunk):
    @pl.when(pl.program_id(0) == 0)
    def _():
        _WeightStream((wg_hbm, wu_hbm, wd_hbm), (wg_ref, wu_ref, wd_ref), stage_cols, stage_rows, sem,
                      ff_chunk).load_all()

    o_ref[...] = _swiglu_resident(h_ref[...], x_ref[...], wg_ref, wu_ref, wd_ref, ff_chunk)


def _weight_scratch(d_ff, ff_chunk):
    return [pltpu.VMEM((D_MODEL, d_ff), BF16), pltpu.VMEM((D_MODEL, d_ff), BF16),
            pltpu.VMEM((d_ff, D_MODEL), BF16),
            pltpu.VMEM((2, 2, D_MODEL, ff_chunk), F32), pltpu.VMEM((2, ff_chunk, D_MODEL), F32)]


def _dense_ffn(x, h, wg, wu, wd, tm=512, ff_chunk=512):
    n = x.shape[0]
    tm = min(tm, n)
    d_ff = wg.shape[1]
    any_spec = pl.BlockSpec(memory_space=pl.ANY)
    return pl.pallas_call(
        functools.partial(_ffn_kernel, ff_chunk=ff_chunk),
        out_shape=jax.ShapeDtypeStruct((n, D_MODEL), F32),
        grid=(n // tm,),
        in_specs=[_row_spec(tm, D_MODEL), _row_spec(tm, D_MODEL), any_spec, any_spec, any_spec],
        out_specs=_row_spec(tm, D_MODEL),
        scratch_shapes=_weight_scratch(d_ff, ff_chunk) + [pltpu.SemaphoreType.DMA((2,))],
        compiler_params=_cparams(("arbitrary",)),
        name="dense_ffn",
    )(h, x, wg, wu, wd)


def _mix_out_kernel(*refs):
    x, h = _mix_out_compute(refs[:9])
    refs[9][...] = x
    if len(refs) > 10:
        refs[10][...] = h.astype(BF16)


def _mix_out(x, ya, o, yc, b_out_g, w_out, ffn_g, with_normed, tm=512):
    n = x.shape[0]
    tm = min(tm, n)
    args, specs = _mix_out_args(x, ya, o, yc, b_out_g, w_out, ffn_g, tm)
    shapes = [jax.ShapeDtypeStruct((n, D_MODEL), F32)] + [jax.ShapeDtypeStruct((n, D_MODEL), BF16)] * with_normed
    return pl.pallas_call(
        _mix_out_kernel,
        out_shape=tuple(shapes),
        grid=(n // tm,),
        in_specs=specs,
        out_specs=tuple(_row_spec(tm, D_MODEL) for _ in shapes),
        compiler_params=_cparams(("parallel",)),
        name="mix_out",
    )(*args)


def _router_kernel(x_ref, fg_ref, wr_ref, route_ref):
    x = x_ref[...]
    h = x * lax.rsqrt(jnp.mean(x * x, axis=-1, keepdims=True) + EPS) * fg_ref[...]
    logits = _dot3(h, wr_ref[...])
    lane = lax.broadcasted_iota(jnp.int32, logits.shape, 1)
    logits = jnp.where(lane < N_EXPERTS, logits, -jnp.inf)
    v1 = jnp.max(logits, axis=-1, keepdims=True)
    i1 = jnp.min(jnp.where(logits == v1, lane, LANES), axis=-1, keepdims=True)
    rest = jnp.where(lane == i1, -jnp.inf, logits)
    v2 = jnp.max(rest, axis=-1, keepdims=True)
    i2 = jnp.min(jnp.where(rest == v2, lane, LANES), axis=-1, keepdims=True)
    e2 = jnp.exp(v2 - v1)
    g1 = 1.0 / (1.0 + e2)
    g2 = e2 / (1.0 + e2)
    route_ref[...] = jnp.where(lane == 0, i1.astype(F32),
                               jnp.where(lane == 1, i2.astype(F32),
                                         jnp.where(lane == 2, g1, jnp.where(lane == 3, g2, 0.0))))


def _router(x, ffn_g, w_router, tm=512):
    n = x.shape[0]
    tm = min(tm, n)
    wr = jnp.pad(w_router, ((0, 0), (0, LANES - N_EXPERTS)))
    return pl.pallas_call(
        _router_kernel,
        out_shape=jax.ShapeDtypeStruct((n, LANES), F32),
        grid=(n // tm,),
        in_specs=[_row_spec(tm, D_MODEL), _const_spec((1, D_MODEL)), _const_spec(wr.shape)],
        out_specs=_row_spec(tm, LANES),
        compiler_params=_cparams(("parallel",)),
        name="moe_router",
    )(x, ffn_g.reshape(1, -1), wr)


def _moe_plan(route, tm):
    n = route.shape[0]
    experts = jnp.concatenate([route[:, 0], route[:, 1]]).astype(jnp.int32)
    onehot = (experts[:, None] == jnp.arange(N_EXPERTS, dtype=jnp.int32)[None, :]).astype(jnp.int32)
    csum = jnp.cumsum(onehot, axis=0)
    rank = jnp.sum(csum * onehot, axis=1) - 1
    counts = csum[-1]
    padded = (counts + tm - 1) // tm * tm
    ends = jnp.cumsum(padded)
    pos = (ends - padded)[experts] + rank
    n_rows = 2 * n + N_EXPERTS * tm
    tokens = jnp.concatenate([jnp.arange(n, dtype=jnp.int32)] * 2)
    src = jnp.zeros((n_rows,), jnp.int32).at[pos].set(tokens)
    tile_start = jnp.arange(n_rows // tm, dtype=jnp.int32) * tm
    tile_expert = jnp.minimum(jnp.sum(tile_start[:, None] >= ends[None, :], axis=1), N_EXPERTS - 1)
    n_used = (ends[-1] // tm).reshape(1)
    return pos.astype(jnp.int32), src, tile_expert.astype(jnp.int32), n_used.astype(jnp.int32)


def _moe_expert_kernel(te_ref, src_ref, nu_ref, x_hbm, fg_ref, wg_hbm, wu_hbm, wd_hbm, y_ref,
                       rows_ref, wg_ref, wu_ref, wd_ref, stage_cols, stage_rows, row_sem, w_sem, *, ff_chunk):
    t = pl.program_id(0)
    tm = y_ref.shape[0]
    n_used = nu_ref[0]
    slot = t % 2
    e = te_ref[t]
    e_next = te_ref[jnp.minimum(t + 1, pl.num_programs(0) - 1)]
    hand_over = (t + 1 < n_used) & (e_next != e)

    def stream(expert):
        return _WeightStream((wg_hbm.at[expert], wu_hbm.at[expert], wd_hbm.at[expert]),
                             (wg_ref, wu_ref, wd_ref), stage_cols, stage_rows, w_sem, ff_chunk)

    def row_copy(tile, s, r):
        return pltpu.make_async_copy(x_hbm.at[pl.ds(src_ref[tile * tm + r], 1)],
                                     rows_ref.at[s, pl.ds(r, 1)], row_sem.at[s])

    def wait_rows(s):
        pltpu.make_async_copy(x_hbm.at[pl.ds(0, tm)], rows_ref.at[s], row_sem.at[s]).wait()

    @pl.when(t == 0)
    def _():
        def body(r, c):
            row_copy(0, 0, r).start()
            return c
        lax.fori_loop(0, tm, body, 0, unroll=8)
        stream(e).load_all()

    @pl.when(t < n_used)
    def _():
        wait_rows(slot)
        x = rows_ref[slot]
        h = (x * lax.rsqrt(jnp.mean(x * x, axis=-1, keepdims=True) + EPS) * fg_ref[...]).astype(BF16)
        n_chunks = len(_ff_chunks(wg_ref.shape[1], ff_chunk))
        rows_per_chunk = -(-tm // n_chunks)
        nxt = stream(e_next)

        def before_chunk(ci):
            for r in range(ci * rows_per_chunk, min((ci + 1) * rows_per_chunk, tm)):
                row_copy(t + 1, 1 - slot, r).start()
            if ci == 0:
                pl.when(hand_over)(lambda: nxt.start(0))

        def after_chunk(ci):
            @pl.when(hand_over)
            def _():
                if ci + 1 < n_chunks:
                    nxt.start(ci + 1)
                nxt.finish(ci)

        y_ref[...] = _swiglu_resident(h, jnp.zeros((tm, D_MODEL), F32), wg_ref, wu_ref, wd_ref, ff_chunk,
                                      between=before_chunk, after=after_chunk)

    @pl.when(t == n_used)
    def _():
        wait_rows(slot)

    @pl.when((t == pl.num_programs(0) - 1) & (t < n_used))
    def _():
        wait_rows(1 - slot)

    @pl.when(t >= n_used)
    def _():
        y_ref[...] = jnp.zeros_like(y_ref)


def _moe_experts(x, src, tile_expert, n_used, ffn_g, wg, wu, wd, tm, ff_chunk=512):
    n_rows = src.shape[0]
    src = jnp.pad(src, (0, tm))
    d_ff = wg.shape[2]
    any_spec = pl.BlockSpec(memory_space=pl.ANY)
    return pl.pallas_call(
        functools.partial(_moe_expert_kernel, ff_chunk=min(ff_chunk, d_ff)),
        out_shape=jax.ShapeDtypeStruct((n_rows, D_MODEL), F32),
        grid_spec=pltpu.PrefetchScalarGridSpec(
            num_scalar_prefetch=3,
            grid=(n_rows // tm,),
            in_specs=[any_spec, pl.BlockSpec((1, D_MODEL), lambda t, *_: (0, 0)),
                      any_spec, any_spec, any_spec],
            out_specs=pl.BlockSpec((tm, D_MODEL), lambda t, *_: (t, 0)),
            scratch_shapes=[pltpu.VMEM((2, tm, D_MODEL), F32)] + _weight_scratch(d_ff, min(ff_chunk, d_ff))
                           + [pltpu.SemaphoreType.DMA((2,)), pltpu.SemaphoreType.DMA((2,))]),
        compiler_params=_cparams(("arbitrary",)),
        name="moe_experts",
    )(tile_expert, src, n_used, x, ffn_g.reshape(1, -1), wg, wu, wd)


def _moe_combine_kernel(pos_ref, x_ref, route_ref, y_hbm, o_ref, buf_ref, sem, *, n_tokens):
    i = pl.program_id(0)
    tm = x_ref.shape[0]
    slot = i % 2

    def start_rows(tile, s):
        def body(r, c):
            tok = tile * tm + r
            for choice in range(2):
                pltpu.make_async_copy(y_hbm.at[pl.ds(pos_ref[choice * n_tokens + tok], 1)],
                                      buf_ref.at[s, choice, pl.ds(r, 1)], sem.at[s]).start()
            return c
        lax.fori_loop(0, tm, body, 0, unroll=4)

    @pl.when(i == 0)
    def _():
        start_rows(0, 0)

    @pl.when(i + 1 < pl.num_programs(0))
    def _():
        start_rows(i + 1, 1 - slot)

    for choice in range(2):
        pltpu.make_async_copy(y_hbm.at[pl.ds(0, tm)], buf_ref.at[slot, choice], sem.at[slot]).wait()
    route = route_ref[...]
    lane = lax.broadcasted_iota(jnp.int32, route.shape, 1)
    g1 = jnp.sum(jnp.where(lane == 2, route, 0.0), axis=-1, keepdims=True)
    g2 = jnp.sum(jnp.where(lane == 3, route, 0.0), axis=-1, keepdims=True)
    o_ref[...] = x_ref[...] + g1 * buf_ref[slot, 0] + g2 * buf_ref[slot, 1]


def _moe_combine(x, route, pos, y, tm=256):
    n = x.shape[0]
    tm = min(tm, n)
    return pl.pallas_call(
        functools.partial(_moe_combine_kernel, n_tokens=n),
        out_shape=jax.ShapeDtypeStruct((n, D_MODEL), F32),
        grid_spec=pltpu.PrefetchScalarGridSpec(
            num_scalar_prefetch=1,
            grid=(n // tm,),
            in_specs=[pl.BlockSpec((tm, D_MODEL), lambda i, *_: (i, 0)),
                      pl.BlockSpec((tm, LANES), lambda i, *_: (i, 0)),
                      pl.BlockSpec(memory_space=pl.ANY)],
            out_specs=pl.BlockSpec((tm, D_MODEL), lambda i, *_: (i, 0)),
            scratch_shapes=[pltpu.VMEM((2, 2, tm, D_MODEL), F32), pltpu.SemaphoreType.DMA((2,))]),
        compiler_params=_cparams(("arbitrary",)),
        name="moe_combine",
    )(pos, x, route, y)


def _moe(x, route, ffn_g, wg, wu, wd, tm=512):
    tm = min(tm, x.shape[0])
    pos, src, tile_expert, n_used = _moe_plan(route, tm)
    y = _moe_experts(x, src, tile_expert, n_used, ffn_g, wg, wu, wd, tm)
    return _moe_combine(x, route, pos, y)


def kernel(x, positions, mix_norm_g, w_in, shift_mu, a_w0, a_w_up, a_a0, a_a_up, a_g_up, a_k_k, a_k_a,
           a_r_k, a_ln_g, a_ln_b, a_v0, a_v_down, a_v_up, b_q_norm_g, b_w_uq, b_kv_norm_g, b_w_ukv,
           b_q_head_g, b_k_head_g, b_out_g, c_ln_g, c_ln_b, c_w_s, c_b_s, c_out_g, w_out, ffn_norm_g,
           dense_w_gate, dense_w_up, dense_w_down, moe_router, moe_w_gate, moe_w_up, moe_w_down):
    n_batch, seq_len, d_model = x.shape
    depth = w_in.shape[0]
    xf = x.reshape(n_batch * seq_len, d_model)
    cos_t, sin_t = _rope_tables(positions)
    v_first = None
    for l in range(depth):
        za, q, kq, vq, yc = _mixer_in(
            xf, mix_norm_g[l], w_in[l], shift_mu[l], seq_len, cos_t, sin_t,
            _mla_pre_consts(b_q_norm_g[l], b_w_uq[l], b_kv_norm_g[l], b_w_ukv[l], b_q_head_g[l], b_k_head_g[l]),
            _gmlp_consts(c_ln_g[l], c_ln_b[l], c_w_s[l], c_b_s[l], c_out_g[l]))
        vres = _rwkv_vres_consts(a_v0[l - 1], a_v_down[l - 1], a_v_up[l - 1]) if l > 0 else None
        ya, v = _rwkv_mix(
            za, _rwkv_consts(a_w0[l], a_w_up[l], a_a0[l], a_a_up[l], a_g_up[l], a_k_k[l], a_k_a[l],
                             a_r_k[l].reshape(-1)),
            v_first if l > 0 else None, vres, a_ln_g[l], a_ln_b[l], n_batch, seq_len)
        if l == 0:
            v_first = v
        o = _attention(q, kq, vq, n_batch, seq_len)
        i = l // 2
        if l % 2 == 0:
            xf, h = _mix_out(xf, ya, o, yc, b_out_g[l], w_out[l], ffn_norm_g[l], with_normed=True)
            xf = _dense_ffn(xf, h, dense_w_gate[i], dense_w_up[i], dense_w_down[i])
        else:
            xf, = _mix_out(xf, ya, o, yc, b_out_g[l], w_out[l], ffn_norm_g[l], with_normed=False)
            route = _router(xf, ffn_norm_g[l], moe_router[i])
            xf = _moe(xf, route, ffn_norm_g[l], moe_w_gate[i], moe_w_up[i], moe_w_down[i])
    return xf.reshape(n_batch, seq_len, d_model)
```

```python
import functools
import math

import jax
import jax.numpy as jnp
from jax import lax
from jax.experimental import pallas as pl
from jax.experimental.pallas import tpu as pltpu

F32 = jnp.float32
BF16 = jnp.bfloat16

D_MODEL = 1024
CHUNK = 64
A_HEADS = 6
A_HEAD_DIM = 64
A_WIDTH = A_HEADS * A_HEAD_DIM
A_W_LORA = 32
A_A_LORA = 32
A_G_LORA = 64
A_COLS = 3 * A_WIDTH + A_W_LORA + A_A_LORA + A_G_LORA
RWKV_GN_EPS = 64e-5
B_HEADS = 6
B_NOPE = 64
B_ROPE = 32
B_QK = B_NOPE + B_ROPE
B_V = 64
B_WIDTH = B_HEADS * B_V
B_Q_LORA = 256
B_KV_LORA = 128
B_COLS = B_Q_LORA + B_KV_LORA + B_ROPE
ROPE_THETA = 10000.0
C_GROUPS = 4
C_GROUP_DIM = 64
C_WIDTH = C_GROUPS * C_GROUP_DIM
C_BLOCK = 128
C_COLS = 2 * C_WIDTH
N_EXPERTS = 8
EPS = 1e-6

LANES = 128
SUBLANES = 8
VMEM_LIMIT_BYTES = 56 * 1024 * 1024

SCAN_CHUNK = 64
PAIR = 2 * A_HEAD_DIM
N_PAIRS = A_WIDTH // PAIR
B_HEAD_PAD = LANES
NEG_BIG = -1e30
LOG2_E = math.log2(math.e)


def _cparams(sem):
    return pltpu.CompilerParams(dimension_semantics=sem, vmem_limit_bytes=VMEM_LIMIT_BYTES)


def _dot(a, b):
    return jnp.dot(a.astype(BF16), b.astype(BF16), preferred_element_type=F32)


def _dot_nt(a, b):
    return lax.dot_general(a.astype(BF16), b.astype(BF16), (((1,), (1,)), ((), ())),
                           preferred_element_type=F32)


def _dot_split(a, b, split_lhs=True):
    x = a if split_lhs else b
    hi = x.astype(BF16)
    lo = (x - hi.astype(F32)).astype(BF16)
    if split_lhs:
        e = b.astype(BF16)
        return jnp.dot(hi, e, preferred_element_type=F32) + jnp.dot(lo, e, preferred_element_type=F32)
    e = a.astype(BF16)
    return jnp.dot(e, hi, preferred_element_type=F32) + jnp.dot(e, lo, preferred_element_type=F32)


def _dot3(a, b):
    a_hi = a.astype(BF16)
    a_lo = (a - a_hi.astype(F32)).astype(BF16)
    b_hi = b.astype(BF16)
    b_lo = (b - b_hi.astype(F32)).astype(BF16)
    return (jnp.dot(a_hi, b_hi, preferred_element_type=F32) + jnp.dot(a_lo, b_hi, preferred_element_type=F32)
            + jnp.dot(a_hi, b_lo, preferred_element_type=F32))


def _dot_state(m, h):
    return _dot(m, h)


def _sigmoid(x):
    return 1.0 / (1.0 + jnp.exp(-x))


def _const_spec(shape):
    nd = len(shape)
    return pl.BlockSpec(shape, lambda *_: (0,) * nd)


def _row_spec(tm, width):
    return pl.BlockSpec((tm, width), lambda i: (i, 0))


def _mixer_in_kernel(*refs, tiles_per_seq):
    (x_ref, g_ref, wa_ref, wb_ref, wc_ref, mu_ref, cos_ref, sin_ref) = refs[:8]
    mla_consts = refs[8:15]
    gmlp_consts = refs[15:20]
    za_ref, q_o, k_o, v_o, yc_ref, carry_ref = refs[20:]
    x = x_ref[...]
    ms = jnp.mean(x * x, axis=-1, keepdims=True)
    h = (x * lax.rsqrt(ms + EPS) * g_ref[...]).astype(BF16)
    zb = jnp.dot(h, wb_ref[...], preferred_element_type=F32)
    zc = jnp.dot(h, wc_ref[...], preferred_element_type=F32)
    za = jnp.dot(h, wa_ref[...], preferred_element_type=F32)
    _mla_pre_compute(zb, cos_ref[...], sin_ref[...], *mla_consts, q_o, k_o, v_o)
    _gmlp_compute(zc, *gmlp_consts, yc_ref)
    tm = za.shape[0]
    first = jnp.where(pl.program_id(0) % tiles_per_seq == 0, 0.0, carry_ref[0:1, :])
    row = lax.broadcasted_iota(jnp.int32, za.shape, 0)
    prev = jnp.where(row == 0, first, pltpu.roll(za, 1, 0))
    carry_ref[0:1, :] = za[tm - 1:tm, :]
    za_ref[...] = za + (prev - za) * mu_ref[...]


def _mixer_in(x, g, w_in, mu, seq_len, cos_t, sin_t, mla_consts, gmlp_consts, tm=512):
    n = x.shape[0]
    wa = w_in[:, :A_COLS].astype(BF16)
    wb = jnp.pad(w_in[:, A_COLS:A_COLS + B_COLS], ((0, 0), (0, 4 * LANES - B_COLS))).astype(BF16)
    wc = w_in[:, A_COLS + B_COLS:].astype(BF16)
    consts = [g.reshape(1, -1), wa, wb, wc, mu.reshape(1, -1)]
    width = B_HEADS * B_HEAD_PAD
    return pl.pallas_call(
        functools.partial(_mixer_in_kernel, tiles_per_seq=seq_len // tm),
        out_shape=(jax.ShapeDtypeStruct((n, A_COLS), F32),
                   jax.ShapeDtypeStruct((n, width), BF16), jax.ShapeDtypeStruct((n, width), BF16),
                   jax.ShapeDtypeStruct((n, B_WIDTH), BF16), jax.ShapeDtypeStruct((n, C_WIDTH), F32)),
        grid=(n // tm,),
        in_specs=[_row_spec(tm, D_MODEL)] + [_const_spec(c.shape) for c in consts]
                 + [_row_spec(tm, LANES), _row_spec(tm, LANES)]
                 + [_const_spec(c.shape) for c in mla_consts + gmlp_consts],
        out_specs=(_row_spec(tm, A_COLS), _row_spec(tm, width), _row_spec(tm, width),
                   _row_spec(tm, B_WIDTH), _row_spec(tm, C_WIDTH)),
        scratch_shapes=[pltpu.VMEM((SUBLANES, A_COLS), F32)],
        compiler_params=_cparams(("arbitrary",)),
        name="mixer_in",
    )(x, *consts, cos_t, sin_t, *mla_consts, *gmlp_consts)


def _rwkv_pre_compute(za, w0_ref, wup_ref, a0_ref, aup_ref, gup_ref, kk_ref, ka_ref, rk_ref, bd_ref, vres):
    r = za[:, 0:A_WIDTH]
    k = za[:, A_WIDTH:2 * A_WIDTH]
    v = za[:, 2 * A_WIDTH:3 * A_WIDTH]
    lora = za[:, 3 * A_WIDTH:A_COLS]
    w = w0_ref[...] + _dot3(jnp.tanh(lora), wup_ref[...])
    w_log = -(jnp.maximum(-w, 0.0) + jnp.log(1.0 + jnp.exp(-jnp.abs(w)))) - 0.5
    lw = -jnp.exp(w_log)
    a = _sigmoid(a0_ref[...] + _dot3(lora, aup_ref[...]))
    g = _dot(_sigmoid(lora), gup_ref[...])
    if vres is not None:
        vf, v0_ref, vdn_ref, vup_ref = vres
        mix = _sigmoid(v0_ref[...] + _dot3(_dot3(v, vdn_ref[...]), vup_ref[...]))
        v = v + (vf - v) * mix
    kk = k * kk_ref[...]
    ss = _dot_split(kk * kk, bd_ref[...])
    kk = kk / jnp.maximum(jnp.sqrt(ss), 1e-12)
    k = k * (1.0 + (a - 1.0) * ka_ref[...])
    bonus = _dot_split(r * k * rk_ref[...], bd_ref[...]) * v
    return r, k, v, kk, a, lw, g, bonus


def _head_block_diag(width, head_dim):
    idx = jnp.arange(width) // head_dim
    return (idx[:, None] == idx[None, :]).astype(F32)


def _pad_rows(w, row_off, rows=LANES):
    return jnp.pad(w, ((row_off, rows - row_off - w.shape[0]), (0, 0)))


def _rwkv_consts(w0, w_up, a0, a_up, g_up, k_k, k_a, r_k):
    row = lambda t: t.reshape(1, -1)
    return [row(w0), _pad_rows(w_up, 0), row(a0), _pad_rows(a_up, A_W_LORA),
            _pad_rows(g_up, A_W_LORA + A_A_LORA), row(k_k), row(k_a), row(r_k),
            _head_block_diag(A_WIDTH, A_HEAD_DIM)]


def _rwkv_vres_consts(v0, v_down, v_up):
    return [v0.reshape(1, -1), jnp.pad(v_down, ((0, 0), (0, LANES - v_down.shape[1]))), _pad_rows(v_up, 0)]


def _stack_heads(x, lane_lo):
    return jnp.concatenate([jnp.where(lane_lo, x, 0.0), jnp.where(lane_lo, 0.0, x)], axis=0)


def _rwkv_mix_kernel(*refs, has_vres):
    za_ref = refs[0]
    pre_consts = refs[1:10]
    bd_ref = pre_consts[-1]
    if has_vres:
        vf_ref, vres_consts = refs[10], refs[11:14]
        lng_ref, lnb_ref, y_ref, v_out, h_ref = refs[14:]
    else:
        lng_ref, lnb_ref, y_ref, v_out, h_ref = refs[10:]
    n_batch, rows_per_step, _ = za_ref.shape
    L = SCAN_CHUNK
    n_chunks = rows_per_step // L
    S2 = 2 * L

    @pl.when(pl.program_id(0) == 0)
    def _():
        h_ref[...] = jnp.zeros_like(h_ref)

    ti = lax.broadcasted_iota(jnp.int32, (L, L), 0)
    si = lax.broadcasted_iota(jnp.int32, (L, L), 1)
    tril_incl = (ti >= si).astype(F32)
    row_w = lax.broadcasted_iota(jnp.int32, (L, S2), 0)
    col_w = lax.broadcasted_iota(jnp.int32, (L, S2), 1) % L
    strict = row_w > col_w
    incl = row_w >= col_w
    eye_w = row_w == col_w
    eye = (lax.broadcasted_iota(jnp.int32, (PAIR, PAIR), 0)
           == lax.broadcasted_iota(jnp.int32, (PAIR, PAIR), 1))
    lane_lo = lax.broadcasted_iota(jnp.int32, (1, PAIR), 1) < A_HEAD_DIM
    zeros_s = jnp.zeros((S2, PAIR), F32)
    mid = L // 2 - 1

    chains = []
    pre = []
    for b in range(n_batch):
        vres = (vf_ref[b], *vres_consts) if has_vres else None
        pre.append(_rwkv_pre_compute(za_ref[b], *pre_consts, vres))
        v_out[b] = pre[b][2]
    for b, ck in [(b, ck) for b in range(n_batch) for ck in range(n_chunks)]:
        rows = slice(ck * L, (ck + 1) * L)
        r_b, k_b, v_b, kk_b, a_b, lw_b = (t[rows] for t in pre[b][:6])
        lw = lw_b
        cum = _dot_split(tril_incl, lw, split_lhs=False)
        c_mid = cum[mid:mid + 1, :]
        c_end = cum[L - 1:L, :]
        kk = kk_b
        kb = kk * a_b
        kmod = k_b
        e_dn = jnp.exp(c_mid - cum)
        e_hat = jnp.exp(c_end - cum)
        at_all = -kk * jnp.exp(cum - lw - c_mid)
        bt_all = kb * e_dn
        kt_all = kmod * e_dn
        rt_all = r_b * jnp.exp(cum - c_mid)
        bh_all = kb * e_hat
        kh_all = kmod * e_hat
        p_end = jnp.exp(c_end)
        p_mid = jnp.exp(c_mid)
        at_true = at_all * p_mid
        rt_true = rt_all * p_mid
        v_all = v_b
        for p in range(N_PAIRS):
            sl = slice(p * PAIR, (p + 1) * PAIR)
            stack = lambda t: _stack_heads(t[:, sl], lane_lo)
            chains.append(dict(
                b=b, p=p, ck=ck, ar=jnp.concatenate([at_all[:, sl], rt_all[:, sl]], axis=0),
                bk_s=jnp.concatenate([stack(bt_all), stack(kt_all)], axis=0),
                bkh_t=jnp.concatenate([stack(bh_all).T, stack(kh_all).T], axis=1),
                v_s=stack(v_all), a_in_s=stack(at_true), r_in=rt_true[:, sl], p_end=p_end[:, sl]))
    stack = lambda t: _stack_heads(t, lane_lo)
    for c in chains:
        gram = _dot_nt(c["ar"], c["bk_s"])
        a_ab = jnp.where(strict, gram[:L, :S2], 0.0)
        c["a_ak"] = jnp.where(strict, gram[:L, S2:], 0.0)
        c["a_r"] = jnp.where(jnp.concatenate([incl, incl], axis=1), gram[L:], 0.0)
        c["x"] = a_ab
        c["t"] = jnp.where(eye_w, 1.0, a_ab)
    n_doublings = int(math.log2(L)) - 1
    for c in chains:
        c["x"] = _dot(c["x"], stack(c["x"]))
    for it in range(n_doublings):
        for c in chains:
            if it + 1 < n_doublings:
                both = _dot(c["x"], jnp.concatenate([stack(c["t"]), stack(c["x"])], axis=1))
                c["t"] = c["t"] + both[:, :S2]
                c["x"] = both[:, S2:]
            else:
                c["t"] = c["t"] + _dot(c["x"], stack(c["t"]))
    for c in chains:
        c["av"] = _dot(c["a_ak"], c["v_s"])
    for c in chains:
        tw = _dot(c["t"], jnp.concatenate([c["a_in_s"], stack(c["av"])], axis=1))
        c["rhs"] = jnp.concatenate(
            [jnp.concatenate([stack(tw[:, :PAIR]), stack(tw[:, PAIR:])], axis=1),
             jnp.concatenate([zeros_s, c["v_s"]], axis=1)], axis=0)
    for c in chains:
        out = _dot(jnp.concatenate([c["a_r"], c["bkh_t"]], axis=0), c["rhs"])
        c["q"] = c["r_in"] + out[:L, :PAIR]
        c["y0"] = out[:L, PAIR:]
        c["m"] = jnp.where(eye, c["p_end"], 0.0) + out[L:, :PAIR]
        c["c"] = out[L:, PAIR:]
    state = {(b, p): h_ref[b, p] for b in range(n_batch) for p in range(N_PAIRS)}
    for ck in range(n_chunks):
        for c in [c for c in chains if c["ck"] == ck]:
            out = _dot_state(jnp.concatenate([c["q"], c["m"]], axis=0), state[c["b"], c["p"]])
            c["y"] = out[:L] + c["y0"]
            state[c["b"], c["p"]] = out[L:] + c["c"]
    for (b, p), h in state.items():
        h_ref[b, p] = h
    for b in range(n_batch):
        y = jnp.concatenate(
            [jnp.concatenate([c["y"] for c in chains if c["b"] == b and c["ck"] == ck], axis=1)
             for ck in range(n_chunks)], axis=0)
        mu = _dot_split(y, bd_ref[...]) * (1.0 / A_HEAD_DIM)
        d = y - mu
        var = _dot_split(d * d, bd_ref[...]) * (1.0 / A_HEAD_DIM)
        yn = d * lax.rsqrt(var + RWKV_GN_EPS) * lng_ref[...] + lnb_ref[...]
        y_ref[b] = (yn + pre[b][7]) * pre[b][6]


def _rwkv_mix(za, pre_consts, v_first, vres_consts, ln_g, ln_b, n_batch, seq_len, chunks_per_step=4):
    rows = SCAN_CHUNK * chunks_per_step
    has_vres = v_first is not None
    shp = (n_batch, seq_len, A_WIDTH)
    seq_spec = pl.BlockSpec((n_batch, rows, A_WIDTH), lambda i: (0, i, 0))
    args = [za.reshape(n_batch, seq_len, A_COLS)] + list(pre_consts)
    specs = [pl.BlockSpec((n_batch, rows, A_COLS), lambda i: (0, i, 0))] + [_const_spec(c.shape) for c in pre_consts]
    if has_vres:
        args += [v_first.reshape(shp)] + list(vres_consts)
        specs += [seq_spec] + [_const_spec(c.shape) for c in vres_consts]
    norm = [ln_g.reshape(1, -1), ln_b.reshape(1, -1)]
    y, v = pl.pallas_call(
        functools.partial(_rwkv_mix_kernel, has_vres=has_vres),
        out_shape=(jax.ShapeDtypeStruct(shp, F32), jax.ShapeDtypeStruct(shp, F32)),
        grid=(seq_len // rows,),
        in_specs=specs + [_const_spec(c.shape) for c in norm],
        out_specs=(seq_spec, seq_spec),
        scratch_shapes=[pltpu.VMEM((n_batch, N_PAIRS, PAIR, PAIR), F32)],
        compiler_params=_cparams(("arbitrary",)),
        name="rwkv_mix",
    )(*args, *norm)
    return y.reshape(n_batch * seq_len, A_WIDTH), v.reshape(n_batch * seq_len, A_WIDTH)


def _exchange_rope_halves(t):
    half = B_ROPE // 2
    nope = jnp.zeros_like(t[..., :B_NOPE])
    return jnp.concatenate([nope, t[..., B_NOPE + half:], t[..., B_NOPE:B_NOPE + half]], axis=-1)


def _mla_pre_compute(zb, cos_t, sin_t, gq_ref, wuq_ref, gkv_ref, wk_ref, wv_ref, gqh_ref, gkh_ref,
                     q_o, k_o, v_o):
    cq = zb[:, 0:B_Q_LORA]
    ckv = zb[:, B_Q_LORA:B_Q_LORA + B_KV_LORA]
    kr_raw = zb[:, B_Q_LORA + B_KV_LORA:]
    half = B_ROPE // 2
    lane = lax.broadcasted_iota(jnp.int32, kr_raw.shape, 1)
    kr = pltpu.roll(kr_raw, B_NOPE, 1)
    kr_sw = jnp.where(lane < B_NOPE + half, pltpu.roll(kr_raw, B_NOPE - half, 1),
                      pltpu.roll(kr_raw, B_NOPE + half, 1))
    qn = cq * lax.rsqrt(jnp.mean(cq * cq, axis=-1, keepdims=True) + EPS) * gq_ref[...]
    kvn = ckv * lax.rsqrt(jnp.mean(ckv * ckv, axis=-1, keepdims=True) + EPS) * gkv_ref[...]
    width = B_HEADS * B_HEAD_PAD
    q2 = _dot(qn, wuq_ref[...])
    k = _dot(kvn, wk_ref[...])
    v_o[...] = _dot(kvn, wv_ref[...]).astype(BF16)
    q_scale = LOG2_E / math.sqrt(B_QK)
    gq_cos = cos_t * (gqh_ref[0:1, :] * q_scale)
    gq_sin = sin_t * (gqh_ref[1:2, :] * q_scale)
    gk_cos = cos_t * gkh_ref[0:1, :]
    gk_sin = kr_sw * sin_t * gkh_ref[1:2, :]
    for h in range(B_HEADS):
        sl = slice(h * B_HEAD_PAD, (h + 1) * B_HEAD_PAD)
        qh = q2[:, sl]
        q_sw = q2[:, width + h * B_HEAD_PAD:width + (h + 1) * B_HEAD_PAD]
        kh = k[:, sl] + kr
        rs_q = lax.rsqrt(jnp.sum(qh * qh, axis=-1, keepdims=True) * (1.0 / B_QK) + EPS)
        rs_k = lax.rsqrt(jnp.sum(kh * kh, axis=-1, keepdims=True) * (1.0 / B_QK) + EPS)
        q_o[:, sl] = ((qh * gq_cos + q_sw * gq_sin) * rs_q).astype(BF16)
        k_o[:, sl] = ((kh * gk_cos + gk_sin) * rs_k).astype(BF16)


def _rope_tables(positions):
    half = B_ROPE // 2
    inv_freq = ROPE_THETA ** (-jnp.arange(0, B_ROPE, 2, dtype=F32) / B_ROPE)
    ang = positions.astype(F32).reshape(-1, 1) * inv_freq
    cos, sin = jnp.cos(ang), jnp.sin(ang)
    n = ang.shape[0]
    ones = jnp.ones((n, B_NOPE), F32)
    zeros = jnp.zeros((n, B_NOPE), F32)
    tail = jnp.zeros((n, B_HEAD_PAD - B_QK), F32)
    cos_t = jnp.concatenate([ones, cos, cos, tail], axis=1)
    sin_t = jnp.concatenate([zeros, -sin, sin, tail], axis=1)
    assert cos_t.shape[1] == B_HEAD_PAD and half * 2 == B_ROPE
    return cos_t, sin_t


def _pad_heads(w, heads, dim):
    kdim = w.shape[0]
    w = w.reshape(kdim, heads, dim)
    return jnp.pad(w, ((0, 0), (0, 0), (0, B_HEAD_PAD - dim))).reshape(kdim, heads * B_HEAD_PAD)


def _mla_pre_consts(q_norm_g, w_uq, kv_norm_g, w_ukv, q_head_g, k_head_g):
    w_uq_x = _exchange_rope_halves(w_uq.reshape(B_Q_LORA, B_HEADS, B_QK)).reshape(B_Q_LORA, -1)
    wuq = jnp.concatenate([_pad_heads(w_uq, B_HEADS, B_QK), _pad_heads(w_uq_x, B_HEADS, B_QK)],
                          axis=1).astype(BF16)
    w_ukv = w_ukv.reshape(B_KV_LORA, B_HEADS, B_NOPE + B_V)
    wk = _pad_heads(w_ukv[:, :, :B_NOPE].reshape(B_KV_LORA, -1), B_HEADS, B_NOPE).astype(BF16)
    wv = w_ukv[:, :, B_NOPE:].reshape(B_KV_LORA, B_WIDTH).astype(BF16)
    pad_g = lambda gg: jnp.pad(jnp.stack([gg, _exchange_rope_halves(gg)]), ((0, 0), (0, B_HEAD_PAD - B_QK)))
    return [q_norm_g.reshape(1, -1), wuq, kv_norm_g.reshape(1, -1), wk, wv,
            pad_g(q_head_g), pad_g(k_head_g)]


def _attn_kernel(qt_ref, kt_ref, q_ref, k_ref, v_ref, o_ref, m_ref, acc_ref):
    t = pl.program_id(2)
    i = qt_ref[t]
    j = kt_ref[t]
    tk = k_ref.shape[0]

    @pl.when(j == 0)
    def _():
        m_ref[...] = jnp.full_like(m_ref, NEG_BIG)
        acc_ref[...] = jnp.zeros_like(acc_ref)

    def update(rows, cols, masked):
        v = v_ref[cols, :]
        n_cols = v.shape[0]
        lane_v = lax.broadcasted_iota(jnp.int32, v.shape, 1)
        one = jnp.ones_like(v)
        for h in range(2):
            sl = slice(h * B_HEAD_PAD, (h + 1) * B_HEAD_PAD)
            s = lax.dot_general(q_ref[rows, sl], k_ref[cols, sl], (((1,), (1,)), ((), ())),
                                preferred_element_type=F32)
            if masked:
                qc = lax.broadcasted_iota(jnp.int32, s.shape, 0) // CHUNK
                kc = lax.broadcasted_iota(jnp.int32, s.shape, 1) // CHUNK
                s = jnp.where(kc <= qc, s, NEG_BIG)
            m_prev = m_ref[h, rows, :]
            m_new = jnp.maximum(m_prev, jnp.max(s, axis=-1, keepdims=True))
            alpha = jnp.exp2(m_prev - m_new)
            p = jnp.exp2((s - jnp.concatenate([m_new] * (n_cols // LANES), axis=1)).astype(BF16))
            vh = jnp.where((lane_v < B_V) == (h == 0), v, one)
            acc_ref[h, rows, :] = alpha * acc_ref[h, rows, :] + jnp.dot(p, vh, preferred_element_type=F32)
            m_ref[h, rows, :] = m_new

    @pl.when(j < i)
    def _():
        update(slice(0, tk), slice(0, tk), False)

    @pl.when(j == i)
    def _():
        update(slice(0, tk), slice(0, tk), True)
        lane = lax.broadcasted_iota(jnp.int32, o_ref.shape, 1)
        a0 = acc_ref[0]
        a1 = acc_ref[1]
        o_ref[...] = jnp.where(lane < B_V, a0 / pltpu.roll(a0, B_V, 1), a1 / pltpu.roll(a1, B_V, 1))


def _attention(q, k, v, n_batch, seq_len, tile=1024):
    tile = min(tile, seq_len)
    nt = seq_len // tile
    width = B_HEADS * B_HEAD_PAD
    q = q.reshape(n_batch, seq_len, width)
    k = k.reshape(n_batch, seq_len, width)
    v = v.reshape(n_batch, seq_len, B_WIDTH)
    q_tiles = jnp.asarray([i for i in range(nt) for _ in range(i + 1)], jnp.int32)
    k_tiles = jnp.asarray([j for i in range(nt) for j in range(i + 1)], jnp.int32)
    o = pl.pallas_call(
        _attn_kernel,
        out_shape=jax.ShapeDtypeStruct((n_batch, seq_len, B_WIDTH), F32),
        grid_spec=pltpu.PrefetchScalarGridSpec(
            num_scalar_prefetch=2,
            grid=(n_batch, B_HEADS // 2, nt * (nt + 1) // 2),
            in_specs=[pl.BlockSpec((None, tile, 2 * B_HEAD_PAD), lambda b, p, t, qt, kt: (b, qt[t], p)),
                      pl.BlockSpec((None, tile, 2 * B_HEAD_PAD), lambda b, p, t, qt, kt: (b, kt[t], p)),
                      pl.BlockSpec((None, tile, 2 * B_V), lambda b, p, t, qt, kt: (b, kt[t], p))],
            out_specs=pl.BlockSpec((None, tile, 2 * B_V), lambda b, p, t, qt, kt: (b, qt[t], p)),
            scratch_shapes=[pltpu.VMEM((2, tile, LANES), F32), pltpu.VMEM((2, tile, 2 * B_V), F32)]),
        compiler_params=_cparams(("parallel", "parallel", "arbitrary")),
        name="mla_attention",
    )(q_tiles, k_tiles, q, k, v)
    return o.reshape(n_batch * seq_len, B_WIDTH)


def _gmlp_compute(z, lng_ref, lnb_ref, ws_ref, bias_ref, og_ref, y_ref):
    z = 0.5 * z * (1.0 + lax.erf(z * (1.0 / math.sqrt(2.0))))
    u = z[:, :C_WIDTH]
    v = z[:, C_WIDTH:]
    mu = jnp.mean(v, axis=-1, keepdims=True)
    d = v - mu
    var = jnp.mean(d * d, axis=-1, keepdims=True)
    v = d * lax.rsqrt(var + EPS) * lng_ref[...] + lnb_ref[...]
    ti = lax.broadcasted_iota(jnp.int32, (C_BLOCK, C_BLOCK), 0)
    si = lax.broadcasted_iota(jnp.int32, (C_BLOCK, C_BLOCK), 1)
    group = lax.broadcasted_iota(jnp.int32, (1, C_WIDTH), 1) // C_GROUP_DIM
    ws = [jnp.where(ti >= si, ws_ref[gi], 0.0).astype(BF16) for gi in range(C_GROUPS)]
    blocks = []
    for blk in range(v.shape[0] // C_BLOCK):
        vb = v[blk * C_BLOCK:(blk + 1) * C_BLOCK]
        sv = bias_ref[...]
        for gi in range(C_GROUPS):
            sv = sv + jnp.dot(ws[gi], jnp.where(group == gi, vb, 0.0).astype(BF16),
                              preferred_element_type=F32)
        blocks.append(sv)
    sv = jnp.concatenate(blocks, axis=0) if len(blocks) > 1 else blocks[0]
    y = u * sv
    y_ref[...] = y * lax.rsqrt(jnp.mean(y * y, axis=-1, keepdims=True) + EPS) * og_ref[...]


def _gmlp_consts(ln_g, ln_b, w_s, b_s, out_g):
    bias = jnp.repeat(jnp.transpose(b_s), C_GROUP_DIM, axis=1)
    return [ln_g.reshape(1, -1), ln_b.reshape(1, -1), w_s, bias, out_g.reshape(1, -1)]


def _mix_out_compute(mix_refs, rows=slice(None)):
    x_ref, ya_ref, o_ref, yc_ref, bg_ref, wa_ref, wb_ref, wc_ref, fg_ref = mix_refs
    o = o_ref[rows, :]
    yb = o * lax.rsqrt(jnp.mean(o * o, axis=-1, keepdims=True) + EPS) * bg_ref[...]
    x = (x_ref[rows, :] + _dot(ya_ref[rows, :], wa_ref[...]) + _dot(yb, wb_ref[...])
         + _dot(yc_ref[rows, :], wc_ref[...]))
    h = x * lax.rsqrt(jnp.mean(x * x, axis=-1, keepdims=True) + EPS) * fg_ref[...]
    return x, h


def _mix_out_args(x, ya, o, yc, b_out_g, w_out, ffn_g, tm):
    wa = w_out[:A_WIDTH].astype(BF16)
    wb = w_out[A_WIDTH:A_WIDTH + B_WIDTH].astype(BF16)
    wc = w_out[A_WIDTH + B_WIDTH:].astype(BF16)
    consts = [b_out_g.reshape(1, -1), wa, wb, wc, ffn_g.reshape(1, -1)]
    specs = ([_row_spec(tm, D_MODEL), _row_spec(tm, A_WIDTH), _row_spec(tm, B_WIDTH), _row_spec(tm, C_WIDTH)]
             + [_const_spec(c.shape) for c in consts])
    return [x, ya, o, yc] + consts, specs


def _ff_chunks(d_ff, chunk):
    return [(c0, min(chunk, d_ff - c0)) for c0 in range(0, d_ff, chunk)]


class _WeightStream:
    def __init__(self, hbm, resident, stage_cols, stage_rows, sem, chunk):
        self.hbm, self.resident = hbm, resident
        self.stage_cols, self.stage_rows, self.sem = stage_cols, stage_rows, sem
        self.chunks = _ff_chunks(resident[0].shape[1], chunk)

    def _copies(self, ci):
        (c0, cn), slot = self.chunks[ci], ci % 2
        wg, wu, wd = self.hbm
        return (pltpu.make_async_copy(wg.at[:, pl.ds(c0, cn)], self.stage_cols.at[slot, 0, :, pl.ds(0, cn)],
                                      self.sem.at[slot]),
                pltpu.make_async_copy(wu.at[:, pl.ds(c0, cn)], self.stage_cols.at[slot, 1, :, pl.ds(0, cn)],
                                      self.sem.at[slot]),
                pltpu.make_async_copy(wd.at[pl.ds(c0, cn)], self.stage_rows.at[slot, pl.ds(0, cn)],
                                      self.sem.at[slot]))

    def start(self, ci):
        for c in self._copies(ci):
            c.start()

    def finish(self, ci):
        for c in self._copies(ci):
            c.wait()
        (c0, cn), slot = self.chunks[ci], ci % 2
        wg_ref, wu_ref, wd_ref = self.resident
        wg_ref[:, c0:c0 + cn] = self.stage_cols[slot, 0, :, 0:cn].astype(BF16)
        wu_ref[:, c0:c0 + cn] = self.stage_cols[slot, 1, :, 0:cn].astype(BF16)
        wd_ref[c0:c0 + cn, :] = self.stage_rows[slot, 0:cn, :].astype(BF16)

    def load_all(self):
        self.start(0)
        for ci in range(len(self.chunks)):
            if ci + 1 < len(self.chunks):
                self.start(ci + 1)
            self.finish(ci)


SWIGLU_DOTS = 3


def _swiglu_resident(h, acc, wg_ref, wu_ref, wd_ref, chunk, between=None, after=None):
    between = between or (lambda ci, k: None)
    for ci, (c0, cn) in enumerate(_ff_chunks(wg_ref.shape[1], chunk)):
        between(ci, 0)
        gate = jnp.dot(h, wg_ref[:, c0:c0 + cn], preferred_element_type=F32)
        between(ci, 1)
        up = jnp.dot(h, wu_ref[:, c0:c0 + cn], preferred_element_type=F32)
        act = (gate * _sigmoid(gate) * up).astype(BF16)
        between(ci, 2)
        acc = acc + jnp.dot(act, wd_ref[c0:c0 + cn, :], preferred_element_type=F32)
        if after is not None:
            after(ci)
    return acc


def _ffn_kernel(h_ref, x_ref, wg_hbm, wu_hbm, wd_hbm, o_ref, wg_ref, wu_ref, wd_ref, stage_cols, stage_rows,
                sem, *, ff_chunk):
    @pl.when(pl.program_id(0) == 0)
    def _():
        _WeightStream((wg_hbm, wu_hbm, wd_hbm), (wg_ref, wu_ref, wd_ref), stage_cols, stage_rows, sem,
                      ff_chunk).load_all()

    o_ref[...] = _swiglu_resident(h_ref[...], x_ref[...], wg_ref, wu_ref, wd_ref, ff_chunk)


def _weight_scratch(d_ff, ff_chunk):
    return [pltpu.VMEM((D_MODEL, d_ff), BF16), pltpu.VMEM((D_MODEL, d_ff), BF16),
            pltpu.VMEM((d_ff, D_MODEL), BF16),
            pltpu.VMEM((2, 2, D_MODEL, ff_chunk), F32), pltpu.VMEM((2, ff_chunk, D_MODEL), F32)]


def _dense_ffn(x, h, wg, wu, wd, tm=512, ff_chunk=512):
    n = x.shape[0]
    tm = min(tm, n)
    d_ff = wg.shape[1]
    any_spec = pl.BlockSpec(memory_space=pl.ANY)
    return pl.pallas_call(
        functools.partial(_ffn_kernel, ff_chunk=ff_chunk),
        out_shape=jax.ShapeDtypeStruct((n, D_MODEL), F32),
        grid=(n // tm,),
        in_specs=[_row_spec(tm, D_MODEL), _row_spec(tm, D_MODEL), any_spec, any_spec, any_spec],
        out_specs=_row_spec(tm, D_MODEL),
        scratch_shapes=_weight_scratch(d_ff, ff_chunk) + [pltpu.SemaphoreType.DMA((2,))],
        compiler_params=_cparams(("arbitrary",)),
        name="dense_ffn",
    )(h, x, wg, wu, wd)


def _mix_out_kernel(*refs):
    x, h = _mix_out_compute(refs[:9])
    refs[9][...] = x
    if len(refs) > 10:
        refs[10][...] = h.astype(BF16)


def _mix_out(x, ya, o, yc, b_out_g, w_out, ffn_g, with_normed, tm=512):
    n = x.shape[0]
    tm = min(tm, n)
    args, specs = _mix_out_args(x, ya, o, yc, b_out_g, w_out, ffn_g, tm)
    shapes = [jax.ShapeDtypeStruct((n, D_MODEL), F32)] + [jax.ShapeDtypeStruct((n, D_MODEL), BF16)] * with_normed
    return pl.pallas_call(
        _mix_out_kernel,
        out_shape=tuple(shapes),
        grid=(n // tm,),
        in_specs=specs,
        out_specs=tuple(_row_spec(tm, D_MODEL) for _ in shapes),
        compiler_params=_cparams(("parallel",)),
        name="mix_out",
    )(*args)


def _router_kernel(x_ref, fg_ref, wr_ref, route_ref):
    x = x_ref[...]
    h = x * lax.rsqrt(jnp.mean(x * x, axis=-1, keepdims=True) + EPS) * fg_ref[...]
    logits = _dot3(h, wr_ref[...])
    lane = lax.broadcasted_iota(jnp.int32, logits.shape, 1)
    logits = jnp.where(lane < N_EXPERTS, logits, -jnp.inf)
    v1 = jnp.max(logits, axis=-1, keepdims=True)
    i1 = jnp.min(jnp.where(logits == v1, lane, LANES), axis=-1, keepdims=True)
    rest = jnp.where(lane == i1, -jnp.inf, logits)
    v2 = jnp.max(rest, axis=-1, keepdims=True)
    i2 = jnp.min(jnp.where(rest == v2, lane, LANES), axis=-1, keepdims=True)
    e2 = jnp.exp(v2 - v1)
    g1 = 1.0 / (1.0 + e2)
    g2 = e2 / (1.0 + e2)
    route_ref[...] = jnp.where(lane == 0, i1.astype(F32),
                               jnp.where(lane == 1, i2.astype(F32),
                                         jnp.where(lane == 2, g1, jnp.where(lane == 3, g2, 0.0))))


def _router(x, ffn_g, w_router, tm=512):
    n = x.shape[0]
    tm = min(tm, n)
    wr = jnp.pad(w_router, ((0, 0), (0, LANES - N_EXPERTS)))
    return pl.pallas_call(
        _router_kernel,
        out_shape=jax.ShapeDtypeStruct((n, LANES), F32),
        grid=(n // tm,),
        in_specs=[_row_spec(tm, D_MODEL), _const_spec((1, D_MODEL)), _const_spec(wr.shape)],
        out_specs=_row_spec(tm, LANES),
        compiler_params=_cparams(("parallel",)),
        name="moe_router",
    )(x, ffn_g.reshape(1, -1), wr)


def _moe_plan(route, tm):
    n = route.shape[0]
    experts = jnp.concatenate([route[:, 0], route[:, 1]]).astype(jnp.int32)
    onehot = (experts[:, None] == jnp.arange(N_EXPERTS, dtype=jnp.int32)[None, :]).astype(jnp.int32)
    csum = jnp.cumsum(onehot, axis=0)
    rank = jnp.sum(csum * onehot, axis=1) - 1
    counts = csum[-1]
    padded = (counts + tm - 1) // tm * tm
    ends = jnp.cumsum(padded)
    pos = (ends - padded)[experts] + rank
    n_rows = 2 * n + N_EXPERTS * tm
    pair = jnp.full((n_rows,), -1, jnp.int32).at[pos].set(
        jnp.arange(2 * n, dtype=jnp.int32), unique_indices=True, mode="promise_in_bounds")
    is_pad = pair < 0
    src = jnp.where(is_pad, 0, pair % n)
    spare = 2 * n + tm + jnp.cumsum(is_pad.astype(jnp.int32)) - 1
    dst = jnp.concatenate([2 * n + jnp.arange(tm, dtype=jnp.int32), jnp.where(is_pad, spare, pair)])
    tile_start = jnp.arange(n_rows // tm, dtype=jnp.int32) * tm
    tile_expert = jnp.minimum(jnp.sum(tile_start[:, None] >= ends[None, :], axis=1), N_EXPERTS - 1)
    n_used = (ends[-1] // tm).reshape(1)
    return (src.astype(jnp.int32), dst.astype(jnp.int32), tile_expert.astype(jnp.int32),
            n_used.astype(jnp.int32))


def _moe_expert_kernel(te_ref, src_ref, dst_ref, nu_ref, x_hbm, fg_ref, wg_hbm, wu_hbm, wd_hbm, y_hbm,
                       rows_ref, ybuf_ref, wg_ref, wu_ref, wd_ref, stage_cols, stage_rows,
                       row_sem, out_sem, w_sem, *, ff_chunk):
    t = pl.program_id(0)
    n_tiles = pl.num_programs(0) - 1
    tm = rows_ref.shape[1]
    n_used = nu_ref[0]
    slot = t % 2
    e = te_ref[jnp.minimum(t, n_tiles - 1)]
    e_next = te_ref[jnp.minimum(t + 1, n_tiles - 1)]
    hand_over = (t + 1 < n_used) & (e_next != e)

    def stream(expert):
        return _WeightStream((wg_hbm.at[expert], wu_hbm.at[expert], wd_hbm.at[expert]),
                             (wg_ref, wu_ref, wd_ref), stage_cols, stage_rows, w_sem, ff_chunk)

    def row_copy(tile, s, r):
        return pltpu.make_async_copy(x_hbm.at[pl.ds(src_ref[tile * tm + r], 1)],
                                     rows_ref.at[s, pl.ds(r, 1)], row_sem.at[s])

    def wait_rows(s):
        pltpu.make_async_copy(x_hbm.at[pl.ds(0, tm)], rows_ref.at[s], row_sem.at[s]).wait()

    def out_copy(r):
        return pltpu.make_async_copy(ybuf_ref.at[1 - slot, pl.ds(r, 1)],
                                     y_hbm.at[pl.ds(dst_ref[t * tm + r], 1)], out_sem.at[1 - slot])

    def wait_out(s):
        pltpu.make_async_copy(ybuf_ref.at[s], y_hbm.at[pl.ds(0, tm)], out_sem.at[s]).wait()

    @pl.when(t == 0)
    def _():
        ybuf_ref[1] = jnp.zeros((tm, D_MODEL), F32)

        def body(r, c):
            row_copy(0, 0, r).start()
            return c
        lax.fori_loop(0, tm, body, 0, unroll=8)
        stream(e).load_all()

    @pl.when(t < n_used)
    def _():
        wait_rows(slot)
        x = rows_ref[slot]
        h = (x * lax.rsqrt(jnp.mean(x * x, axis=-1, keepdims=True) + EPS) * fg_ref[...]).astype(BF16)
        n_chunks = len(_ff_chunks(wg_ref.shape[1], ff_chunk))
        rows_per_dot = -(-tm // (n_chunks * SWIGLU_DOTS))
        nxt = stream(e_next)

        def before_chunk(ci, k):
            first = (ci * SWIGLU_DOTS + k) * rows_per_dot
            for r in range(first, min(first + rows_per_dot, tm)):
                row_copy(t + 1, 1 - slot, r).start()
                out_copy(r).start()
            if ci == 0 and k == 0:
                pl.when(hand_over)(lambda: nxt.start(0))

        def after_chunk(ci):
            @pl.when(hand_over)
            def _():
                if ci + 1 < n_chunks:
                    nxt.start(ci + 1)
                nxt.finish(ci)

        acc = _swiglu_resident(h, jnp.zeros((tm, D_MODEL), F32), wg_ref, wu_ref, wd_ref, ff_chunk,
                               between=before_chunk, after=after_chunk)

        @pl.when(t >= 1)
        def _():
            wait_out(slot)

        ybuf_ref[slot] = acc

    @pl.when(t >= n_used)
    def _():
        @pl.when(t == n_used)
        def _():
            wait_rows(slot)

        def body(r, c):
            out_copy(r).start()
            return c
        lax.fori_loop(0, tm, body, 0, unroll=8)
        wait_out(slot)
        ybuf_ref[slot] = jnp.zeros((tm, D_MODEL), F32)

        @pl.when(t == n_tiles)
        def _():
            wait_out(1 - slot)


def _moe_experts(x, src, dst, tile_expert, n_used, ffn_g, wg, wu, wd, tm, ff_chunk=512):
    n_rows = src.shape[0]
    n_tokens = x.shape[0]
    src = jnp.pad(src, (0, tm))
    d_ff = wg.shape[2]
    any_spec = pl.BlockSpec(memory_space=pl.ANY)
    return pl.pallas_call(
        functools.partial(_moe_expert_kernel, ff_chunk=min(ff_chunk, d_ff)),
        out_shape=jax.ShapeDtypeStruct((2 * n_tokens + tm + N_EXPERTS * tm, D_MODEL), F32),
        grid_spec=pltpu.PrefetchScalarGridSpec(
            num_scalar_prefetch=4,
            grid=(n_rows // tm + 1,),
            in_specs=[any_spec, pl.BlockSpec((1, D_MODEL), lambda t, *_: (0, 0)),
                      any_spec, any_spec, any_spec],
            out_specs=any_spec,
            scratch_shapes=[pltpu.VMEM((2, tm, D_MODEL), F32), pltpu.VMEM((2, tm, D_MODEL), F32)]
                           + _weight_scratch(d_ff, min(ff_chunk, d_ff))
                           + [pltpu.SemaphoreType.DMA((2,)), pltpu.SemaphoreType.DMA((2,)),
                              pltpu.SemaphoreType.DMA((2,))]),
        compiler_params=_cparams(("arbitrary",)),
        name="moe_experts",
    )(tile_expert, src, dst, n_used, x, ffn_g.reshape(1, -1), wg, wu, wd)


def _moe_combine_kernel(x_ref, route_ref, y0_ref, y1_ref, o_ref):
    route = route_ref[...]
    lane = lax.broadcasted_iota(jnp.int32, route.shape, 1)
    g1 = jnp.sum(jnp.where(lane == 2, route, 0.0), axis=-1, keepdims=True)
    g2 = jnp.sum(jnp.where(lane == 3, route, 0.0), axis=-1, keepdims=True)
    o_ref[...] = x_ref[...] + g1 * y0_ref[...] + g2 * y1_ref[...]


def _moe_combine(x, route, y, tm=512):
    n = x.shape[0]
    tm = min(tm, n)
    return pl.pallas_call(
        _moe_combine_kernel,
        out_shape=jax.ShapeDtypeStruct((n, D_MODEL), F32),
        grid=(n // tm,),
        in_specs=[_row_spec(tm, D_MODEL), _row_spec(tm, LANES), _row_spec(tm, D_MODEL),
                  pl.BlockSpec((tm, D_MODEL), lambda i: (i + n // tm, 0))],
        out_specs=_row_spec(tm, D_MODEL),
        compiler_params=_cparams(("parallel",)),
        name="moe_combine",
    )(x, route, y, y)


def _moe(x, route, ffn_g, wg, wu, wd, tm=512):
    tm = min(tm, x.shape[0])
    src, dst, tile_expert, n_used = _moe_plan(route, tm)
    y = _moe_experts(x, src, dst, tile_expert, n_used, ffn_g, wg, wu, wd, tm)
    return _moe_combine(x, route, y)


def kernel(x, positions, mix_norm_g, w_in, shift_mu, a_w0, a_w_up, a_a0, a_a_up, a_g_up, a_k_k, a_k_a,
           a_r_k, a_ln_g, a_ln_b, a_v0, a_v_down, a_v_up, b_q_norm_g, b_w_uq, b_kv_norm_g, b_w_ukv,
           b_q_head_g, b_k_head_g, b_out_g, c_ln_g, c_ln_b, c_w_s, c_b_s, c_out_g, w_out, ffn_norm_g,
           dense_w_gate, dense_w_up, dense_w_down, moe_router, moe_w_gate, moe_w_up, moe_w_down):
    n_batch, seq_len, d_model = x.shape
    depth = w_in.shape[0]
    xf = x.reshape(n_batch * seq_len, d_model)
    cos_t, sin_t = _rope_tables(positions)
    v_first = None
    for l in range(depth):
        za, q, kq, vq, yc = _mixer_in(
            xf, mix_norm_g[l], w_in[l], shift_mu[l], seq_len, cos_t, sin_t,
            _mla_pre_consts(b_q_norm_g[l], b_w_uq[l], b_kv_norm_g[l], b_w_ukv[l], b_q_head_g[l], b_k_head_g[l]),
            _gmlp_consts(c_ln_g[l], c_ln_b[l], c_w_s[l], c_b_s[l], c_out_g[l]))
        vres = _rwkv_vres_consts(a_v0[l - 1], a_v_down[l - 1], a_v_up[l - 1]) if l > 0 else None
        ya, v = _rwkv_mix(
            za, _rwkv_consts(a_w0[l], a_w_up[l], a_a0[l], a_a_up[l], a_g_up[l], a_k_k[l], a_k_a[l],
                             a_r_k[l].reshape(-1)),
            v_first if l > 0 else None, vres, a_ln_g[l], a_ln_b[l], n_batch, seq_len)
        if l == 0:
            v_first = v
        o = _attention(q, kq, vq, n_batch, seq_len)
        i = l // 2
        if l % 2 == 0:
            xf, h = _mix_out(xf, ya, o, yc, b_out_g[l], w_out[l], ffn_norm_g[l], with_normed=True)
            xf = _dense_ffn(xf, h, dense_w_gate[i], dense_w_up[i], dense_w_down[i])
        else:
            xf, = _mix_out(xf, ya, o, yc, b_out_g[l], w_out[l], ffn_norm_g[l], with_normed=False)
            route = _router(xf, ffn_norm_g[l], moe_router[i])
            xf = _moe(xf, route, ffn_norm_g[l], moe_w_gate[i], moe_w_up[i], moe_w_down[i])
    return xf.reshape(n_batch, seq_len, d_model)
```
